```python
import jax, jax.numpy as jnp
from jax import lax
import numpy as np

D_MODEL = 2048
BATCH = 16
SEQ = 2048
DEPTH = 2

N_MIXERS = 2
N_META = 16
D_FF = 5632
FFN_HALF = 0.5
NORM_EPS = 1e-6
ROPE_THETA = 500000.0
ROPE_FRAC = 4
M_HEADS = 4
M_DQK = D_MODEL // (2 * M_HEADS)
M_DV = D_MODEL // M_HEADS
M_CHUNK = 64
LOG_I_PAD = -1e30
A_HEADS = 16
A_KV_HEADS = 4
A_HEAD_DIM = D_MODEL // A_HEADS
IDX_HEADS = 16
IDX_DIM = 64
TOPK_MAX = 256
SPARSE_Q_BLOCK = 64

M_IN = 2 * M_HEADS * M_DQK + 2 * M_HEADS * M_DV + 2 * M_HEADS
A_IN = A_HEADS * A_HEAD_DIM + 2 * A_KV_HEADS * A_HEAD_DIM + IDX_HEADS * IDX_DIM + IDX_DIM + IDX_HEADS
N_A = (DEPTH + 1) // 2
N_B = DEPTH // 2

kernel_name = "hybrid_mlstm_dsa_macaron_meta"


def rmsnorm(x, g):
    xf = x.astype(jnp.float32)
    y = xf * lax.rsqrt(jnp.mean(xf * xf, axis=-1, keepdims=True) + NORM_EPS)
    return (y * g.astype(jnp.float32)).astype(x.dtype)


def swiglu(x, w_gate, w_up, w_down):
    return (jax.nn.silu(x @ w_gate) * (x @ w_up)) @ w_down


def rope_partial(x, pos):
    d = x.shape[-1]
    rot = d // ROPE_FRAC
    half = rot // 2
    inv = 1.0 / (ROPE_THETA ** (jnp.arange(0, rot, 2, dtype=jnp.float32) / rot))
    ang = pos[:, None] * inv[None, :]
    cos = jnp.cos(ang)[:, None, :]
    sin = jnp.sin(ang)[:, None, :]
    xf = x.astype(jnp.float32)
    x1 = xf[..., :half]
    x2 = xf[..., half:rot]
    out = jnp.concatenate([x1 * cos - x2 * sin, x2 * cos + x1 * sin, xf[..., rot:]], axis=-1)
    return out.astype(x.dtype)


def mlstm_mixer(u, w_in, b_i, b_f, g_head, w_out):
    B, T, _ = u.shape
    H, DK, DV = M_HEADS, M_DQK, M_DV
    f32 = jnp.float32
    p = u @ w_in
    cuts = np.cumsum([H * DK, H * DK, H * DV, H * DV, H]).tolist()
    q, k, v, o, ig, fg = jnp.split(p, cuts, axis=-1)

    def heads(a, d):
        return a.reshape(B, T, H, d).transpose(0, 2, 1, 3).astype(f32)

    q = heads(q, DK) * (DK ** -0.5)
    k = heads(k, DK)
    v = heads(v, DV)
    log_i = (ig.astype(f32) + b_i.astype(f32)).transpose(0, 2, 1)
    log_f = jax.nn.log_sigmoid(fg.astype(f32) + b_f.astype(f32)).transpose(0, 2, 1)

    pad = (-N_META) % M_CHUNK
    pw = ((0, 0), (0, 0), (pad, 0), (0, 0))
    q = jnp.pad(q, pw)
    k = jnp.pad(k, pw)
    v = jnp.pad(v, pw)
    log_i = jnp.pad(log_i, pw[:3], constant_values=LOG_I_PAD)
    log_f = jnp.pad(log_f, pw[:3])
    Tp = T + pad
    L = M_CHUNK
    NC = Tp // L

    def chunks(a):
        return jnp.moveaxis(a.reshape(B, H, NC, L, *a.shape[3:]), 2, 0)

    causal = jnp.tril(jnp.ones((L, L), dtype=bool))

    def step(carry, xs):
        C, n, m = carry
        qc, kc, vc, li, lf = xs
        b = jnp.cumsum(lf, axis=-1)
        Dm = jnp.where(causal, b[..., :, None] - b[..., None, :] + li[..., None, :], -jnp.inf)
        inter = b + m[..., None]
        m_t = jnp.maximum(inter, jnp.max(Dm, axis=-1))
        s = jnp.einsum('bhld,bhsd->bhls', qc, kc) * jnp.exp(Dm - m_t[..., None])
        w_inter = jnp.exp(inter - m_t)
        num = w_inter[..., None] * jnp.einsum('bhld,bhde->bhle', qc, C) + jnp.einsum('bhls,bhse->bhle', s, vc)
        den = w_inter * jnp.einsum('bhld,bhd->bhl', qc, n) + jnp.sum(s, axis=-1)
        h_out = num / jnp.maximum(jnp.abs(den), jnp.exp(-m_t))[..., None]
        bL = b[..., -1]
        g = bL[..., None] - b + li
        m_new = jnp.maximum(bL + m, jnp.max(g, axis=-1))
        decay = jnp.exp(bL + m - m_new)
        wk = jnp.exp(g - m_new[..., None])
        C_new = decay[..., None, None] * C + jnp.einsum('bhld,bhle->bhde', kc * wk[..., None], vc)
        n_new = decay[..., None] * n + jnp.einsum('bhl,bhld->bhd', wk, kc)
        return (C_new, n_new, m_new), h_out

    init = (jnp.zeros((B, H, DK, DV), f32), jnp.zeros((B, H, DK), f32), jnp.zeros((B, H), f32))
    _, hs = lax.scan(step, init, (chunks(q), chunks(k), chunks(v), chunks(log_i), chunks(log_f)))
    hs = jnp.moveaxis(hs, 0, 2).reshape(B, H, Tp, DV)[:, :, pad:].transpose(0, 2, 1, 3)
    hn = rmsnorm(hs, g_head)
    og = jax.nn.sigmoid(o.astype(f32)).reshape(B, T, H, DV)
    y = (og * hn).reshape(B, T, H * DV).astype(u.dtype)
    return y @ w_out


def dsa_mixer(u, w_in, g_q, g_k, w_out):
    B, T, _ = u.shape
    G, R, hd = A_KV_HEADS, A_HEADS // A_KV_HEADS, A_HEAD_DIM
    f32 = jnp.float32
    p = u @ w_in
    cuts = np.cumsum([A_HEADS * hd, G * hd, G * hd, IDX_HEADS * IDX_DIM, IDX_DIM]).tolist()
    q, k, v, qi, ki, wi = jnp.split(p, cuts, axis=-1)
    pos = jnp.arange(T, dtype=f32)
    q = rope_partial(rmsnorm(q.reshape(B, T, A_HEADS, hd), g_q), pos)
    k = rope_partial(rmsnorm(k.reshape(B, T, G, hd), g_k), pos)
    v = v.reshape(B, T, G, hd)
    qi = rope_partial(qi.reshape(B, T, IDX_HEADS, IDX_DIM), pos)
    ki = rope_partial(ki.reshape(B, T, 1, IDX_DIM), pos)[:, :, 0]
    wi = wi.astype(f32) * (IDX_HEADS ** -0.5 * IDX_DIM ** -0.5)

    topk = min(TOPK_MAX, (T - N_META) // 4)
    QB = SPARSE_Q_BLOCK
    nb = -(-T // QB)
    Tq = nb * QB

    def qblocks(a):
        a = jnp.pad(a, [(0, 0), (0, Tq - T)] + [(0, 0)] * (a.ndim - 2))
        return jnp.moveaxis(a.reshape(B, nb, QB, *a.shape[2:]), 1, 0)

    key_pos = jnp.arange(T)
    q_pos = jnp.arange(Tq).reshape(nb, QB)

    def gather(a, idx):
        return jax.vmap(lambda ab, ib: ab[ib])(a, idx)

    def attend(args):
        qb, qib, wib, tb = args
        sc = jnp.einsum('bqhd,bsd->bqhs', qib, ki, preferred_element_type=f32)
        score = jnp.einsum('bqhs,bqh->bqs', jax.nn.relu(sc), wib)
        admissible = key_pos[None, :] <= tb[:, None]
        score = jnp.where(admissible[None], score, -jnp.inf)
        _, idx = lax.top_k(score, topk)
        valid = idx <= tb[None, :, None]
        k_sel = gather(k, idx)
        v_sel = gather(v, idx)
        qg = qb.reshape(B, QB, G, R, hd)
        logits = jnp.einsum('bqgrd,bqkgd->bqgrk', qg, k_sel, preferred_element_type=f32) * (hd ** -0.5)
        logits = jnp.where(valid[:, :, None, None, :], logits, -jnp.inf)
        prob = jax.nn.softmax(logits, axis=-1).astype(v.dtype)
        return jnp.einsum('bqgrk,bqkgd->bqgrd', prob, v_sel).reshape(B, QB, A_HEADS * hd)

    o = lax.map(attend, (qblocks(q), qblocks(qi), qblocks(wi), q_pos))
    o = jnp.moveaxis(o, 0, 1).reshape(B, Tq, A_HEADS * hd)[:, :T]
    return o @ w_out


def setup_inputs(seed: int = 0) -> dict:
    key = jax.random.key(seed)
    ks = jax.random.split(key, 24)
    f32 = jnp.float32

    def nrm(k, shape, scale):
        return jax.random.normal(k, shape, f32) * scale

    def gain(k, shape):
        return 1.0 + 0.05 * jax.random.normal(k, shape, f32)

    D, F = D_MODEL, D_FF
    return {
        "x": nrm(ks[0], (BATCH, SEQ, D), 1.0),
        "meta_tokens": nrm(ks[1], (N_META, D), 1.0),
        "ffn1_norm": gain(ks[2], (DEPTH, D)),
        "ffn1_w_gate": nrm(ks[3], (DEPTH, D, F), D ** -0.5),
        "ffn1_w_up": nrm(ks[4], (DEPTH, D, F), D ** -0.5),
        "ffn1_w_down": nrm(ks[5], (DEPTH, F, D), F ** -0.5),
        "mix_norm": gain(ks[6], (DEPTH, D)),
        "ffn2_norm": gain(ks[7], (DEPTH, D)),
        "ffn2_w_gate": nrm(ks[8], (DEPTH, D, F), D ** -0.5),
        "ffn2_w_up": nrm(ks[9], (DEPTH, D, F), D ** -0.5),
        "ffn2_w_down": nrm(ks[10], (DEPTH, F, D), F ** -0.5),
        "mlstm_w_in": nrm(ks[11], (N_A, D, M_IN), D ** -0.5),
        "mlstm_b_i": nrm(ks[12], (N_A, M_HEADS), 0.1),
        "mlstm_b_f": 3.0 + nrm(ks[13], (N_A, M_HEADS), 0.5),
        "mlstm_head_norm": gain(ks[14], (N_A, M_HEADS, M_DV)),
        "mlstm_w_out": nrm(ks[15], (N_A, M_HEADS * M_DV, D), (M_HEADS * M_DV) ** -0.5),
        "dsa_w_in": nrm(ks[16], (N_B, D, A_IN), D ** -0.5),
        "dsa_q_norm": gain(ks[17], (N_B, A_HEAD_DIM)),
        "dsa_k_norm": gain(ks[18], (N_B, A_HEAD_DIM)),
        "dsa_w_out": nrm(ks[19], (N_B, A_HEADS * A_HEAD_DIM, D), (A_HEADS * A_HEAD_DIM) ** -0.5),
    }


def reference(x, meta_tokens, ffn1_norm, ffn1_w_gate, ffn1_w_up, ffn1_w_down, mix_norm,
              ffn2_norm, ffn2_w_gate, ffn2_w_up, ffn2_w_down,
              mlstm_w_in, mlstm_b_i, mlstm_b_f, mlstm_head_norm, mlstm_w_out,
              dsa_w_in, dsa_q_norm, dsa_k_norm, dsa_w_out):
    B = x.shape[0]
    meta = jnp.broadcast_to(meta_tokens[None].astype(x.dtype), (B, N_META, D_MODEL))
    h = jnp.concatenate([meta, x], axis=1)
    for layer in range(DEPTH):
        h = h + FFN_HALF * swiglu(rmsnorm(h, ffn1_norm[layer]), ffn1_w_gate[layer], ffn1_w_up[layer], ffn1_w_down[layer])
        u = rmsnorm(h, mix_norm[layer])
        j = layer // N_MIXERS
        if layer % N_MIXERS == 0:
            mix = mlstm_mixer(u, mlstm_w_in[j], mlstm_b_i[j], mlstm_b_f[j], mlstm_head_norm[j], mlstm_w_out[j])
        else:
            mix = dsa_mixer(u, dsa_w_in[j], dsa_q_norm[j], dsa_k_norm[j], dsa_w_out[j])
        h = h + mix
        h = h + FFN_HALF * swiglu(rmsnorm(h, ffn2_norm[layer]), ffn2_w_gate[layer], ffn2_w_up[layer], ffn2_w_down[layer])
    return h[:, N_META:]
```

```python
import functools

import numpy as np
import jax
import jax.numpy as jnp
from jax import lax
from jax.experimental import pallas as pl
from jax.experimental.pallas import tpu as pltpu

F32 = jnp.float32
BF16 = jnp.bfloat16

N_META = 16
FFN_HALF = 0.5
NORM_EPS = 1e-6
ROPE_THETA = 500000.0
ROPE_FRAC = 4
M_HEADS = 4
A_HEADS = 16
A_KV_HEADS = 4
IDX_HEADS = 16
IDX_DIM = 64
TOPK_MAX = 256
LOG_I_PAD = -1e30

LANES = 128
VMEM_LIMIT = 56 * 1024 * 1024
INT_MIN = int(np.iinfo(np.int32).min)
NEG_INF_KEY = int(np.array(-np.inf, np.float32).view(np.int32) ^ np.int32(0x7FFFFFFF))


def _params(*sem):
    return pltpu.CompilerParams(dimension_semantics=sem, vmem_limit_bytes=VMEM_LIMIT)


def _rms(x, g):
    return x * lax.rsqrt(jnp.mean(x * x, axis=-1, keepdims=True) + NORM_EPS) * g


def _dot(a, b):
    return jnp.dot(a, b, preferred_element_type=F32)


def _dot_nt(a, b):
    return lax.dot_general(a, b, (((1,), (1,)), ((), ())), preferred_element_type=F32)


def _ffn_kernel(h_ref, g_ref, wg_ref, wu_ref, wd_ref, o_ref, xn_ref):
    @pl.when(pl.program_id(1) == 0)
    def _():
        h = h_ref[...]
        xn_ref[...] = _rms(h, g_ref[...]).astype(BF16)
        o_ref[...] = h

    xn = xn_ref[...]
    gate = _dot(xn, wg_ref[...])
    up = _dot(xn, wu_ref[...])
    act = (gate * jax.nn.sigmoid(gate)) * (up * FFN_HALF)
    o_ref[...] += _dot(act.astype(BF16), wd_ref[...])


def _ffn(h, g, wg, wu, wd, tm, tf):
    M, D = h.shape
    F = wg.shape[1]
    return pl.pallas_call(
        _ffn_kernel,
        grid=(M // tm, F // tf),
        in_specs=[
            pl.BlockSpec((tm, D), lambda i, j: (i, 0)),
            pl.BlockSpec((1, D), lambda i, j: (0, 0)),
            pl.BlockSpec((D, tf), lambda i, j: (0, j)),
            pl.BlockSpec((D, tf), lambda i, j: (0, j)),
            pl.BlockSpec((tf, D), lambda i, j: (j, 0)),
        ],
        out_specs=pl.BlockSpec((tm, D), lambda i, j: (i, 0)),
        out_shape=jax.ShapeDtypeStruct((M, D), F32),
        scratch_shapes=[pltpu.VMEM((tm, D), BF16)],
        compiler_params=_params("parallel", "arbitrary"),
        name="ffn",
    )(h, g.reshape(1, D), wg, wu, wd)


def _inproj_m_kernel(h_ref, g_ref, w_ref, ws_ref, p_ref, ps_ref, xn_ref):
    @pl.when(pl.program_id(1) == 0)
    def _():
        xn = _rms(h_ref[...], g_ref[...]).astype(BF16)
        xn_ref[...] = xn
        ps_ref[...] = _dot(xn, ws_ref[...])

    p_ref[...] = _dot(xn_ref[...], w_ref[...]).astype(BF16)


def _inproj_m(h, g, w, ws, tm, tn):
    M, D = h.shape
    N = w.shape[1]
    return pl.pallas_call(
        _inproj_m_kernel,
        grid=(M // tm, N // tn),
        in_specs=[
            pl.BlockSpec((tm, D), lambda i, j: (i, 0)),
            pl.BlockSpec((1, D), lambda i, j: (0, 0)),
            pl.BlockSpec((D, tn), lambda i, j: (0, j)),
            pl.BlockSpec((D, LANES), lambda i, j: (0, 0)),
        ],
        out_specs=[
            pl.BlockSpec((tm, tn), lambda i, j: (i, j)),
            pl.BlockSpec((tm, LANES), lambda i, j: (i, 0)),
        ],
        out_shape=[jax.ShapeDtypeStruct((M, N), BF16), jax.ShapeDtypeStruct((M, LANES), F32)],
        scratch_shapes=[pltpu.VMEM((tm, D), BF16)],
        compiler_params=_params("parallel", "arbitrary"),
        name="mlstm_inproj",
    )(h, g.reshape(1, D), w, ws)


def _log_sigmoid(x):
    return jnp.minimum(x, 0.0) - jnp.log1p(jnp.exp(-jnp.abs(x)))


def _mlstm_kernel(q_ref, k_ref, v_ref, o_ref, ps_ref, bias_ref, gh_ref, c0_ref, n0_ref, m0_ref,
                  y_ref, c_ref, n_ref, m_ref, *, L, npad, H, DK, DV):
    @pl.when(pl.program_id(1) == 0)
    def _():
        c_ref[...] = c0_ref[...]
        n_ref[...] = n0_ref[...]
        m_ref[...] = m0_ref[...]

    gates = ps_ref[...] + bias_ref[...]
    logf = _log_sigmoid(gates)
    if npad:
        valid = lax.broadcasted_iota(jnp.int32, (L, 1), 0) >= npad
        logf = jnp.where(valid, logf, 0.0)
        gates = jnp.where(valid, gates, LOG_I_PAD)
    ii = lax.broadcasted_iota(jnp.int32, (L, L), 0)
    jj = lax.broadcasted_iota(jnp.int32, (L, L), 1)
    causal = jj <= ii
    cum = jnp.dot(causal.astype(F32), logf, preferred_element_type=F32,
                  precision=lax.Precision.HIGHEST)
    cum_t = cum.T
    gates_t = gates.T

    for h in range(H):
        b_col = cum[:, H + h:H + h + 1]
        b_row = cum_t[H + h:H + h + 1, :]
        li_col = gates[:, h:h + 1]
        li_row = gates_t[h:h + 1, :]
        m_prev = m_ref[h][:, 0:1]
        q = q_ref[:, h * DK:(h + 1) * DK] * (DK ** -0.5)
        k = k_ref[:, h * DK:(h + 1) * DK]
        v = v_ref[:, h * DV:(h + 1) * DV]
        c_old = c_ref[h]
        n_old = n_ref[h]

        dm = jnp.where(causal, b_col - b_row + li_row, -jnp.inf)
        inter = b_col + m_prev
        m_t = jnp.maximum(inter, jnp.max(dm, axis=-1, keepdims=True))
        s = _dot_nt(q, k) * jnp.exp(dm - m_t)
        w_inter = jnp.exp(inter - m_t)
        num = w_inter * _dot(q, c_old.astype(BF16)) + _dot(s.astype(BF16), v)
        qn = jnp.sum(q.astype(F32) * n_old, axis=-1, keepdims=True)
        den = w_inter * qn + jnp.sum(s, axis=-1, keepdims=True)
        hout = num / jnp.maximum(jnp.abs(den), jnp.exp(-m_t))

        b_last = b_col[L - 1:L, :]
        g_row = b_last - b_row + li_row
        g_col = b_last - b_col + li_col
        m_new = jnp.maximum(b_last + m_prev, jnp.max(g_row, axis=-1, keepdims=True))
        decay = jnp.exp(b_last + m_prev - m_new)
        kw = k.astype(F32) * jnp.exp(g_col - m_new)
        c_ref[h] = decay * c_old + _dot(kw.T.astype(BF16), v)
        n_ref[h] = decay * n_old + jnp.sum(kw, axis=0, keepdims=True)
        m_ref[h] = jnp.broadcast_to(m_new, (1, LANES))

        hn = _rms(hout, gh_ref[:, h * DV:(h + 1) * DV])
        og = jax.nn.sigmoid(o_ref[:, h * DV:(h + 1) * DV].astype(F32))
        y_ref[:, h * DV:(h + 1) * DV] = (og * hn).astype(BF16)


def _mlstm(p, ps, bias, gh, c0, n0, m0, B, L, npad):
    M = p.shape[0]
    H = M_HEADS
    DK, DV = c0.shape[1], c0.shape[2]
    NC = M // (B * L)
    row = lambda b, c: b * NC + c
    qk_w, vo_w = H * DK, H * DV
    assert vo_w == 2 * qk_w
    kern = functools.partial(_mlstm_kernel, L=L, npad=npad, H=H, DK=DK, DV=DV)
    return pl.pallas_call(
        kern,
        grid=(B, NC),
        in_specs=[
            pl.BlockSpec((L, qk_w), lambda b, c: (row(b, c), 0)),
            pl.BlockSpec((L, qk_w), lambda b, c: (row(b, c), 1)),
            pl.BlockSpec((L, vo_w), lambda b, c: (row(b, c), 1)),
            pl.BlockSpec((L, vo_w), lambda b, c: (row(b, c), 2)),
            pl.BlockSpec((L, LANES), lambda b, c: (row(b, c), 0)),
            pl.BlockSpec((1, LANES), lambda b, c: (0, 0)),
            pl.BlockSpec((1, vo_w), lambda b, c: (0, 0)),
            pl.BlockSpec((H, DK, DV), lambda b, c: (0, 0, 0)),
            pl.BlockSpec((H, 1, DK), lambda b, c: (0, 0, 0)),
            pl.BlockSpec((H, 1, LANES), lambda b, c: (0, 0, 0)),
        ],
        out_specs=[
            pl.BlockSpec((L, vo_w), lambda b, c: (row(b, c), 0)),
            pl.BlockSpec((None, H, DK, DV), lambda b, c: (b, 0, 0, 0)),
            pl.BlockSpec((None, H, 1, DK), lambda b, c: (b, 0, 0, 0)),
            pl.BlockSpec((None, H, 1, LANES), lambda b, c: (b, 0, 0, 0)),
        ],
        out_shape=[
            jax.ShapeDtypeStruct((M, vo_w), BF16),
            jax.ShapeDtypeStruct((B, H, DK, DV), F32),
            jax.ShapeDtypeStruct((B, H, 1, DK), F32),
            jax.ShapeDtypeStruct((B, H, 1, LANES), F32),
        ],
        compiler_params=_params("parallel", "arbitrary"),
        name="mlstm",
    )(p, p, p, p, ps, bias, gh, c0, n0, m0)


def _outproj_kernel(y_ref, w_ref, h_ref, o_ref):
    o_ref[...] = h_ref[...] + _dot(y_ref[...], w_ref[...])


def _outproj(y, w, h, tm):
    M, D = h.shape
    K = y.shape[1]
    return pl.pallas_call(
        _outproj_kernel,
        grid=(M // tm,),
        in_specs=[
            pl.BlockSpec((tm, K), lambda i: (i, 0)),
            pl.BlockSpec((K, D), lambda i: (0, 0)),
            pl.BlockSpec((tm, D), lambda i: (i, 0)),
        ],
        out_specs=pl.BlockSpec((tm, D), lambda i: (i, 0)),
        out_shape=jax.ShapeDtypeStruct((M, D), F32),
        compiler_params=_params("parallel"),
        name="outproj",
    )(y, w, h)


def _rope(x, tab, half):
    c = tab[:, 0:LANES]
    s1 = tab[:, LANES:2 * LANES]
    s2 = tab[:, 2 * LANES:3 * LANES]
    return x * c + pltpu.roll(x, LANES - half, 1) * s1 + pltpu.roll(x, half, 1) * s2


def _inproj_a_kernel(h_ref, g_ref, w_ref, ws_ref, gq_ref, gk_ref, t128_ref, t64_ref, tki_ref,
                     p_ref, ps_ref, xn_ref, *, nq_tiles, tn, half128, half64):
    j = pl.program_id(1)

    @pl.when(j == 0)
    def _():
        xn = _rms(h_ref[...], g_ref[...]).astype(BF16)
        xn_ref[...] = xn
        ps_ref[...] = _rope(_dot(xn, ws_ref[...]), tki_ref[...], half64)

    acc = _dot(xn_ref[...], w_ref[...])
    nh = tn // LANES

    @pl.when(j < nq_tiles)
    def _():
        t = t128_ref[...]
        for c in range(nh):
            xs = _rms(acc[:, c * LANES:(c + 1) * LANES], gq_ref[...])
            p_ref[:, c * LANES:(c + 1) * LANES] = _rope(xs, t, half128).astype(BF16)

    @pl.when(j == nq_tiles)
    def _():
        t = t128_ref[...]
        for c in range(nh // 2):
            xs = _rms(acc[:, c * LANES:(c + 1) * LANES], gk_ref[...])
            p_ref[:, c * LANES:(c + 1) * LANES] = _rope(xs, t, half128).astype(BF16)
        p_ref[:, tn // 2:] = acc[:, tn // 2:].astype(BF16)

    @pl.when(j == nq_tiles + 1)
    def _():
        t = t64_ref[...]
        for c in range(nh):
            p_ref[:, c * LANES:(c + 1) * LANES] = _rope(acc[:, c * LANES:(c + 1) * LANES], t, half64).astype(BF16)


def _inproj_a(h, g, w, ws, gq, gk, t128, t64, tki, tm):
    M, D = h.shape
    N = w.shape[1]
    tn = 1024
    hd = D // A_HEADS
    assert hd == LANES and D // 4 * 2 == tn and N == D + 2 * tn
    nt = t128.shape[0] // tm
    kern = functools.partial(_inproj_a_kernel, nq_tiles=D // tn, tn=tn,
                             half128=hd // ROPE_FRAC // 2, half64=IDX_DIM // ROPE_FRAC // 2)
    tab = pl.BlockSpec((tm, 3 * LANES), lambda i, j: (i % nt, 0))
    return pl.pallas_call(
        kern,
        grid=(M // tm, N // tn),
        in_specs=[
            pl.BlockSpec((tm, D), lambda i, j: (i, 0)),
            pl.BlockSpec((1, D), lambda i, j: (0, 0)),
            pl.BlockSpec((D, tn), lambda i, j: (0, j)),
            pl.BlockSpec((D, LANES), lambda i, j: (0, 0)),
            pl.BlockSpec((1, LANES), lambda i, j: (0, 0)),
            pl.BlockSpec((1, LANES), lambda i, j: (0, 0)),
            tab, tab, tab,
        ],
        out_specs=[
            pl.BlockSpec((tm, tn), lambda i, j: (i, j)),
            pl.BlockSpec((tm, LANES), lambda i, j: (i, 0)),
        ],
        out_shape=[jax.ShapeDtypeStruct((M, N), BF16), jax.ShapeDtypeStruct((M, LANES), F32)],
        scratch_shapes=[pltpu.VMEM((tm, D), BF16)],
        compiler_params=_params("parallel", "arbitrary"),
        name="dsa_inproj",
    )(h, g.reshape(1, D), w, ws, gq, gk, t128, t64, tki)


def _rope_table(pos, d, extra_scale_lanes=None, extra_scale=1.0, reps=None):
    rot = d // ROPE_FRAC
    half = rot // 2
    inv = 1.0 / (ROPE_THETA ** (jnp.arange(0, rot, 2, dtype=F32) / rot))
    ang = pos[:, None] * inv[None, :]
    cos, sin = jnp.cos(ang), jnp.sin(ang)
    T = pos.shape[0]
    one = jnp.ones((T, d - rot), F32)
    zero = lambda n: jnp.zeros((T, n), F32)
    c = jnp.concatenate([cos, cos, one], axis=1)
    s1 = jnp.concatenate([-sin, zero(d - half)], axis=1)
    s2 = jnp.concatenate([zero(half), sin, zero(d - rot)], axis=1)
    if reps is None:
        reps = LANES // d
    c, s1, s2 = (jnp.tile(a, (1, reps)) for a in (c, s1, s2))
    fill = LANES - reps * d
    if fill:
        tail = jnp.ones((T, fill), F32)
        if extra_scale_lanes is not None:
            lo, hi = extra_scale_lanes
            lane = jnp.arange(reps * d, LANES)
            tail = jnp.where((lane >= lo) & (lane < hi), extra_scale, 1.0)[None, :] * tail
        c = jnp.concatenate([c, tail], axis=1)
        s1 = jnp.concatenate([s1, zero(fill)], axis=1)
        s2 = jnp.concatenate([s2, zero(fill)], axis=1)
    return jnp.concatenate([c, s1, s2], axis=1)


def _float_key(x):
    bits = lax.bitcast_convert_type(x, jnp.int32)
    return jnp.where(bits < 0, bits ^ jnp.int32(0x7FFFFFFF), bits)


def _dsa_kernel(q_ref, qi_ref, psq_ref, k_ref, v_ref, psk_ref, km_ref, vm_ref, psm_ref, o_ref,
                kcat, vcat, kie, kio, key_ref, bias_ref, *, tq, S, topk, n_meta):
    i = pl.program_id(1)
    NK = LANES + S
    NCH = NK // LANES
    hd = LANES
    G = A_KV_HEADS
    R = A_HEADS // G

    @pl.when(i == 0)
    def _():
        kcat[0:LANES, :] = km_ref[...]
        kcat[LANES:, :] = k_ref[...]
        vcat[0:LANES, :] = vm_ref[...]
        vcat[LANES:, :] = v_ref[...]
        for dst0, dst1, src in ((0, LANES, psm_ref), (LANES, NK, psk_ref)):
            a = src[...]
            lane = lax.broadcasted_iota(jnp.int32, a.shape, 1)
            even = jnp.where(lane < IDX_DIM, a, 0.0)
            kie[dst0:dst1, :] = even.astype(BF16)
            kio[dst0:dst1, :] = pltpu.roll(even, IDX_DIM, 1).astype(BF16)

    wi = psq_ref[...]
    score = jnp.zeros((tq, NK), F32)
    for p in range(IDX_HEADS // 2):
        qp = qi_ref[:, p * LANES:(p + 1) * LANES]
        c0 = IDX_DIM + 2 * p
        score += jnp.maximum(_dot_nt(qp, kie[...]), 0.0) * wi[:, c0:c0 + 1]
        score += jnp.maximum(_dot_nt(qp, kio[...]), 0.0) * wi[:, c0 + 1:c0 + 2]

    col = lax.broadcasted_iota(jnp.int32, (tq, NK), 1)
    row = lax.broadcasted_iota(jnp.int32, (tq, NK), 0) + i * tq
    adm = jnp.where(col < LANES, col - n_meta, col - LANES - row - 1) < 0
    score = jnp.where(score == 0.0, 0.0, score)
    key_ref[...] = _float_key(jnp.where(adm, score, -jnp.inf))

    ones = jnp.ones((LANES, LANES), BF16)
    kf = float(topk)

    def count_ge(cand):
        acc = jnp.zeros((tq, LANES), F32)
        for c in range(NCH):
            acc += jnp.where(key_ref[:, c * LANES:(c + 1) * LANES] >= cand, 1.0, 0.0)
        return _dot(acc.astype(BF16), ones)

    zero = jnp.zeros((tq, LANES), jnp.int32)
    base = jnp.where(count_ge(zero) >= kf, zero, INT_MIN)

    def search(it, base):
        cand = base + lax.shift_left(jnp.int32(1), jnp.int32(30) - it.astype(jnp.int32))
        return jnp.where(count_ge(cand) >= kf, cand, base)

    base = lax.fori_loop(0, 31, search, base)
    thr = jnp.maximum(base, NEG_INF_KEY + 1)
    cnt = count_ge(thr)
    for c in range(NCH):
        sl = slice(c * LANES, (c + 1) * LANES)
        bias_ref[:, sl] = jnp.where(key_ref[:, sl] >= thr, 0.0, -jnp.inf)

    @pl.when(jnp.max(cnt) > kf)
    def _():
        need = kf - count_ge(thr + 1)
        rr = lax.broadcasted_iota(jnp.int32, (LANES, LANES), 0)
        cc = lax.broadcasted_iota(jnp.int32, (LANES, LANES), 1)
        tri = jnp.where(rr <= cc, 1.0, 0.0).astype(BF16)
        run = jnp.zeros((tq, LANES), F32)
        for c in range(NCH):
            sl = slice(c * LANES, (c + 1) * LANES)
            kc = key_ref[:, sl]
            eq = kc == thr
            eqb = jnp.where(eq, 1.0, 0.0).astype(BF16)
            rank = run + _dot(eqb, tri)
            keep_eq = jnp.where(rank <= need, 0.0, -jnp.inf)
            bias_ref[:, sl] = jnp.where(kc > thr, 0.0, jnp.where(eq, keep_eq, -jnp.inf))
            run = run + _dot(eqb, ones)

    for g in range(G):
        kg = kcat[:, g * hd:(g + 1) * hd]
        vg = vcat[:, g * hd:(g + 1) * hd]
        for r in range(R):
            hh = g * R + r
            logits = _dot_nt(q_ref[:, hh * hd:(hh + 1) * hd], kg) + bias_ref[...]
            mx = jnp.max(logits, axis=-1, keepdims=True)
            pr = jnp.exp(logits - mx)
            den = jnp.sum(pr, axis=-1, keepdims=True)
            o_ref[:, hh * hd:(hh + 1) * hd] = (_dot(pr.astype(BF16), vg) / den).astype(BF16)


def _dsa(p, ps, pm, psm, B, S, tq, topk):
    M, N = p.shape
    D = N // 2
    kv_w = D // 4
    nq = S // tq
    NK = LANES + S
    kern = functools.partial(_dsa_kernel, tq=tq, S=S, topk=topk, n_meta=N_META)
    return pl.pallas_call(
        kern,
        grid=(B, nq),
        in_specs=[
            pl.BlockSpec((tq, D), lambda b, i: (b * nq + i, 0)),
            pl.BlockSpec((tq, 1024), lambda b, i: (b * nq + i, (D + 2 * kv_w) // 1024)),
            pl.BlockSpec((tq, LANES), lambda b, i: (b * nq + i, 0)),
            pl.BlockSpec((S, kv_w), lambda b, i: (b, D // kv_w)),
            pl.BlockSpec((S, kv_w), lambda b, i: (b, D // kv_w + 1)),
            pl.BlockSpec((S, LANES), lambda b, i: (b, 0)),
            pl.BlockSpec((LANES, kv_w), lambda b, i: (0, D // kv_w)),
            pl.BlockSpec((LANES, kv_w), lambda b, i: (0, D // kv_w + 1)),
            pl.BlockSpec((LANES, LANES), lambda b, i: (0, 0)),
        ],
        out_specs=pl.BlockSpec((tq, D), lambda b, i: (b * nq + i, 0)),
        out_shape=jax.ShapeDtypeStruct((M, D), BF16),
        scratch_shapes=[
            pltpu.VMEM((NK, kv_w), BF16),
            pltpu.VMEM((NK, kv_w), BF16),
            pltpu.VMEM((NK, LANES), BF16),
            pltpu.VMEM((NK, LANES), BF16),
            pltpu.VMEM((tq, NK), jnp.int32),
            pltpu.VMEM((tq, NK), F32),
        ],
        compiler_params=_params("parallel", "arbitrary"),
        name="dsa_attention",
    )(p, p, ps, p, p, ps, pm, pm, psm)


def _row_tile(m, pref):
    return pref if m % pref == 0 else m


def kernel(x, meta_tokens, ffn1_norm, ffn1_w_gate, ffn1_w_up, ffn1_w_down, mix_norm, ffn2_norm, ffn2_w_gate, ffn2_w_up, ffn2_w_down, mlstm_w_in, mlstm_b_i, mlstm_b_f, mlstm_head_norm, mlstm_w_out, dsa_w_in, dsa_q_norm, dsa_k_norm, dsa_w_out):
    B, S, D = x.shape
    depth = ffn1_norm.shape[0]
    assert depth == 2 and meta_tokens.shape == (N_META, D)
    F = ffn1_w_gate.shape[-1]
    H, DK, DV = M_HEADS, D // (2 * M_HEADS), D // M_HEADS
    hd = D // A_HEADS
    L = 256
    TM = 512
    TF = 512
    assert S % L == 0 and (B * S) % TM == 0 and F % TF == 0

    hx = x.reshape(B * S, D)
    hm = meta_tokens.astype(x.dtype)
    bf = lambda w: w.astype(BF16)

    def ffn(h, g, wg, wu, wd):
        return _ffn(h, g, wg, wu, wd, _row_tile(h.shape[0], TM), TF)

    def pad_cols(w):
        return jnp.pad(w, ((0, 0), (0, LANES - w.shape[1])))

    def pad_rows(a, n, front=False):
        r = n - a.shape[0]
        return jnp.pad(a, ((r, 0) if front else (0, r), (0, 0)))

    w = [bf(ffn1_w_gate[0]), bf(ffn1_w_up[0]), bf(ffn1_w_down[0])]
    hx = ffn(hx, ffn1_norm[0], *w)
    hm = ffn(hm, ffn1_norm[0], *w)

    n_wide = 2 * H * DK + 2 * H * DV
    w_wide = bf(mlstm_w_in[0][:, :n_wide])
    w_gate = bf(pad_cols(mlstm_w_in[0][:, n_wide:]))
    px, psx = _inproj_m(hx, mix_norm[0], w_wide, w_gate, TM, 1024)
    pm, psm = _inproj_m(hm, mix_norm[0], w_wide, w_gate, N_META, 1024)

    bias = pad_cols(jnp.concatenate([mlstm_b_i[0], mlstm_b_f[0]]).astype(F32)[None, :])
    gh = mlstm_head_norm[0].astype(F32).reshape(1, H * DV)
    zc = jnp.zeros((H, DK, DV), F32)
    zn = jnp.zeros((H, 1, DK), F32)
    zm = jnp.zeros((H, 1, LANES), F32)
    ym, c0, n0, m0 = _mlstm(pad_rows(pm, L, True), pad_rows(psm, L, True), bias, gh, zc, zn, zm,
                            1, L, L - N_META)
    yx, _, _, _ = _mlstm(px, psx, bias, gh, c0[0], n0[0], m0[0], B, L, 0)
    w_out = bf(mlstm_w_out[0])
    hx = _outproj(yx, w_out, hx, TM)
    hm = _outproj(ym[L - N_META:], w_out, hm, N_META)

    w = [bf(ffn2_w_gate[0]), bf(ffn2_w_up[0]), bf(ffn2_w_down[0])]
    hx = ffn(hx, ffn2_norm[0], *w)
    hm = ffn(hm, ffn2_norm[0], *w)

    w = [bf(ffn1_w_gate[1]), bf(ffn1_w_up[1]), bf(ffn1_w_down[1])]
    hx = ffn(hx, ffn1_norm[1], *w)
    hm = ffn(hm, ffn1_norm[1], *w)

    n_wide = A_HEADS * hd + 2 * A_KV_HEADS * hd + IDX_HEADS * IDX_DIM
    w_wide = bf(dsa_w_in[0][:, :n_wide])
    w_idx = bf(pad_cols(dsa_w_in[0][:, n_wide:]))
    gq = (dsa_q_norm[0].astype(F32) * hd ** -0.5)[None, :]
    gk = dsa_k_norm[0].astype(F32)[None, :]
    wi_scale = IDX_HEADS ** -0.5 * IDX_DIM ** -0.5
    pos = jnp.arange(N_META + S, dtype=F32)
    t128 = _rope_table(pos, hd)
    t64 = _rope_table(pos, IDX_DIM)
    tki = _rope_table(pos, IDX_DIM, (IDX_DIM, IDX_DIM + IDX_HEADS), wi_scale, reps=1)
    tabs_x = [t[N_META:] for t in (t128, t64, tki)]
    tabs_m = [t[:N_META] for t in (t128, t64, tki)]
    px, psx = _inproj_a(hx, mix_norm[1], w_wide, w_idx, gq, gk, *tabs_x, TM)
    pm, psm = _inproj_a(hm, mix_norm[1], w_wide, w_idx, gq, gk, *tabs_m, N_META)

    topk = min(TOPK_MAX, (N_META + S - N_META) // 4)
    ox = _dsa(px, psx, pad_rows(pm, LANES), pad_rows(psm, LANES), B, S, 256, topk)
    hx = _outproj(ox, bf(dsa_w_out[0]), hx, TM)

    w = [bf(ffn2_w_gate[1]), bf(ffn2_w_up[1]), bf(ffn2_w_down[1])]
    hx = ffn(hx, ffn2_norm[1], *w)
    return hx.reshape(B, S, D)
```

```python
import functools

import numpy as np
import jax
import jax.numpy as jnp
from jax import lax
from jax.experimental import pallas as pl
from jax.experimental.pallas import tpu as pltpu

F32 = jnp.float32
BF16 = jnp.bfloat16

N_META = 16
FFN_HALF = 0.5
NORM_EPS = 1e-6
ROPE_THETA = 500000.0
ROPE_FRAC = 4
M_HEADS = 4
A_HEADS = 16
A_KV_HEADS = 4
IDX_HEADS = 16
IDX_DIM = 64
TOPK_MAX = 256
LOG_I_PAD = -1e30

LANES = 128
VMEM_LIMIT = 56 * 1024 * 1024
INT_MIN = int(np.iinfo(np.int32).min)
NEG_INF_KEY = int(np.array(-np.inf, np.float32).view(np.int32) ^ np.int32(0x7FFFFFFF))


def _params(*sem):
    return pltpu.CompilerParams(dimension_semantics=sem, vmem_limit_bytes=VMEM_LIMIT)


def _row_block(tm, d):
    return pl.BlockSpec((tm, d), lambda i, j: (i, 0), pipeline_mode=pl.Buffered(1))


def _rms(x, g):
    return x * lax.rsqrt(jnp.mean(x * x, axis=-1, keepdims=True) + NORM_EPS) * g


def _dot(a, b):
    return jnp.dot(a, b, preferred_element_type=F32)


def _dot_nt(a, b):
    return lax.dot_general(a, b, (((1,), (1,)), ((), ())), preferred_element_type=F32)


def _ffn_kernel(h_ref, g_ref, wg_ref, wu_ref, wd_ref, o_ref, xn_ref):
    @pl.when(pl.program_id(1) == 0)
    def _():
        h = h_ref[...]
        xn_ref[...] = _rms(h, g_ref[...]).astype(BF16)
        o_ref[...] = h

    xn = xn_ref[...]
    gate = _dot(xn, wg_ref[...])
    up = _dot(xn, wu_ref[...])
    act = (gate * jax.nn.sigmoid(gate)) * (up * FFN_HALF)
    o_ref[...] += _dot(act.astype(BF16), wd_ref[...])


def _ffn(h, g, wg, wu, wd, tm, tf):
    M, D = h.shape
    F = wg.shape[1]
    return pl.pallas_call(
        _ffn_kernel,
        grid=(M // tm, F // tf),
        in_specs=[
            _row_block(tm, D),
            pl.BlockSpec((1, D), lambda i, j: (0, 0)),
            pl.BlockSpec((D, tf), lambda i, j: (0, j)),
            pl.BlockSpec((D, tf), lambda i, j: (0, j)),
            pl.BlockSpec((tf, D), lambda i, j: (j, 0)),
        ],
        out_specs=pl.BlockSpec((tm, D), lambda i, j: (i, 0)),
        out_shape=jax.ShapeDtypeStruct((M, D), F32),
        scratch_shapes=[pltpu.VMEM((tm, D), BF16)],
        compiler_params=_params("parallel", "arbitrary"),
        name="ffn",
    )(h, g.reshape(1, D), wg, wu, wd)


def _inproj_m_kernel(h_ref, g_ref, w_ref, ws_ref, p_ref, ps_ref, xn_ref):
    @pl.when(pl.program_id(1) == 0)
    def _():
        xn = _rms(h_ref[...], g_ref[...]).astype(BF16)
        xn_ref[...] = xn
        ps_ref[...] = _dot(xn, ws_ref[...])

    p_ref[...] = _dot(xn_ref[...], w_ref[...]).astype(BF16)


def _inproj_m(h, g, w, ws, tm, tn):
    M, D = h.shape
    N = w.shape[1]
    return pl.pallas_call(
        _inproj_m_kernel,
        grid=(M // tm, N // tn),
        in_specs=[
            _row_block(tm, D),
            pl.BlockSpec((1, D), lambda i, j: (0, 0)),
            pl.BlockSpec((D, tn), lambda i, j: (0, j)),
            pl.BlockSpec((D, LANES), lambda i, j: (0, 0)),
        ],
        out_specs=[
            pl.BlockSpec((tm, tn), lambda i, j: (i, j)),
            pl.BlockSpec((tm, LANES), lambda i, j: (i, 0)),
        ],
        out_shape=[jax.ShapeDtypeStruct((M, N), BF16), jax.ShapeDtypeStruct((M, LANES), F32)],
        scratch_shapes=[pltpu.VMEM((tm, D), BF16)],
        compiler_params=_params("parallel", "arbitrary"),
        name="mlstm_inproj",
    )(h, g.reshape(1, D), w, ws)


def _log_sigmoid(x):
    return jnp.minimum(x, 0.0) - jnp.log1p(jnp.exp(-jnp.abs(x)))


def _mlstm_kernel(q_ref, k_ref, v_ref, o_ref, ps_ref, bias_ref, gh_ref, c0_ref, n0_ref, m0_ref,
                  y_ref, c_ref, n_ref, m_ref, *, L, npad, H, DK, DV):
    @pl.when(pl.program_id(1) == 0)
    def _():
        c_ref[...] = c0_ref[...]
        n_ref[...] = n0_ref[...]
        m_ref[...] = m0_ref[...]

    gates = ps_ref[...] + bias_ref[...]
    logf = _log_sigmoid(gates)
    if npad:
        valid = lax.broadcasted_iota(jnp.int32, (L, 1), 0) >= npad
        logf = jnp.where(valid, logf, 0.0)
        gates = jnp.where(valid, gates, LOG_I_PAD)
    ii = lax.broadcasted_iota(jnp.int32, (L, L), 0)
    jj = lax.broadcasted_iota(jnp.int32, (L, L), 1)
    causal = jj <= ii
    cum = jnp.dot(causal.astype(F32), logf, preferred_element_type=F32,
                  precision=lax.Precision.HIGHEST)
    cum_t = cum.T
    gates_t = gates.T

    for h in range(H):
        b_col = cum[:, H + h:H + h + 1]
        b_row = cum_t[H + h:H + h + 1, :]
        li_col = gates[:, h:h + 1]
        li_row = gates_t[h:h + 1, :]
        m_prev = m_ref[h][:, 0:1]
        q = q_ref[:, h * DK:(h + 1) * DK] * (DK ** -0.5)
        k = k_ref[:, h * DK:(h + 1) * DK]
        v = v_ref[:, h * DV:(h + 1) * DV]
        c_old = c_ref[h]
        n_old = n_ref[h]

        dm = jnp.where(causal, b_col - b_row + li_row, -jnp.inf)
        inter = b_col + m_prev
        m_t = jnp.maximum(inter, jnp.max(dm, axis=-1, keepdims=True))
        s = _dot_nt(q, k) * jnp.exp(dm - m_t)
        w_inter = jnp.exp(inter - m_t)
        num = w_inter * _dot(q, c_old.astype(BF16)) + _dot(s.astype(BF16), v)
        qn = jnp.sum(q.astype(F32) * n_old, axis=-1, keepdims=True)
        den = w_inter * qn + jnp.sum(s, axis=-1, keepdims=True)
        hout = num / jnp.maximum(jnp.abs(den), jnp.exp(-m_t))

        b_last = b_col[L - 1:L, :]
        g_row = b_last - b_row + li_row
        g_col = b_last - b_col + li_col
        m_new = jnp.maximum(b_last + m_prev, jnp.max(g_row, axis=-1, keepdims=True))
        decay = jnp.exp(b_last + m_prev - m_new)
        kw = k.astype(F32) * jnp.exp(g_col - m_new)
        c_ref[h] = decay * c_old + _dot(kw.T.astype(BF16), v)
        n_ref[h] = decay * n_old + jnp.sum(kw, axis=0, keepdims=True)
        m_ref[h] = jnp.broadcast_to(m_new, (1, LANES))

        hn = _rms(hout, gh_ref[:, h * DV:(h + 1) * DV])
        og = jax.nn.sigmoid(o_ref[:, h * DV:(h + 1) * DV].astype(F32))
        y_ref[:, h * DV:(h + 1) * DV] = (og * hn).astype(BF16)


def _mlstm(p, ps, bias, gh, c0, n0, m0, B, L, npad):
    M = p.shape[0]
    H = M_HEADS
    DK, DV = c0.shape[1], c0.shape[2]
    NC = M // (B * L)
    row = lambda b, c: b * NC + c
    qk_w, vo_w = H * DK, H * DV
    assert vo_w == 2 * qk_w
    kern = functools.partial(_mlstm_kernel, L=L, npad=npad, H=H, DK=DK, DV=DV)
    return pl.pallas_call(
        kern,
        grid=(B, NC),
        in_specs=[
            pl.BlockSpec((L, qk_w), lambda b, c: (row(b, c), 0)),
            pl.BlockSpec((L, qk_w), lambda b, c: (row(b, c), 1)),
            pl.BlockSpec((L, vo_w), lambda b, c: (row(b, c), 1)),
            pl.BlockSpec((L, vo_w), lambda b, c: (row(b, c), 2)),
            pl.BlockSpec((L, LANES), lambda b, c: (row(b, c), 0)),
            pl.BlockSpec((1, LANES), lambda b, c: (0, 0)),
            pl.BlockSpec((1, vo_w), lambda b, c: (0, 0)),
            pl.BlockSpec((H, DK, DV), lambda b, c: (0, 0, 0)),
            pl.BlockSpec((H, 1, DK), lambda b, c: (0, 0, 0)),
            pl.BlockSpec((H, 1, LANES), lambda b, c: (0, 0, 0)),
        ],
        out_specs=[
            pl.BlockSpec((L, vo_w), lambda b, c: (row(b, c), 0)),
            pl.BlockSpec((None, H, DK, DV), lambda b, c: (b, 0, 0, 0)),
            pl.BlockSpec((None, H, 1, DK), lambda b, c: (b, 0, 0, 0)),
            pl.BlockSpec((None, H, 1, LANES), lambda b, c: (b, 0, 0, 0)),
        ],
        out_shape=[
            jax.ShapeDtypeStruct((M, vo_w), BF16),
            jax.ShapeDtypeStruct((B, H, DK, DV), F32),
            jax.ShapeDtypeStruct((B, H, 1, DK), F32),
            jax.ShapeDtypeStruct((B, H, 1, LANES), F32),
        ],
        compiler_params=_params("parallel", "arbitrary"),
        name="mlstm",
    )(p, p, p, p, ps, bias, gh, c0, n0, m0)


def _outproj_kernel(y_ref, w_ref, h_ref, o_ref):
    o_ref[...] = h_ref[...] + _dot(y_ref[...], w_ref[...])


def _outproj(y, w, h, tm):
    M, D = h.shape
    K = y.shape[1]
    return pl.pallas_call(
        _outproj_kernel,
        grid=(M // tm,),
        in_specs=[
            pl.BlockSpec((tm, K), lambda i: (i, 0)),
            pl.BlockSpec((K, D), lambda i: (0, 0)),
            pl.BlockSpec((tm, D), lambda i: (i, 0)),
        ],
        out_specs=pl.BlockSpec((tm, D), lambda i: (i, 0)),
        out_shape=jax.ShapeDtypeStruct((M, D), F32),
        compiler_params=_params("parallel"),
        name="outproj",
    )(y, w, h)


def _rope(x, tab, half):
    c = tab[:, 0:LANES]
    s1 = tab[:, LANES:2 * LANES]
    s2 = tab[:, 2 * LANES:3 * LANES]
    return x * c + pltpu.roll(x, LANES - half, 1) * s1 + pltpu.roll(x, half, 1) * s2


def _inproj_a_kernel(h_ref, g_ref, w_ref, ws_ref, gq_ref, gk_ref, t128_ref, t64_ref, tki_ref,
                     p_ref, ps_ref, xn_ref, *, nq_tiles, tn, half128, half64):
    j = pl.program_id(1)

    @pl.when(j == 0)
    def _():
        xn = _rms(h_ref[...], g_ref[...]).astype(BF16)
        xn_ref[...] = xn
        ps_ref[...] = _rope(_dot(xn, ws_ref[...]), tki_ref[...], half64)

    acc = _dot(xn_ref[...], w_ref[...])
    nh = tn // LANES

    @pl.when(j < nq_tiles)
    def _():
        t = t128_ref[...]
        for c in range(nh):
            xs = _rms(acc[:, c * LANES:(c + 1) * LANES], gq_ref[...])
            p_ref[:, c * LANES:(c + 1) * LANES] = _rope(xs, t, half128).astype(BF16)

    @pl.when(j == nq_tiles)
    def _():
        t = t128_ref[...]
        for c in range(nh // 2):
            xs = _rms(acc[:, c * LANES:(c + 1) * LANES], gk_ref[...])
            p_ref[:, c * LANES:(c + 1) * LANES] = _rope(xs, t, half128).astype(BF16)
        p_ref[:, tn // 2:] = acc[:, tn // 2:].astype(BF16)

    @pl.when(j == nq_tiles + 1)
    def _():
        t = t64_ref[...]
        for c in range(nh):
            p_ref[:, c * LANES:(c + 1) * LANES] = _rope(acc[:, c * LANES:(c + 1) * LANES], t, half64).astype(BF16)


def _inproj_a(h, g, w, ws, gq, gk, t128, t64, tki, tm):
    M, D = h.shape
    N = w.shape[1]
    tn = 1024
    hd = D // A_HEADS
    assert hd == LANES and D // 4 * 2 == tn and N == D + 2 * tn
    nt = t128.shape[0] // tm
    kern = functools.partial(_inproj_a_kernel, nq_tiles=D // tn, tn=tn,
                             half128=hd // ROPE_FRAC // 2, half64=IDX_DIM // ROPE_FRAC // 2)
    tab = pl.BlockSpec((tm, 3 * LANES), lambda i, j: (i % nt, 0))
    return pl.pallas_call(
        kern,
        grid=(M // tm, N // tn),
        in_specs=[
            _row_block(tm, D),
            pl.BlockSpec((1, D), lambda i, j: (0, 0)),
            pl.BlockSpec((D, tn), lambda i, j: (0, j)),
            pl.BlockSpec((D, LANES), lambda i, j: (0, 0)),
            pl.BlockSpec((1, LANES), lambda i, j: (0, 0)),
            pl.BlockSpec((1, LANES), lambda i, j: (0, 0)),
            tab, tab, tab,
        ],
        out_specs=[
            pl.BlockSpec((tm, tn), lambda i, j: (i, j)),
            pl.BlockSpec((tm, LANES), lambda i, j: (i, 0)),
        ],
        out_shape=[jax.ShapeDtypeStruct((M, N), BF16), jax.ShapeDtypeStruct((M, LANES), F32)],
        scratch_shapes=[pltpu.VMEM((tm, D), BF16)],
        compiler_params=_params("parallel", "arbitrary"),
        name="dsa_inproj",
    )(h, g.reshape(1, D), w, ws, gq, gk, t128, t64, tki)


def _rope_table(pos, d, extra_scale_lanes=None, extra_scale=1.0, reps=None):
    rot = d // ROPE_FRAC
    half = rot // 2
    inv = 1.0 / (ROPE_THETA ** (jnp.arange(0, rot, 2, dtype=F32) / rot))
    ang = pos[:, None] * inv[None, :]
    cos, sin = jnp.cos(ang), jnp.sin(ang)
    T = pos.shape[0]
    one = jnp.ones((T, d - rot), F32)
    zero = lambda n: jnp.zeros((T, n), F32)
    c = jnp.concatenate([cos, cos, one], axis=1)
    s1 = jnp.concatenate([-sin, zero(d - half)], axis=1)
    s2 = jnp.concatenate([zero(half), sin, zero(d - rot)], axis=1)
    if reps is None:
        reps = LANES // d
    c, s1, s2 = (jnp.tile(a, (1, reps)) for a in (c, s1, s2))
    fill = LANES - reps * d
    if fill:
        tail = jnp.ones((T, fill), F32)
        if extra_scale_lanes is not None:
            lo, hi = extra_scale_lanes
            lane = jnp.arange(reps * d, LANES)
            tail = jnp.where((lane >= lo) & (lane < hi), extra_scale, 1.0)[None, :] * tail
        c = jnp.concatenate([c, tail], axis=1)
        s1 = jnp.concatenate([s1, zero(fill)], axis=1)
        s2 = jnp.concatenate([s2, zero(fill)], axis=1)
    return jnp.concatenate([c, s1, s2], axis=1)


def _float_key(x):
    bits = lax.bitcast_convert_type(x, jnp.int32)
    return jnp.where(bits < 0, bits ^ jnp.int32(0x7FFFFFFF), bits)


def _dsa_tile(q_ref, qi_ref, psq_ref, o_ref, kcat, vcat, kie, kio, key_ref, bias_ref, qs, os,
              *, tq, row0, nk, topk, n_meta):
    NCH = nk // LANES
    hd = LANES
    G = A_KV_HEADS
    R = A_HEADS // G

    wi = psq_ref[...]
    ke = kie[0:nk, :]
    ko = kio[0:nk, :]
    score = jnp.zeros((tq, nk), F32)
    for p in range(IDX_HEADS // 2):
        qp = qi_ref[:, p * LANES:(p + 1) * LANES]
        c0 = IDX_DIM + 2 * p
        score += jnp.maximum(_dot_nt(qp, ke), 0.0) * wi[:, c0:c0 + 1]
        score += jnp.maximum(_dot_nt(qp, ko), 0.0) * wi[:, c0 + 1:c0 + 2]

    col = lax.broadcasted_iota(jnp.int32, (tq, nk), 1)
    row = lax.broadcasted_iota(jnp.int32, (tq, nk), 0) + row0
    adm = jnp.where(col < LANES, col - n_meta, col - LANES - row - 1) < 0
    score = jnp.where(score == 0.0, 0.0, score)
    key_ref[:, 0:nk] = _float_key(jnp.where(adm, score, -jnp.inf))

    ones = jnp.ones((LANES, LANES), BF16)
    kf = float(topk)

    def count_ge(cand):
        acc = jnp.zeros((tq, LANES), F32)
        for c in range(NCH):
            acc += jnp.where(key_ref[:, c * LANES:(c + 1) * LANES] >= cand, 1.0, 0.0)
        return _dot(acc.astype(BF16), ones)

    zero = jnp.zeros((tq, LANES), jnp.int32)
    base = jnp.where(count_ge(zero) >= kf, zero, INT_MIN)

    def search(it, base):
        cand = base + lax.shift_left(jnp.int32(1), jnp.int32(30) - it.astype(jnp.int32))
        return jnp.where(count_ge(cand) >= kf, cand, base)

    base = lax.fori_loop(0, 31, search, base)
    thr = jnp.maximum(base, NEG_INF_KEY + 1)
    cnt = count_ge(thr)
    for c in range(NCH):
        sl = slice(c * LANES, (c + 1) * LANES)
        bias_ref[:, sl] = jnp.where(key_ref[:, sl] >= thr, 0.0, -jnp.inf)

    @pl.when(jnp.max(cnt) > kf)
    def _():
        need = kf - count_ge(thr + 1)
        rr = lax.broadcasted_iota(jnp.int32, (LANES, LANES), 0)
        cc = lax.broadcasted_iota(jnp.int32, (LANES, LANES), 1)
        tri = jnp.where(rr <= cc, 1.0, 0.0).astype(BF16)
        run = jnp.zeros((tq, LANES), F32)
        for c in range(NCH):
            sl = slice(c * LANES, (c + 1) * LANES)
            kc = key_ref[:, sl]
            eq = kc == thr
            eqb = jnp.where(eq, 1.0, 0.0).astype(BF16)
            rank = run + _dot(eqb, tri)
            keep_eq = jnp.where(rank <= need, 0.0, -jnp.inf)
            bias_ref[:, sl] = jnp.where(kc > thr, 0.0, jnp.where(eq, keep_eq, -jnp.inf))
            run = run + _dot(eqb, ones)

    for g in range(G):
        qs[g] = q_ref[:, g * R * hd:(g + 1) * R * hd]

    def group(g, carry):
        kg = kcat[g, 0:nk, :]
        vg = vcat[g, 0:nk, :]
        for r in range(R):
            logits = _dot_nt(qs[g, :, r * hd:(r + 1) * hd], kg) + bias_ref[:, 0:nk]
            mx = jnp.max(logits, axis=-1, keepdims=True)
            pv = _dot(jnp.exp2(logits - mx).astype(BF16), vg)
            os[g, :, r * hd:(r + 1) * hd] = (pv[:, 0:hd] / pv[:, hd:hd + 1]).astype(BF16)
        return carry

    lax.fori_loop(0, G, group, 0)
    for g in range(G):
        o_ref[:, g * R * hd:(g + 1) * R * hd] = os[g]


def _dsa_kernel(q_ref, qi_ref, psq_ref, k_ref, v_ref, psk_ref, km_ref, vm_ref, psm_ref, o_ref,
                kcat, vcat, kie, kio, key_ref, bias_ref, qs, os, *, tq, S, topk, n_meta):
    i = pl.program_id(1)
    NK = LANES + S
    hd = LANES
    G = A_KV_HEADS

    @pl.when(i == 0)
    def _():
        lane = lax.broadcasted_iota(jnp.int32, (NK, hd), 1)
        ones_col = jnp.where(lane == 0, 1.0, 0.0).astype(BF16)
        for g in range(G):
            sl = slice(g * hd, (g + 1) * hd)
            kcat[g, 0:LANES, :] = km_ref[:, sl]
            kcat[g, LANES:, :] = k_ref[:, sl]
            vcat[g, 0:LANES, 0:hd] = vm_ref[:, sl]
            vcat[g, LANES:, 0:hd] = v_ref[:, sl]
            vcat[g, :, hd:] = ones_col
        for dst0, dst1, src in ((0, LANES, psm_ref), (LANES, NK, psk_ref)):
            a = src[...]
            lane = lax.broadcasted_iota(jnp.int32, a.shape, 1)
            even = jnp.where(lane < IDX_DIM, a, 0.0)
            kie[dst0:dst1, :] = even.astype(BF16)
            kio[dst0:dst1, :] = pltpu.roll(even, IDX_DIM, 1).astype(BF16)

    for t in range(S // tq):
        pl.when(i == t)(functools.partial(
            _dsa_tile, q_ref, qi_ref, psq_ref, o_ref, kcat, vcat, kie, kio, key_ref, bias_ref, qs, os,
            tq=tq, row0=t * tq, nk=LANES + (t + 1) * tq, topk=topk, n_meta=n_meta))


def _dsa(p, ps, pm, psm, B, S, tq, topk):
    M, N = p.shape
    D = N // 2
    kv_w = D // 4
    nq = S // tq
    NK = LANES + S
    kern = functools.partial(_dsa_kernel, tq=tq, S=S, topk=topk, n_meta=N_META)
    return pl.pallas_call(
        kern,
        grid=(B, nq),
        in_specs=[
            pl.BlockSpec((tq, D), lambda b, i: (b * nq + i, 0)),
            pl.BlockSpec((tq, 1024), lambda b, i: (b * nq + i, (D + 2 * kv_w) // 1024)),
            pl.BlockSpec((tq, LANES), lambda b, i: (b * nq + i, 0)),
            pl.BlockSpec((S, kv_w), lambda b, i: (b, D // kv_w)),
            pl.BlockSpec((S, kv_w), lambda b, i: (b, D // kv_w + 1)),
            pl.BlockSpec((S, LANES), lambda b, i: (b, 0)),
            pl.BlockSpec((LANES, kv_w), lambda b, i: (0, D // kv_w)),
            pl.BlockSpec((LANES, kv_w), lambda b, i: (0, D // kv_w + 1)),
            pl.BlockSpec((LANES, LANES), lambda b, i: (0, 0)),
        ],
        out_specs=pl.BlockSpec((tq, D), lambda b, i: (b * nq + i, 0)),
        out_shape=jax.ShapeDtypeStruct((M, D), BF16),
        scratch_shapes=[
            pltpu.VMEM((A_KV_HEADS, NK, LANES), BF16),
            pltpu.VMEM((A_KV_HEADS, NK, 2 * LANES), BF16),
            pltpu.VMEM((NK, LANES), BF16),
            pltpu.VMEM((NK, LANES), BF16),
            pltpu.VMEM((tq, NK), jnp.int32),
            pltpu.VMEM((tq, NK), F32),
            pltpu.VMEM((A_KV_HEADS, tq, D // A_KV_HEADS), BF16),
            pltpu.VMEM((A_KV_HEADS, tq, D // A_KV_HEADS), BF16),
        ],
        compiler_params=_params("parallel", "arbitrary"),
        name="dsa_attention",
    )(p, p, ps, p, p, ps, pm, pm, psm)


def _row_tile(m, pref):
    return pref if m % pref == 0 else m


def kernel(x, meta_tokens, ffn1_norm, ffn1_w_gate, ffn1_w_up, ffn1_w_down, mix_norm, ffn2_norm, ffn2_w_gate, ffn2_w_up, ffn2_w_down, mlstm_w_in, mlstm_b_i, mlstm_b_f, mlstm_head_norm, mlstm_w_out, dsa_w_in, dsa_q_norm, dsa_k_norm, dsa_w_out):
    B, S, D = x.shape
    depth = ffn1_norm.shape[0]
    assert depth == 2 and meta_tokens.shape == (N_META, D)
    F = ffn1_w_gate.shape[-1]
    H, DK, DV = M_HEADS, D // (2 * M_HEADS), D // M_HEADS
    hd = D // A_HEADS
    L = 256
    TM = 1024
    TMO = 512
    TF = 512
    assert S % L == 0 and S % TM == 0 and (B * S) % TMO == 0 and F % TF == 0

    hx = x.reshape(B * S, D)
    hm = meta_tokens.astype(x.dtype)
    bf = lambda w: w.astype(BF16)

    def ffn(h, g, wg, wu, wd):
        return _ffn(h, g, wg, wu, wd, _row_tile(h.shape[0], TM), TF)

    def pad_cols(w):
        return jnp.pad(w, ((0, 0), (0, LANES - w.shape[1])))

    def pad_rows(a, n, front=False):
        r = n - a.shape[0]
        return jnp.pad(a, ((r, 0) if front else (0, r), (0, 0)))

    w = [bf(ffn1_w_gate[0]), bf(ffn1_w_up[0]), bf(ffn1_w_down[0])]
    hx = ffn(hx, ffn1_norm[0], *w)
    hm = ffn(hm, ffn1_norm[0], *w)

    n_wide = 2 * H * DK + 2 * H * DV
    w_wide = bf(mlstm_w_in[0][:, :n_wide])
    w_gate = bf(pad_cols(mlstm_w_in[0][:, n_wide:]))
    px, psx = _inproj_m(hx, mix_norm[0], w_wide, w_gate, TM, 1024)
    pm, psm = _inproj_m(hm, mix_norm[0], w_wide, w_gate, N_META, 1024)

    bias = pad_cols(jnp.concatenate([mlstm_b_i[0], mlstm_b_f[0]]).astype(F32)[None, :])
    gh = mlstm_head_norm[0].astype(F32).reshape(1, H * DV)
    zc = jnp.zeros((H, DK, DV), F32)
    zn = jnp.zeros((H, 1, DK), F32)
    zm = jnp.zeros((H, 1, LANES), F32)
    ym, c0, n0, m0 = _mlstm(pad_rows(pm, L, True), pad_rows(psm, L, True), bias, gh, zc, zn, zm,
                            1, L, L - N_META)
    yx, _, _, _ = _mlstm(px, psx, bias, gh, c0[0], n0[0], m0[0], B, L, 0)
    w_out = bf(mlstm_w_out[0])
    hx = _outproj(yx, w_out, hx, TMO)
    hm = _outproj(ym[L - N_META:], w_out, hm, N_META)

    w = [bf(ffn2_w_gate[0]), bf(ffn2_w_up[0]), bf(ffn2_w_down[0])]
    hx = ffn(hx, ffn2_norm[0], *w)
    hm = ffn(hm, ffn2_norm[0], *w)

    w = [bf(ffn1_w_gate[1]), bf(ffn1_w_up[1]), bf(ffn1_w_down[1])]
    hx = ffn(hx, ffn1_norm[1], *w)
    hm = ffn(hm, ffn1_norm[1], *w)

    n_wide = A_HEADS * hd + 2 * A_KV_HEADS * hd + IDX_HEADS * IDX_DIM
    w_wide = bf(dsa_w_in[0][:, :n_wide])
    w_idx = bf(pad_cols(dsa_w_in[0][:, n_wide:]))
    gq = (dsa_q_norm[0].astype(F32) * (hd ** -0.5 * float(np.log2(np.e))))[None, :]
    gk = dsa_k_norm[0].astype(F32)[None, :]
    wi_scale = IDX_HEADS ** -0.5 * IDX_DIM ** -0.5
    pos = jnp.arange(N_META + S, dtype=F32)
    t128 = _rope_table(pos, hd)
    t64 = _rope_table(pos, IDX_DIM)
    tki = _rope_table(pos, IDX_DIM, (IDX_DIM, IDX_DIM + IDX_HEADS), wi_scale, reps=1)
    tabs_x = [t[N_META:] for t in (t128, t64, tki)]
    tabs_m = [t[:N_META] for t in (t128, t64, tki)]
    px, psx = _inproj_a(hx, mix_norm[1], w_wide, w_idx, gq, gk, *tabs_x, TM)
    pm, psm = _inproj_a(hm, mix_norm[1], w_wide, w_idx, gq, gk, *tabs_m, N_META)

    topk = min(TOPK_MAX, (N_META + S - N_META) // 4)
    ox = _dsa(px, psx, pad_rows(pm, LANES), pad_rows(psm, LANES), B, S, 256, topk)
    hx = _outproj(ox, bf(dsa_w_out[0]), hx, TMO)

    w = [bf(ffn2_w_gate[1]), bf(ffn2_w_up[1]), bf(ffn2_w_down[1])]
    hx = ffn(hx, ffn2_norm[1], *w)
    return hx.reshape(B, S, D)
```

```python
import functools

import numpy as np
import jax
import jax.numpy as jnp
from jax import lax
from jax.experimental import pallas as pl
from jax.experimental.pallas import tpu as pltpu

F32 = jnp.float32
BF16 = jnp.bfloat16

N_META = 16
FFN_HALF = 0.5
NORM_EPS = 1e-6
ROPE_THETA = 500000.0
ROPE_FRAC = 4
M_HEADS = 4
A_HEADS = 16
A_KV_HEADS = 4
IDX_HEADS = 16
IDX_DIM = 64
TOPK_MAX = 256
LOG_I_PAD = -1e30

LANES = 128
VMEM_LIMIT = 60 * 1024 * 1024
DSA_VARIANTS = 3
SEARCH_ROW_BLOCKS = 2
SEARCH_UNROLL = 4
INT_MIN = int(np.iinfo(np.int32).min)
NEG_INF_KEY = int(np.array(-np.inf, np.float32).view(np.int32) ^ np.int32(0x7FFFFFFF))


def _params(*sem):
    return pltpu.CompilerParams(dimension_semantics=sem, vmem_limit_bytes=VMEM_LIMIT)


def _row_block(tm, d):
    return pl.BlockSpec((tm, d), lambda i, j: (i, 0))


def _rms(x, g):
    return x * lax.rsqrt(jnp.mean(x * x, axis=-1, keepdims=True) + NORM_EPS) * g


def _dot(a, b):
    return jnp.dot(a, b, preferred_element_type=F32)


def _dot_nt(a, b):
    return lax.dot_general(a, b, (((1,), (1,)), ((), ())), preferred_element_type=F32)


def _ffn_kernel(h_ref, g_ref, wg_ref, wu_ref, wd_ref, o_ref, xn_ref):
    @pl.when(pl.program_id(1) == 0)
    def _():
        h = h_ref[...]
        xn_ref[...] = _rms(h, g_ref[...]).astype(BF16)
        o_ref[...] = h

    xn = xn_ref[...]
    gate = _dot(xn, wg_ref[...])
    up = _dot(xn, wu_ref[...])
    act = (gate * jax.nn.sigmoid(gate)) * (up * FFN_HALF)
    o_ref[...] += _dot(act.astype(BF16), wd_ref[...])


def _ffn(h, g, wg, wu, wd, tm, tf):
    M, D = h.shape
    F = wg.shape[1]
    return pl.pallas_call(
        _ffn_kernel,
        grid=(M // tm, F // tf),
        in_specs=[
            _row_block(tm, D),
            pl.BlockSpec((1, D), lambda i, j: (0, 0)),
            pl.BlockSpec((D, tf), lambda i, j: (0, j)),
            pl.BlockSpec((D, tf), lambda i, j: (0, j)),
            pl.BlockSpec((tf, D), lambda i, j: (j, 0)),
        ],
        out_specs=pl.BlockSpec((tm, D), lambda i, j: (i, 0)),
        out_shape=jax.ShapeDtypeStruct((M, D), F32),
        scratch_shapes=[pltpu.VMEM((tm, D), BF16)],
        compiler_params=_params("parallel", "arbitrary"),
        name="ffn",
    )(h, g.reshape(1, D), wg, wu, wd)


def _inproj_m_kernel(h_ref, g_ref, w_ref, ws_ref, p_ref, ps_ref, xn_ref):
    @pl.when(pl.program_id(1) == 0)
    def _():
        xn = _rms(h_ref[...], g_ref[...]).astype(BF16)
        xn_ref[...] = xn
        ps_ref[...] = _dot(xn, ws_ref[...])

    p_ref[...] = _dot(xn_ref[...], w_ref[...]).astype(BF16)


def _inproj_m(h, g, w, ws, tm, tn):
    M, D = h.shape
    N = w.shape[1]
    return pl.pallas_call(
        _inproj_m_kernel,
        grid=(M // tm, N // tn),
        in_specs=[
            _row_block(tm, D),
            pl.BlockSpec((1, D), lambda i, j: (0, 0)),
            pl.BlockSpec((D, tn), lambda i, j: (0, j)),
            pl.BlockSpec((D, LANES), lambda i, j: (0, 0)),
        ],
        out_specs=[
            pl.BlockSpec((tm, tn), lambda i, j: (i, j)),
            pl.BlockSpec((tm, LANES), lambda i, j: (i, 0)),
        ],
        out_shape=[jax.ShapeDtypeStruct((M, N), BF16), jax.ShapeDtypeStruct((M, LANES), F32)],
        scratch_shapes=[pltpu.VMEM((tm, D), BF16)],
        compiler_params=_params("parallel", "arbitrary"),
        name="mlstm_inproj",
    )(h, g.reshape(1, D), w, ws)


def _log_sigmoid(x):
    return jnp.minimum(x, 0.0) - jnp.log1p(jnp.exp(-jnp.abs(x)))


def _mlstm_kernel(q_ref, k_ref, v_ref, o_ref, ps_ref, bias_ref, gh_ref, c0_ref, n0_ref, m0_ref,
                  y_ref, c_ref, n_ref, m_ref, *, L, npad, H, DK, DV):
    @pl.when(pl.program_id(1) == 0)
    def _():
        c_ref[...] = c0_ref[...]
        n_ref[...] = n0_ref[...]
        m_ref[...] = m0_ref[...]

    gates = ps_ref[...] + bias_ref[...]
    logf = _log_sigmoid(gates)
    if npad:
        valid = lax.broadcasted_iota(jnp.int32, (L, 1), 0) >= npad
        logf = jnp.where(valid, logf, 0.0)
        gates = jnp.where(valid, gates, LOG_I_PAD)
    ii = lax.broadcasted_iota(jnp.int32, (L, L), 0)
    jj = lax.broadcasted_iota(jnp.int32, (L, L), 1)
    causal = jj <= ii
    cum = jnp.dot(causal.astype(F32), logf, preferred_element_type=F32,
                  precision=lax.Precision.HIGHEST)
    cum_t = cum.T
    gates_t = gates.T

    for h in range(H):
        b_col = cum[:, H + h:H + h + 1]
        b_row = cum_t[H + h:H + h + 1, :]
        li_col = gates[:, h:h + 1]
        li_row = gates_t[h:h + 1, :]
        m_prev = m_ref[h][:, 0:1]
        q = q_ref[:, h * DK:(h + 1) * DK] * (DK ** -0.5)
        k = k_ref[:, h * DK:(h + 1) * DK]
        v = v_ref[:, h * DV:(h + 1) * DV]
        c_old = c_ref[h]
        n_old = n_ref[h]

        dm = jnp.where(causal, b_col - b_row + li_row, -jnp.inf)
        inter = b_col + m_prev
        m_t = jnp.maximum(inter, jnp.max(dm, axis=-1, keepdims=True))
        s = _dot_nt(q, k) * jnp.exp(dm - m_t)
        w_inter = jnp.exp(inter - m_t)
        num = w_inter * _dot(q, c_old.astype(BF16)) + _dot(s.astype(BF16), v)
        qn = jnp.sum(q.astype(F32) * n_old, axis=-1, keepdims=True)
        den = w_inter * qn + jnp.sum(s, axis=-1, keepdims=True)
        hout = num / jnp.maximum(jnp.abs(den), jnp.exp(-m_t))

        b_last = b_col[L - 1:L, :]
        g_row = b_last - b_row + li_row
        g_col = b_last - b_col + li_col
        m_new = jnp.maximum(b_last + m_prev, jnp.max(g_row, axis=-1, keepdims=True))
        decay = jnp.exp(b_last + m_prev - m_new)
        kw = k.astype(F32) * jnp.exp(g_col - m_new)
        c_ref[h] = decay * c_old + _dot(kw.T.astype(BF16), v)
        n_ref[h] = decay * n_old + jnp.sum(kw, axis=0, keepdims=True)
        m_ref[h] = jnp.broadcast_to(m_new, (1, LANES))

        hn = _rms(hout, gh_ref[:, h * DV:(h + 1) * DV])
        og = jax.nn.sigmoid(o_ref[:, h * DV:(h + 1) * DV].astype(F32))
        y_ref[:, h * DV:(h + 1) * DV] = (og * hn).astype(BF16)


def _mlstm(p, ps, bias, gh, c0, n0, m0, B, L, npad):
    M = p.shape[0]
    H = M_HEADS
    DK, DV = c0.shape[1], c0.shape[2]
    NC = M // (B * L)
    row = lambda b, c: b * NC + c
    qk_w, vo_w = H * DK, H * DV
    assert vo_w == 2 * qk_w
    kern = functools.partial(_mlstm_kernel, L=L, npad=npad, H=H, DK=DK, DV=DV)
    return pl.pallas_call(
        kern,
        grid=(B, NC),
        in_specs=[
            pl.BlockSpec((L, qk_w), lambda b, c: (row(b, c), 0)),
            pl.BlockSpec((L, qk_w), lambda b, c: (row(b, c), 1)),
            pl.BlockSpec((L, vo_w), lambda b, c: (row(b, c), 1)),
            pl.BlockSpec((L, vo_w), lambda b, c: (row(b, c), 2)),
            pl.BlockSpec((L, LANES), lambda b, c: (row(b, c), 0)),
            pl.BlockSpec((1, LANES), lambda b, c: (0, 0)),
            pl.BlockSpec((1, vo_w), lambda b, c: (0, 0)),
            pl.BlockSpec((H, DK, DV), lambda b, c: (0, 0, 0)),
            pl.BlockSpec((H, 1, DK), lambda b, c: (0, 0, 0)),
            pl.BlockSpec((H, 1, LANES), lambda b, c: (0, 0, 0)),
        ],
        out_specs=[
            pl.BlockSpec((L, vo_w), lambda b, c: (row(b, c), 0)),
            pl.BlockSpec((None, H, DK, DV), lambda b, c: (b, 0, 0, 0)),
            pl.BlockSpec((None, H, 1, DK), lambda b, c: (b, 0, 0, 0)),
            pl.BlockSpec((None, H, 1, LANES), lambda b, c: (b, 0, 0, 0)),
        ],
        out_shape=[
            jax.ShapeDtypeStruct((M, vo_w), BF16),
            jax.ShapeDtypeStruct((B, H, DK, DV), F32),
            jax.ShapeDtypeStruct((B, H, 1, DK), F32),
            jax.ShapeDtypeStruct((B, H, 1, LANES), F32),
        ],
        compiler_params=_params("parallel", "arbitrary"),
        name="mlstm",
    )(p, p, p, p, ps, bias, gh, c0, n0, m0)


def _outproj_kernel(y_ref, w_ref, h_ref, o_ref):
    o_ref[...] = h_ref[...] + _dot(y_ref[...], w_ref[...])


def _outproj(y, w, h, tm):
    M, D = h.shape
    K = y.shape[1]
    return pl.pallas_call(
        _outproj_kernel,
        grid=(M // tm,),
        in_specs=[
            pl.BlockSpec((tm, K), lambda i: (i, 0)),
            pl.BlockSpec((K, D), lambda i: (0, 0)),
            pl.BlockSpec((tm, D), lambda i: (i, 0)),
        ],
        out_specs=pl.BlockSpec((tm, D), lambda i: (i, 0)),
        out_shape=jax.ShapeDtypeStruct((M, D), F32),
        compiler_params=_params("parallel"),
        name="outproj",
    )(y, w, h)


def _rope(x, tab, half):
    c = tab[:, 0:LANES]
    s1 = tab[:, LANES:2 * LANES]
    s2 = tab[:, 2 * LANES:3 * LANES]
    return x * c + pltpu.roll(x, LANES - half, 1) * s1 + pltpu.roll(x, half, 1) * s2


def _inproj_a_kernel(h_ref, g_ref, w_ref, ws_ref, gq_ref, gk_ref, t128_ref, t64_ref, tki_ref,
                     p_ref, ps_ref, xn_ref, *, nq_tiles, tn, half128, half64):
    j = pl.program_id(1)

    @pl.when(j == 0)
    def _():
        xn = _rms(h_ref[...], g_ref[...]).astype(BF16)
        xn_ref[...] = xn
        ps_ref[...] = _rope(_dot(xn, ws_ref[...]), tki_ref[...], half64)

    acc = _dot(xn_ref[...], w_ref[...])
    nh = tn // LANES

    @pl.when(j < nq_tiles)
    def _():
        t = t128_ref[...]
        for c in range(nh):
            xs = _rms(acc[:, c * LANES:(c + 1) * LANES], gq_ref[...])
            p_ref[:, c * LANES:(c + 1) * LANES] = _rope(xs, t, half128).astype(BF16)

    @pl.when(j == nq_tiles)
    def _():
        t = t128_ref[...]
        for c in range(nh // 2):
            xs = _rms(acc[:, c * LANES:(c + 1) * LANES], gk_ref[...])
            p_ref[:, c * LANES:(c + 1) * LANES] = _rope(xs, t, half128).astype(BF16)
        p_ref[:, tn // 2:] = acc[:, tn // 2:].astype(BF16)

    @pl.when(j == nq_tiles + 1)
    def _():
        t = t64_ref[...]
        for c in range(nh):
            p_ref[:, c * LANES:(c + 1) * LANES] = _rope(acc[:, c * LANES:(c + 1) * LANES], t, half64).astype(BF16)


def _inproj_a(h, g, w, ws, gq, gk, t128, t64, tki, tm):
    M, D = h.shape
    N = w.shape[1]
    tn = 1024
    hd = D // A_HEADS
    assert hd == LANES and D // 4 * 2 == tn and N == D + 2 * tn
    nt = t128.shape[0] // tm
    kern = functools.partial(_inproj_a_kernel, nq_tiles=D // tn, tn=tn,
                             half128=hd // ROPE_FRAC // 2, half64=IDX_DIM // ROPE_FRAC // 2)
    tab = pl.BlockSpec((tm, 3 * LANES), lambda i, j: (i % nt, 0))
    return pl.pallas_call(
        kern,
        grid=(M // tm, N // tn),
        in_specs=[
            _row_block(tm, D),
            pl.BlockSpec((1, D), lambda i, j: (0, 0)),
            pl.BlockSpec((D, tn), lambda i, j: (0, j)),
            pl.BlockSpec((D, LANES), lambda i, j: (0, 0)),
            pl.BlockSpec((1, LANES), lambda i, j: (0, 0)),
            pl.BlockSpec((1, LANES), lambda i, j: (0, 0)),
            tab, tab, tab,
        ],
        out_specs=[
            pl.BlockSpec((tm, tn), lambda i, j: (i, j)),
            pl.BlockSpec((tm, LANES), lambda i, j: (i, 0)),
        ],
        out_shape=[jax.ShapeDtypeStruct((M, N), BF16), jax.ShapeDtypeStruct((M, LANES), F32)],
        scratch_shapes=[pltpu.VMEM((tm, D), BF16)],
        compiler_params=_params("parallel", "arbitrary"),
        name="dsa_inproj",
    )(h, g.reshape(1, D), w, ws, gq, gk, t128, t64, tki)


def _rope_table(pos, d, extra_scale_lanes=None, extra_scale=1.0, reps=None):
    rot = d // ROPE_FRAC
    half = rot // 2
    inv = 1.0 / (ROPE_THETA ** (jnp.arange(0, rot, 2, dtype=F32) / rot))
    ang = pos[:, None] * inv[None, :]
    cos, sin = jnp.cos(ang), jnp.sin(ang)
    T = pos.shape[0]
    one = jnp.ones((T, d - rot), F32)
    zero = lambda n: jnp.zeros((T, n), F32)
    c = jnp.concatenate([cos, cos, one], axis=1)
    s1 = jnp.concatenate([-sin, zero(d - half)], axis=1)
    s2 = jnp.concatenate([zero(half), sin, zero(d - rot)], axis=1)
    if reps is None:
        reps = LANES // d
    c, s1, s2 = (jnp.tile(a, (1, reps)) for a in (c, s1, s2))
    fill = LANES - reps * d
    if fill:
        tail = jnp.ones((T, fill), F32)
        if extra_scale_lanes is not None:
            lo, hi = extra_scale_lanes
            lane = jnp.arange(reps * d, LANES)
            tail = jnp.where((lane >= lo) & (lane < hi), extra_scale, 1.0)[None, :] * tail
        c = jnp.concatenate([c, tail], axis=1)
        s1 = jnp.concatenate([s1, zero(fill)], axis=1)
        s2 = jnp.concatenate([s2, zero(fill)], axis=1)
    return jnp.concatenate([c, s1, s2], axis=1)


def _float_key(x):
    bits = lax.bitcast_convert_type(x, jnp.int32)
    return jnp.where(bits < 0, bits ^ jnp.int32(0x7FFFFFFF), bits)


def _dsa_tile(q_ref, qi_ref, psq_ref, o_ref, kcat, vcat, kie, kio, key_ref, bias_ref, qs, os,
              *, tq, row0, nk, topk, n_meta):
    NCH = nk // LANES
    hd = LANES
    G = A_KV_HEADS
    R = A_HEADS // G

    wi = psq_ref[...]
    ke = kie[0:nk, :]
    ko = kio[0:nk, :]
    score = jnp.zeros((tq, nk), F32)
    for p in range(IDX_HEADS // 2):
        qp = qi_ref[:, p * LANES:(p + 1) * LANES]
        c0 = IDX_DIM + 2 * p
        score += jnp.maximum(_dot_nt(qp, ke), 0.0) * wi[:, c0:c0 + 1]
        score += jnp.maximum(_dot_nt(qp, ko), 0.0) * wi[:, c0 + 1:c0 + 2]

    col = lax.broadcasted_iota(jnp.int32, (tq, nk), 1)
    row = lax.broadcasted_iota(jnp.int32, (tq, nk), 0) + row0
    adm = jnp.where(col < LANES, col - n_meta, col - LANES - row - 1) < 0
    score = jnp.where(score == 0.0, 0.0, score)
    key_ref[:, 0:nk] = _float_key(jnp.where(adm, score, -jnp.inf))

    ones = jnp.ones((LANES, LANES), BF16)
    kf = float(topk)

    def count_ge(cand, r0=0, rows=tq):
        acc = jnp.zeros((rows, LANES), F32)
        for c in range(NCH):
            acc += jnp.where(key_ref[r0:r0 + rows, c * LANES:(c + 1) * LANES] >= cand, 1.0, 0.0)
        return _dot(acc.astype(BF16), ones)

    nblk = SEARCH_ROW_BLOCKS
    rb = tq // nblk

    def search(trip, bases):
        bases = list(bases)
        for u in range(SEARCH_UNROLL):
            bit = jnp.int32(31) - (trip.astype(jnp.int32) * SEARCH_UNROLL + u)
            step = lax.shift_left(jnp.int32(1), bit)
            for blk in range(nblk):
                cand = bases[blk] + step
                bases[blk] = jnp.where(count_ge(cand, blk * rb, rb) >= kf, cand, bases[blk])
        return tuple(bases)

    init = tuple(jnp.full((rb, LANES), INT_MIN, jnp.int32) for _ in range(nblk))
    base = jnp.concatenate(lax.fori_loop(0, 32 // SEARCH_UNROLL, search, init), axis=0)
    thr = jnp.maximum(base, NEG_INF_KEY + 1)
    cnt = count_ge(thr)
    for c in range(NCH):
        sl = slice(c * LANES, (c + 1) * LANES)
        bias_ref[:, sl] = jnp.where(key_ref[:, sl] >= thr, 0.0, -jnp.inf)

    @pl.when(jnp.max(cnt) > kf)
    def _():
        need = kf - count_ge(thr + 1)
        rr = lax.broadcasted_iota(jnp.int32, (LANES, LANES), 0)
        cc = lax.broadcasted_iota(jnp.int32, (LANES, LANES), 1)
        tri = jnp.where(rr <= cc, 1.0, 0.0).astype(BF16)
        run = jnp.zeros((tq, LANES), F32)
        for c in range(NCH):
            sl = slice(c * LANES, (c + 1) * LANES)
            kc = key_ref[:, sl]
            eq = kc == thr
            eqb = jnp.where(eq, 1.0, 0.0).astype(BF16)
            rank = run + _dot(eqb, tri)
            keep_eq = jnp.where(rank <= need, 0.0, -jnp.inf)
            bias_ref[:, sl] = jnp.where(kc > thr, 0.0, jnp.where(eq, keep_eq, -jnp.inf))
            run = run + _dot(eqb, ones)

    for g in range(G):
        qs[g] = q_ref[:, g * R * hd:(g + 1) * R * hd]

    def group(g, carry):
        kg = kcat[g, 0:nk, :]
        vg = vcat[g, 0:nk, :]
        for r in range(R):
            logits = _dot_nt(qs[g, :, r * hd:(r + 1) * hd], kg) + bias_ref[:, 0:nk]
            mx = jnp.max(logits, axis=-1, keepdims=True)
            pv = _dot(jnp.exp2(logits - mx).astype(BF16), vg)
            os[g, :, r * hd:(r + 1) * hd] = (pv[:, 0:hd] / pv[:, hd:hd + 1]).astype(BF16)
        return carry

    lax.fori_loop(0, G, group, 0)
    for g in range(G):
        o_ref[:, g * R * hd:(g + 1) * R * hd] = os[g]


def _dsa_kernel(q_ref, qi_ref, psq_ref, k_ref, v_ref, psk_ref, km_ref, vm_ref, psm_ref, o_ref,
                kcat, vcat, kie, kio, key_ref, bias_ref, qs, os, *, tq, S, topk, n_meta):
    i = pl.program_id(1)
    NK = LANES + S
    hd = LANES
    G = A_KV_HEADS

    @pl.when(i == 0)
    def _():
        lane = lax.broadcasted_iota(jnp.int32, (NK, hd), 1)
        ones_col = jnp.where(lane == 0, 1.0, 0.0).astype(BF16)
        for g in range(G):
            sl = slice(g * hd, (g + 1) * hd)
            kcat[g, 0:LANES, :] = km_ref[:, sl]
            kcat[g, LANES:, :] = k_ref[:, sl]
            vcat[g, 0:LANES, 0:hd] = vm_ref[:, sl]
            vcat[g, LANES:, 0:hd] = v_ref[:, sl]
            vcat[g, :, hd:] = ones_col
        for dst0, dst1, src in ((0, LANES, psm_ref), (LANES, NK, psk_ref)):
            a = src[...]
            lane = lax.broadcasted_iota(jnp.int32, a.shape, 1)
            even = jnp.where(lane < IDX_DIM, a, 0.0)
            kie[dst0:dst1, :] = even.astype(BF16)
            kio[dst0:dst1, :] = pltpu.roll(even, IDX_DIM, 1).astype(BF16)

    nq = S // tq
    bounds = sorted({-(-nq * (v + 1) // DSA_VARIANTS) for v in range(DSA_VARIANTS)})
    lo = 0
    for hi in bounds:
        pl.when((i >= lo) & (i < hi))(functools.partial(
            _dsa_tile, q_ref, qi_ref, psq_ref, o_ref, kcat, vcat, kie, kio, key_ref, bias_ref, qs, os,
            tq=tq, row0=i * tq, nk=LANES + hi * tq, topk=topk, n_meta=n_meta))
        lo = hi


def _dsa(p, ps, pm, psm, B, S, tq, topk):
    M, N = p.shape
    D = N // 2
    kv_w = D // 4
    nq = S // tq
    NK = LANES + S
    kern = functools.partial(_dsa_kernel, tq=tq, S=S, topk=topk, n_meta=N_META)
    return pl.pallas_call(
        kern,
        grid=(B, nq),
        in_specs=[
            pl.BlockSpec((tq, D), lambda b, i: (b * nq + i, 0)),
            pl.BlockSpec((tq, 1024), lambda b, i: (b * nq + i, (D + 2 * kv_w) // 1024)),
            pl.BlockSpec((tq, LANES), lambda b, i: (b * nq + i, 0)),
            pl.BlockSpec((S, kv_w), lambda b, i: (b, D // kv_w)),
            pl.BlockSpec((S, kv_w), lambda b, i: (b, D // kv_w + 1)),
            pl.BlockSpec((S, LANES), lambda b, i: (b, 0)),
            pl.BlockSpec((LANES, kv_w), lambda b, i: (0, D // kv_w)),
            pl.BlockSpec((LANES, kv_w), lambda b, i: (0, D // kv_w + 1)),
            pl.BlockSpec((LANES, LANES), lambda b, i: (0, 0)),
        ],
        out_specs=pl.BlockSpec((tq, D), lambda b, i: (b * nq + i, 0)),
        out_shape=jax.ShapeDtypeStruct((M, D), BF16),
        scratch_shapes=[
            pltpu.VMEM((A_KV_HEADS, NK, LANES), BF16),
            pltpu.VMEM((A_KV_HEADS, NK, 2 * LANES), BF16),
            pltpu.VMEM((NK, LANES), BF16),
            pltpu.VMEM((NK, LANES), BF16),
            pltpu.VMEM((tq, NK), jnp.int32),
            pltpu.VMEM((tq, NK), F32),
            pltpu.VMEM((A_KV_HEADS, tq, D // A_KV_HEADS), BF16),
            pltpu.VMEM((A_KV_HEADS, tq, D // A_KV_HEADS), BF16),
        ],
        compiler_params=_params("parallel", "arbitrary"),
        name="dsa_attention",
    )(p, p, ps, p, p, ps, pm, pm, psm)


def _row_tile(m, pref):
    return pref if m % pref == 0 else m


def kernel(x, meta_tokens, ffn1_norm, ffn1_w_gate, ffn1_w_up, ffn1_w_down, mix_norm, ffn2_norm, ffn2_w_gate, ffn2_w_up, ffn2_w_down, mlstm_w_in, mlstm_b_i, mlstm_b_f, mlstm_head_norm, mlstm_w_out, dsa_w_in, dsa_q_norm, dsa_k_norm, dsa_w_out):
    B, S, D = x.shape
    depth = ffn1_norm.shape[0]
    assert depth == 2 and meta_tokens.shape == (N_META, D)
    F = ffn1_w_gate.shape[-1]
    H, DK, DV = M_HEADS, D // (2 * M_HEADS), D // M_HEADS
    hd = D // A_HEADS
    L = 256
    TM = 1024
    TMO = 512
    TF = 512
    assert S % L == 0 and S % TM == 0 and (B * S) % TMO == 0 and F % TF == 0

    hx = x.reshape(B * S, D)
    hm = meta_tokens.astype(x.dtype)
    bf = lambda w: w.astype(BF16)

    def ffn(h, g, wg, wu, wd):
        return _ffn(h, g, wg, wu, wd, _row_tile(h.shape[0], TM), TF)

    def pad_cols(w):
        return jnp.pad(w, ((0, 0), (0, LANES - w.shape[1])))

    def pad_rows(a, n, front=False):
        r = n - a.shape[0]
        return jnp.pad(a, ((r, 0) if front else (0, r), (0, 0)))

    w = [bf(ffn1_w_gate[0]), bf(ffn1_w_up[0]), bf(ffn1_w_down[0])]
    hx = ffn(hx, ffn1_norm[0], *w)
    hm = ffn(hm, ffn1_norm[0], *w)

    n_wide = 2 * H * DK + 2 * H * DV
    w_wide = bf(mlstm_w_in[0][:, :n_wide])
    w_gate = bf(pad_cols(mlstm_w_in[0][:, n_wide:]))
    px, psx = _inproj_m(hx, mix_norm[0], w_wide, w_gate, TM, 1024)
    pm, psm = _inproj_m(hm, mix_norm[0], w_wide, w_gate, N_META, 1024)

    bias = pad_cols(jnp.concatenate([mlstm_b_i[0], mlstm_b_f[0]]).astype(F32)[None, :])
    gh = mlstm_head_norm[0].astype(F32).reshape(1, H * DV)
    zc = jnp.zeros((H, DK, DV), F32)
    zn = jnp.zeros((H, 1, DK), F32)
    zm = jnp.zeros((H, 1, LANES), F32)
    ym, c0, n0, m0 = _mlstm(pad_rows(pm, L, True), pad_rows(psm, L, True), bias, gh, zc, zn, zm,
                            1, L, L - N_META)
    yx, _, _, _ = _mlstm(px, psx, bias, gh, c0[0], n0[0], m0[0], B, L, 0)
    w_out = bf(mlstm_w_out[0])
    hx = _outproj(yx, w_out, hx, TMO)
    hm = _outproj(ym[L - N_META:], w_out, hm, N_META)

    w = [bf(ffn2_w_gate[0]), bf(ffn2_w_up[0]), bf(ffn2_w_down[0])]
    hx = ffn(hx, ffn2_norm[0], *w)
    hm = ffn(hm, ffn2_norm[0], *w)

    w = [bf(ffn1_w_gate[1]), bf(ffn1_w_up[1]), bf(ffn1_w_down[1])]
    hx = ffn(hx, ffn1_norm[1], *w)
    hm = ffn(hm, ffn1_norm[1], *w)

    n_wide = A_HEADS * hd + 2 * A_KV_HEADS * hd + IDX_HEADS * IDX_DIM
    w_wide = bf(dsa_w_in[0][:, :n_wide])
    w_idx = bf(pad_cols(dsa_w_in[0][:, n_wide:]))
    gq = (dsa_q_norm[0].astype(F32) * (hd ** -0.5 * float(np.log2(np.e))))[None, :]
    gk = dsa_k_norm[0].astype(F32)[None, :]
    wi_scale = IDX_HEADS ** -0.5 * IDX_DIM ** -0.5
    pos = jnp.arange(N_META + S, dtype=F32)
    t128 = _rope_table(pos, hd)
    t64 = _rope_table(pos, IDX_DIM)
    tki = _rope_table(pos, IDX_DIM, (IDX_DIM, IDX_DIM + IDX_HEADS), wi_scale, reps=1)
    tabs_x = [t[N_META:] for t in (t128, t64, tki)]
    tabs_m = [t[:N_META] for t in (t128, t64, tki)]
    px, psx = _inproj_a(hx, mix_norm[1], w_wide, w_idx, gq, gk, *tabs_x, TM)
    pm, psm = _inproj_a(hm, mix_norm[1], w_wide, w_idx, gq, gk, *tabs_m, N_META)

    topk = min(TOPK_MAX, (N_META + S - N_META) // 4)
    ox = _dsa(px, psx, pad_rows(pm, LANES), pad_rows(psm, LANES), B, S, 256, topk)
    hx = _outproj(ox, bf(dsa_w_out[0]), hx, TMO)

    w = [bf(ffn2_w_gate[1]), bf(ffn2_w_up[1]), bf(ffn2_w_down[1])]
    hx = ffn(hx, ffn2_norm[1], *w)
    return hx.reshape(B, S, D)
```

```python
import functools

import numpy as np
import jax
import jax.numpy as jnp
from jax import lax
from jax.experimental import pallas as pl
from jax.experimental.pallas import tpu as pltpu

F32 = jnp.float32
BF16 = jnp.bfloat16

N_META = 16
FFN_HALF = 0.5
NORM_EPS = 1e-6
ROPE_THETA = 500000.0
ROPE_FRAC = 4
M_HEADS = 4
A_HEADS = 16
A_KV_HEADS = 4
IDX_HEADS = 16
IDX_DIM = 64
TOPK_MAX = 256
LOG_I_PAD = -1e30

LANES = 128
VMEM_LIMIT = 60 * 1024 * 1024
DSA_VARIANTS = 3
SEARCH_ROW_BLOCKS = 2
SEARCH_UNROLL = 4
INT_MIN = int(np.iinfo(np.int32).min)
NEG_INF_KEY = int(np.array(-np.inf, np.float32).view(np.int32) ^ np.int32(0x7FFFFFFF))


def _params(*sem):
    return pltpu.CompilerParams(dimension_semantics=sem, vmem_limit_bytes=VMEM_LIMIT)


def _row_block(tm, d):
    return pl.BlockSpec((tm, d), lambda i, j: (i, 0))


def _rms(x, g):
    return x * lax.rsqrt(jnp.mean(x * x, axis=-1, keepdims=True) + NORM_EPS) * g


def _dot(a, b):
    return jnp.dot(a, b, preferred_element_type=F32)


def _dot_nt(a, b):
    return lax.dot_general(a, b, (((1,), (1,)), ((), ())), preferred_element_type=F32)


def _ffn_kernel(h_ref, g_ref, wg_ref, wu_ref, wd_ref, o_ref, xn_ref):
    @pl.when(pl.program_id(1) == 0)
    def _():
        h = h_ref[...]
        xn_ref[...] = _rms(h, g_ref[...]).astype(BF16)
        o_ref[...] = h

    xn = xn_ref[...]
    gate = _dot(xn, wg_ref[...])
    up = _dot(xn, wu_ref[...])
    act = (gate * jax.nn.sigmoid(gate)) * (up * FFN_HALF)
    o_ref[...] += _dot(act.astype(BF16), wd_ref[...])


def _ffn(h, g, wg, wu, wd, tm, tf):
    M, D = h.shape
    F = wg.shape[1]
    return pl.pallas_call(
        _ffn_kernel,
        grid=(M // tm, F // tf),
        in_specs=[
            _row_block(tm, D),
            pl.BlockSpec((1, D), lambda i, j: (0, 0)),
            pl.BlockSpec((D, tf), lambda i, j: (0, j)),
            pl.BlockSpec((D, tf), lambda i, j: (0, j)),
            pl.BlockSpec((tf, D), lambda i, j: (j, 0)),
        ],
        out_specs=pl.BlockSpec((tm, D), lambda i, j: (i, 0)),
        out_shape=jax.ShapeDtypeStruct((M, D), F32),
        scratch_shapes=[pltpu.VMEM((tm, D), BF16)],
        compiler_params=_params("parallel", "arbitrary"),
        name="ffn",
    )(h, g.reshape(1, D), wg, wu, wd)


def _inproj_m_kernel(h_ref, g_ref, w_ref, ws_ref, p_ref, ps_ref, xn_ref):
    @pl.when(pl.program_id(1) == 0)
    def _():
        xn = _rms(h_ref[...], g_ref[...]).astype(BF16)
        xn_ref[...] = xn
        ps_ref[...] = _dot(xn, ws_ref[...])

    p_ref[...] = _dot(xn_ref[...], w_ref[...]).astype(BF16)


def _inproj_m(h, g, w, ws, tm, tn):
    M, D = h.shape
    N = w.shape[1]
    return pl.pallas_call(
        _inproj_m_kernel,
        grid=(M // tm, N // tn),
        in_specs=[
            _row_block(tm, D),
            pl.BlockSpec((1, D), lambda i, j: (0, 0)),
            pl.BlockSpec((D, tn), lambda i, j: (0, j)),
            pl.BlockSpec((D, LANES), lambda i, j: (0, 0)),
        ],
        out_specs=[
            pl.BlockSpec((tm, tn), lambda i, j: (i, j)),
            pl.BlockSpec((tm, LANES), lambda i, j: (i, 0)),
        ],
        out_shape=[jax.ShapeDtypeStruct((M, N), BF16), jax.ShapeDtypeStruct((M, LANES), F32)],
        scratch_shapes=[pltpu.VMEM((tm, D), BF16)],
        compiler_params=_params("parallel", "arbitrary"),
        name="mlstm_inproj",
    )(h, g.reshape(1, D), w, ws)


def _log_sigmoid(x):
    return jnp.minimum(x, 0.0) - jnp.log1p(jnp.exp(-jnp.abs(x)))


def _mlstm_kernel(q_ref, k_ref, v_ref, o_ref, ps_ref, bias_ref, gh_ref, c0_ref, n0_ref, m0_ref,
                  y_ref, c_ref, n_ref, m_ref, *, L, npad, H, DK, DV):
    @pl.when(pl.program_id(1) == 0)
    def _():
        c_ref[...] = c0_ref[...]
        n_ref[...] = n0_ref[...]
        m_ref[...] = m0_ref[...]

    gates = ps_ref[...] + bias_ref[...]
    logf = _log_sigmoid(gates)
    if npad:
        valid = lax.broadcasted_iota(jnp.int32, (L, 1), 0) >= npad
        logf = jnp.where(valid, logf, 0.0)
        gates = jnp.where(valid, gates, LOG_I_PAD)
    ii = lax.broadcasted_iota(jnp.int32, (L, L), 0)
    jj = lax.broadcasted_iota(jnp.int32, (L, L), 1)
    causal = jj <= ii
    cum = jnp.dot(causal.astype(F32), logf, preferred_element_type=F32,
                  precision=lax.Precision.HIGHEST)
    cum_t = cum.T
    gates_t = gates.T

    for h in range(H):
        b_col = cum[:, H + h:H + h + 1]
        b_row = cum_t[H + h:H + h + 1, :]
        li_col = gates[:, h:h + 1]
        li_row = gates_t[h:h + 1, :]
        m_prev = m_ref[h][:, 0:1]
        q = q_ref[:, h * DK:(h + 1) * DK] * (DK ** -0.5)
        k = k_ref[:, h * DK:(h + 1) * DK]
        v = v_ref[:, h * DV:(h + 1) * DV]
        c_old = c_ref[h]
        n_old = n_ref[h]

        dm = jnp.where(causal, b_col - b_row + li_row, -jnp.inf)
        inter = b_col + m_prev
        m_t = jnp.maximum(inter, jnp.max(dm, axis=-1, keepdims=True))
        s = _dot_nt(q, k) * jnp.exp(dm - m_t)
        w_inter = jnp.exp(inter - m_t)
        num = w_inter * _dot(q, c_old.astype(BF16)) + _dot(s.astype(BF16), v)
        qn = jnp.sum(q.astype(F32) * n_old, axis=-1, keepdims=True)
        den = w_inter * qn + jnp.sum(s, axis=-1, keepdims=True)
        hout = num / jnp.maximum(jnp.abs(den), jnp.exp(-m_t))

        b_last = b_col[L - 1:L, :]
        g_row = b_last - b_row + li_row
        g_col = b_last - b_col + li_col
        m_new = jnp.maximum(b_last + m_prev, jnp.max(g_row, axis=-1, keepdims=True))
        decay = jnp.exp(b_last + m_prev - m_new)
        kw = k.astype(F32) * jnp.exp(g_col - m_new)
        c_ref[h] = decay * c_old + _dot(kw.T.astype(BF16), v)
        n_ref[h] = decay * n_old + jnp.sum(kw, axis=0, keepdims=True)
        m_ref[h] = jnp.broadcast_to(m_new, (1, LANES))

        hn = _rms(hout, gh_ref[:, h * DV:(h + 1) * DV])
        og = jax.nn.sigmoid(o_ref[:, h * DV:(h + 1) * DV].astype(F32))
        y_ref[:, h * DV:(h + 1) * DV] = (og * hn).astype(BF16)


def _mlstm(p, ps, bias, gh, c0, n0, m0, B, L, npad):
    M = p.shape[0]
    H = M_HEADS
    DK, DV = c0.shape[1], c0.shape[2]
    NC = M // (B * L)
    row = lambda b, c: b * NC + c
    qk_w, vo_w = H * DK, H * DV
    assert vo_w == 2 * qk_w
    kern = functools.partial(_mlstm_kernel, L=L, npad=npad, H=H, DK=DK, DV=DV)
    return pl.pallas_call(
        kern,
        grid=(B, NC),
        in_specs=[
            pl.BlockSpec((L, qk_w), lambda b, c: (row(b, c), 0)),
            pl.BlockSpec((L, qk_w), lambda b, c: (row(b, c), 1)),
            pl.BlockSpec((L, vo_w), lambda b, c: (row(b, c), 1)),
            pl.BlockSpec((L, vo_w), lambda b, c: (row(b, c), 2)),
            pl.BlockSpec((L, LANES), lambda b, c: (row(b, c), 0)),
            pl.BlockSpec((1, LANES), lambda b, c: (0, 0)),
            pl.BlockSpec((1, vo_w), lambda b, c: (0, 0)),
            pl.BlockSpec((H, DK, DV), lambda b, c: (0, 0, 0)),
            pl.BlockSpec((H, 1, DK), lambda b, c: (0, 0, 0)),
            pl.BlockSpec((H, 1, LANES), lambda b, c: (0, 0, 0)),
        ],
        out_specs=[
            pl.BlockSpec((L, vo_w), lambda b, c: (row(b, c), 0)),
            pl.BlockSpec((None, H, DK, DV), lambda b, c: (b, 0, 0, 0)),
            pl.BlockSpec((None, H, 1, DK), lambda b, c: (b, 0, 0, 0)),
            pl.BlockSpec((None, H, 1, LANES), lambda b, c: (b, 0, 0, 0)),
        ],
        out_shape=[
            jax.ShapeDtypeStruct((M, vo_w), BF16),
            jax.ShapeDtypeStruct((B, H, DK, DV), F32),
            jax.ShapeDtypeStruct((B, H, 1, DK), F32),
            jax.ShapeDtypeStruct((B, H, 1, LANES), F32),
        ],
        compiler_params=_params("parallel", "arbitrary"),
        name="mlstm",
    )(p, p, p, p, ps, bias, gh, c0, n0, m0)


def _outproj_kernel(y_ref, w_ref, h_ref, o_ref):
    o_ref[...] = h_ref[...] + _dot(y_ref[...], w_ref[...])


def _outproj(y, w, h, tm):
    M, D = h.shape
    K = y.shape[1]
    return pl.pallas_call(
        _outproj_kernel,
        grid=(M // tm,),
        in_specs=[
            pl.BlockSpec((tm, K), lambda i: (i, 0)),
            pl.BlockSpec((K, D), lambda i: (0, 0)),
            pl.BlockSpec((tm, D), lambda i: (i, 0)),
        ],
        out_specs=pl.BlockSpec((tm, D), lambda i: (i, 0)),
        out_shape=jax.ShapeDtypeStruct((M, D), F32),
        compiler_params=_params("parallel"),
        name="outproj",
    )(y, w, h)


def _rope(x, tab, half):
    c = tab[:, 0:LANES]
    s1 = tab[:, LANES:2 * LANES]
    s2 = tab[:, 2 * LANES:3 * LANES]
    return x * c + pltpu.roll(x, LANES - half, 1) * s1 + pltpu.roll(x, half, 1) * s2


def _inproj_a_kernel(h_ref, g_ref, w_ref, ws_ref, gq_ref, gk_ref, t128_ref, t64_ref, tki_ref,
                     p_ref, ps_ref, xn_ref, acc_ref, *, nq_tiles, tn, half128, half64):
    j = pl.program_id(1)

    @pl.when(j == 0)
    def _():
        xn = _rms(h_ref[...], g_ref[...]).astype(BF16)
        xn_ref[...] = xn
        ps_ref[...] = _rope(_dot(xn, ws_ref[...]), tki_ref[...], half64)

    nh = tn // LANES
    n_tiles = nq_tiles + 2

    def epilogue(tile, acc):
        if tile < nq_tiles:
            t = t128_ref[...]
            for c in range(nh):
                xs = _rms(acc[:, c * LANES:(c + 1) * LANES], gq_ref[...])
                p_ref[:, c * LANES:(c + 1) * LANES] = _rope(xs, t, half128).astype(BF16)
        elif tile == nq_tiles:
            t = t128_ref[...]
            for c in range(nh // 2):
                xs = _rms(acc[:, c * LANES:(c + 1) * LANES], gk_ref[...])
                p_ref[:, c * LANES:(c + 1) * LANES] = _rope(xs, t, half128).astype(BF16)
            p_ref[:, tn // 2:] = acc[:, tn // 2:].astype(BF16)
        else:
            t = t64_ref[...]
            for c in range(nh):
                p_ref[:, c * LANES:(c + 1) * LANES] = _rope(acc[:, c * LANES:(c + 1) * LANES], t, half64).astype(BF16)

    for s in range(n_tiles + 1):
        @pl.when(j == s)
        def _(s=s):
            if s < n_tiles:
                acc_ref[s % 2] = _dot(xn_ref[...], w_ref[...])
            if s >= 1:
                epilogue(s - 1, acc_ref[(s - 1) % 2])


def _inproj_a(h, g, w, ws, gq, gk, t128, t64, tki, tm):
    M, D = h.shape
    N = w.shape[1]
    tn = 1024
    hd = D // A_HEADS
    assert hd == LANES and D // 4 * 2 == tn and N == D + 2 * tn
    nt = t128.shape[0] // tm
    n_tiles = N // tn
    kern = functools.partial(_inproj_a_kernel, nq_tiles=D // tn, tn=tn,
                             half128=hd // ROPE_FRAC // 2, half64=IDX_DIM // ROPE_FRAC // 2)
    tab = pl.BlockSpec((tm, 3 * LANES), lambda i, j: (i % nt, 0))
    return pl.pallas_call(
        kern,
        grid=(M // tm, n_tiles + 1),
        in_specs=[
            _row_block(tm, D),
            pl.BlockSpec((1, D), lambda i, j: (0, 0)),
            pl.BlockSpec((D, tn), lambda i, j: (0, jnp.minimum(j, n_tiles - 1))),
            pl.BlockSpec((D, LANES), lambda i, j: (0, 0)),
            pl.BlockSpec((1, LANES), lambda i, j: (0, 0)),
            pl.BlockSpec((1, LANES), lambda i, j: (0, 0)),
            tab, tab, tab,
        ],
        out_specs=[
            pl.BlockSpec((tm, tn), lambda i, j: (i, jnp.maximum(j - 1, 0))),
            pl.BlockSpec((tm, LANES), lambda i, j: (i, 0)),
        ],
        out_shape=[jax.ShapeDtypeStruct((M, N), BF16), jax.ShapeDtypeStruct((M, LANES), F32)],
        scratch_shapes=[pltpu.VMEM((tm, D), BF16), pltpu.VMEM((2, tm, tn), F32)],
        compiler_params=_params("parallel", "arbitrary"),
        name="dsa_inproj",
    )(h, g.reshape(1, D), w, ws, gq, gk, t128, t64, tki)


def _rope_table(pos, d, extra_scale_lanes=None, extra_scale=1.0, reps=None):
    rot = d // ROPE_FRAC
    half = rot // 2
    inv = 1.0 / (ROPE_THETA ** (jnp.arange(0, rot, 2, dtype=F32) / rot))
    ang = pos[:, None] * inv[None, :]
    cos, sin = jnp.cos(ang), jnp.sin(ang)
    T = pos.shape[0]
    one = jnp.ones((T, d - rot), F32)
    zero = lambda n: jnp.zeros((T, n), F32)
    c = jnp.concatenate([cos, cos, one], axis=1)
    s1 = jnp.concatenate([-sin, zero(d - half)], axis=1)
    s2 = jnp.concatenate([zero(half), sin, zero(d - rot)], axis=1)
    if reps is None:
        reps = LANES // d
    c, s1, s2 = (jnp.tile(a, (1, reps)) for a in (c, s1, s2))
    fill = LANES - reps * d
    if fill:
        tail = jnp.ones((T, fill), F32)
        if extra_scale_lanes is not None:
            lo, hi = extra_scale_lanes
            lane = jnp.arange(reps * d, LANES)
            tail = jnp.where((lane >= lo) & (lane < hi), extra_scale, 1.0)[None, :] * tail
        c = jnp.concatenate([c, tail], axis=1)
        s1 = jnp.concatenate([s1, zero(fill)], axis=1)
        s2 = jnp.concatenate([s2, zero(fill)], axis=1)
    return jnp.concatenate([c, s1, s2], axis=1)


def _float_key(x):
    bits = lax.bitcast_convert_type(x, jnp.int32)
    return jnp.where(bits < 0, bits ^ jnp.int32(0x7FFFFFFF), bits)


def _dsa_tile(q_ref, qi_ref, psq_ref, o_ref, kcat, vcat, kie, kio, key_ref, bias_ref, qs, os,
              *, tq, row0, nk, topk, n_meta):
    NCH = nk // LANES
    hd = LANES
    G = A_KV_HEADS
    R = A_HEADS // G

    wi = psq_ref[...]
    ke = kie[0:nk, :]
    ko = kio[0:nk, :]
    score = jnp.zeros((tq, nk), F32)
    for p in range(IDX_HEADS // 2):
        qp = qi_ref[:, p * LANES:(p + 1) * LANES]
        c0 = IDX_DIM + 2 * p
        score += jnp.maximum(_dot_nt(qp, ke), 0.0) * wi[:, c0:c0 + 1]
        score += jnp.maximum(_dot_nt(qp, ko), 0.0) * wi[:, c0 + 1:c0 + 2]

    col = lax.broadcasted_iota(jnp.int32, (tq, nk), 1)
    row = lax.broadcasted_iota(jnp.int32, (tq, nk), 0) + row0
    adm = jnp.where(col < LANES, col - n_meta, col - LANES - row - 1) < 0
    score = jnp.where(score == 0.0, 0.0, score)
    key_ref[:, 0:nk] = _float_key(jnp.where(adm, score, -jnp.inf))

    ones = jnp.ones((LANES, LANES), BF16)
    kf = float(topk)

    def count_ge(cand, r0=0, rows=tq):
        acc = jnp.zeros((rows, LANES), F32)
        for c in range(NCH):
            acc += jnp.where(key_ref[r0:r0 + rows, c * LANES:(c + 1) * LANES] >= cand, 1.0, 0.0)
        return _dot(acc.astype(BF16), ones)

    nblk = SEARCH_ROW_BLOCKS
    rb = tq // nblk

    def search(trip, bases):
        bases = list(bases)
        for u in range(SEARCH_UNROLL):
            bit = jnp.int32(31) - (trip.astype(jnp.int32) * SEARCH_UNROLL + u)
            step = lax.shift_left(jnp.int32(1), bit)
            for blk in range(nblk):
                cand = bases[blk] + step
                bases[blk] = jnp.where(count_ge(cand, blk * rb, rb) >= kf, cand, bases[blk])
        return tuple(bases)

    init = tuple(jnp.full((rb, LANES), INT_MIN, jnp.int32) for _ in range(nblk))
    base = jnp.concatenate(lax.fori_loop(0, 32 // SEARCH_UNROLL, search, init), axis=0)
    thr = jnp.maximum(base, NEG_INF_KEY + 1)
    cnt = count_ge(thr)
    for c in range(NCH):
        sl = slice(c * LANES, (c + 1) * LANES)
        bias_ref[:, sl] = jnp.where(key_ref[:, sl] >= thr, 0.0, -jnp.inf)

    @pl.when(jnp.max(cnt) > kf)
    def _():
        need = kf - count_ge(thr + 1)
        rr = lax.broadcasted_iota(jnp.int32, (LANES, LANES), 0)
        cc = lax.broadcasted_iota(jnp.int32, (LANES, LANES), 1)
        tri = jnp.where(rr <= cc, 1.0, 0.0).astype(BF16)
        run = jnp.zeros((tq, LANES), F32)
        for c in range(NCH):
            sl = slice(c * LANES, (c + 1) * LANES)
            kc = key_ref[:, sl]
            eq = kc == thr
            eqb = jnp.where(eq, 1.0, 0.0).astype(BF16)
            rank = run + _dot(eqb, tri)
            keep_eq = jnp.where(rank <= need, 0.0, -jnp.inf)
            bias_ref[:, sl] = jnp.where(kc > thr, 0.0, jnp.where(eq, keep_eq, -jnp.inf))
            run = run + _dot(eqb, ones)

    for g in range(G):
        qs[g] = q_ref[:, g * R * hd:(g + 1) * R * hd]

    def group(g, carry):
        kg = kcat[g, 0:nk, :]
        vg = vcat[g, 0:nk, :]
        for r in range(R):
            logits = _dot_nt(qs[g, :, r * hd:(r + 1) * hd], kg) + bias_ref[:, 0:nk]
            mx = jnp.max(logits, axis=-1, keepdims=True)
            pv = _dot(jnp.exp2(logits - mx).astype(BF16), vg)
            os[g, :, r * hd:(r + 1) * hd] = (pv[:, 0:hd] / pv[:, hd:hd + 1]).astype(BF16)
        return carry

    lax.fori_loop(0, G, group, 0)
    for g in range(G):
        o_ref[:, g * R * hd:(g + 1) * R * hd] = os[g]


def _dsa_kernel(q_ref, qi_ref, psq_ref, k_ref, v_ref, psk_ref, km_ref, vm_ref, psm_ref, o_ref,
                kcat, vcat, kie, kio, key_ref, bias_ref, qs, os, *, tq, S, topk, n_meta):
    i = pl.program_id(1)
    NK = LANES + S
    hd = LANES
    G = A_KV_HEADS

    @pl.when(i == 0)
    def _():
        lane = lax.broadcasted_iota(jnp.int32, (NK, hd), 1)
        ones_col = jnp.where(lane == 0, 1.0, 0.0).astype(BF16)
        for g in range(G):
            sl = slice(g * hd, (g + 1) * hd)
            kcat[g, 0:LANES, :] = km_ref[:, sl]
            kcat[g, LANES:, :] = k_ref[:, sl]
            vcat[g, 0:LANES, 0:hd] = vm_ref[:, sl]
            vcat[g, LANES:, 0:hd] = v_ref[:, sl]
            vcat[g, :, hd:] = ones_col
        for dst0, dst1, src in ((0, LANES, psm_ref), (LANES, NK, psk_ref)):
            a = src[...]
            lane = lax.broadcasted_iota(jnp.int32, a.shape, 1)
            even = jnp.where(lane < IDX_DIM, a, 0.0)
            kie[dst0:dst1, :] = even.astype(BF16)
            kio[dst0:dst1, :] = pltpu.roll(even, IDX_DIM, 1).astype(BF16)

    nq = S // tq
    bounds = sorted({-(-nq * (v + 1) // DSA_VARIANTS) for v in range(DSA_VARIANTS)})
    lo = 0
    for hi in bounds:
        pl.when((i >= lo) & (i < hi))(functools.partial(
            _dsa_tile, q_ref, qi_ref, psq_ref, o_ref, kcat, vcat, kie, kio, key_ref, bias_ref, qs, os,
            tq=tq, row0=i * tq, nk=LANES + hi * tq, topk=topk, n_meta=n_meta))
        lo = hi


def _dsa(p, ps, pm, psm, B, S, tq, topk):
    M, N = p.shape
    D = N // 2
    kv_w = D // 4
    nq = S // tq
    NK = LANES + S
    kern = functools.partial(_dsa_kernel, tq=tq, S=S, topk=topk, n_meta=N_META)
    return pl.pallas_call(
        kern,
        grid=(B, nq),
        in_specs=[
            pl.BlockSpec((tq, D), lambda b, i: (b * nq + i, 0)),
            pl.BlockSpec((tq, 1024), lambda b, i: (b * nq + i, (D + 2 * kv_w) // 1024)),
            pl.BlockSpec((tq, LANES), lambda b, i: (b * nq + i, 0)),
            pl.BlockSpec((S, kv_w), lambda b, i: (b, D // kv_w)),
            pl.BlockSpec((S, kv_w), lambda b, i: (b, D // kv_w + 1)),
            pl.BlockSpec((S, LANES), lambda b, i: (b, 0)),
            pl.BlockSpec((LANES, kv_w), lambda b, i: (0, D // kv_w)),
            pl.BlockSpec((LANES, kv_w), lambda b, i: (0, D // kv_w + 1)),
            pl.BlockSpec((LANES, LANES), lambda b, i: (0, 0)),
        ],
        out_specs=pl.BlockSpec((tq, D), lambda b, i: (b * nq + i, 0)),
        out_shape=jax.ShapeDtypeStruct((M, D), BF16),
        scratch_shapes=[
            pltpu.VMEM((A_KV_HEADS, NK, LANES), BF16),
            pltpu.VMEM((A_KV_HEADS, NK, 2 * LANES), BF16),
            pltpu.VMEM((NK, LANES), BF16),
            pltpu.VMEM((NK, LANES), BF16),
            pltpu.VMEM((tq, NK), jnp.int32),
            pltpu.VMEM((tq, NK), F32),
            pltpu.VMEM((A_KV_HEADS, tq, D // A_KV_HEADS), BF16),
            pltpu.VMEM((A_KV_HEADS, tq, D // A_KV_HEADS), BF16),
        ],
        compiler_params=_params("parallel", "arbitrary"),
        name="dsa_attention",
    )(p, p, ps, p, p, ps, pm, pm, psm)


def _row_tile(m, pref):
    return pref if m % pref == 0 else m


def kernel(x, meta_tokens, ffn1_norm, ffn1_w_gate, ffn1_w_up, ffn1_w_down, mix_norm, ffn2_norm, ffn2_w_gate, ffn2_w_up, ffn2_w_down, mlstm_w_in, mlstm_b_i, mlstm_b_f, mlstm_head_norm, mlstm_w_out, dsa_w_in, dsa_q_norm, dsa_k_norm, dsa_w_out):
    B, S, D = x.shape
    depth = ffn1_norm.shape[0]
    assert depth == 2 and meta_tokens.shape == (N_META, D)
    F = ffn1_w_gate.shape[-1]
    H, DK, DV = M_HEADS, D // (2 * M_HEADS), D // M_HEADS
    hd = D // A_HEADS
    L = 256
    TM = 1024
    TMO = 512
    TF = 512
    assert S % L == 0 and S % TM == 0 and (B * S) % TMO == 0 and F % TF == 0

    hx = x.reshape(B * S, D)
    hm = meta_tokens.astype(x.dtype)
    bf = lambda w: w.astype(BF16)

    def ffn(h, g, wg, wu, wd):
        return _ffn(h, g, wg, wu, wd, _row_tile(h.shape[0], TM), TF)

    def pad_cols(w):
        return jnp.pad(w, ((0, 0), (0, LANES - w.shape[1])))

    def pad_rows(a, n, front=False):
        r = n - a.shape[0]
        return jnp.pad(a, ((r, 0) if front else (0, r), (0, 0)))

    w = [bf(ffn1_w_gate[0]), bf(ffn1_w_up[0]), bf(ffn1_w_down[0])]
    hx = ffn(hx, ffn1_norm[0], *w)
    hm = ffn(hm, ffn1_norm[0], *w)

    n_wide = 2 * H * DK + 2 * H * DV
    w_wide = bf(mlstm_w_in[0][:, :n_wide])
    w_gate = bf(pad_cols(mlstm_w_in[0][:, n_wide:]))
    px, psx = _inproj_m(hx, mix_norm[0], w_wide, w_gate, TM, 1024)
    pm, psm = _inproj_m(hm, mix_norm[0], w_wide, w_gate, N_META, 1024)

    bias = pad_cols(jnp.concatenate([mlstm_b_i[0], mlstm_b_f[0]]).astype(F32)[None, :])
    gh = mlstm_head_norm[0].astype(F32).reshape(1, H * DV)
    zc = jnp.zeros((H, DK, DV), F32)
    zn = jnp.zeros((H, 1, DK), F32)
    zm = jnp.zeros((H, 1, LANES), F32)
    ym, c0, n0, m0 = _mlstm(pad_rows(pm, L, True), pad_rows(psm, L, True), bias, gh, zc, zn, zm,
                            1, L, L - N_META)
    yx, _, _, _ = _mlstm(px, psx, bias, gh, c0[0], n0[0], m0[0], B, L, 0)
    w_out = bf(mlstm_w_out[0])
    hx = _outproj(yx, w_out, hx, TMO)
    hm = _outproj(ym[L - N_META:], w_out, hm, N_META)

    w = [bf(ffn2_w_gate[0]), bf(ffn2_w_up[0]), bf(ffn2_w_down[0])]
    hx = ffn(hx, ffn2_norm[0], *w)
    hm = ffn(hm, ffn2_norm[0], *w)

    w = [bf(ffn1_w_gate[1]), bf(ffn1_w_up[1]), bf(ffn1_w_down[1])]
    hx = ffn(hx, ffn1_norm[1], *w)
    hm = ffn(hm, ffn1_norm[1], *w)

    n_wide = A_HEADS * hd + 2 * A_KV_HEADS * hd + IDX_HEADS * IDX_DIM
    w_wide = bf(dsa_w_in[0][:, :n_wide])
    w_idx = bf(pad_cols(dsa_w_in[0][:, n_wide:]))
    gq = (dsa_q_norm[0].astype(F32) * (hd ** -0.5 * float(np.log2(np.e))))[None, :]
    gk = dsa_k_norm[0].astype(F32)[None, :]
    wi_scale = IDX_HEADS ** -0.5 * IDX_DIM ** -0.5
    pos = jnp.arange(N_META + S, dtype=F32)
    t128 = _rope_table(pos, hd)
    t64 = _rope_table(pos, IDX_DIM)
    tki = _rope_table(pos, IDX_DIM, (IDX_DIM, IDX_DIM + IDX_HEADS), wi_scale, reps=1)
    tabs_x = [t[N_META:] for t in (t128, t64, tki)]
    tabs_m = [t[:N_META] for t in (t128, t64, tki)]
    px, psx = _inproj_a(hx, mix_norm[1], w_wide, w_idx, gq, gk, *tabs_x, TM)
    pm, psm = _inproj_a(hm, mix_norm[1], w_wide, w_idx, gq, gk, *tabs_m, N_META)

    topk = min(TOPK_MAX, (N_META + S - N_META) // 4)
    ox = _dsa(px, psx, pad_rows(pm, LANES), pad_rows(psm, LANES), B, S, 256, topk)
    hx = _outproj(ox, bf(dsa_w_out[0]), hx, TMO)

    w = [bf(ffn2_w_gate[1]), bf(ffn2_w_up[1]), bf(ffn2_w_down[1])]
    hx = ffn(hx, ffn2_norm[1], *w)
    return hx.reshape(B, S, D)
```

```python
import functools

import numpy as np
import jax
import jax.numpy as jnp
from jax import lax
from jax.experimental import pallas as pl
from jax.experimental.pallas import tpu as pltpu

F32 = jnp.float32
BF16 = jnp.bfloat16

N_META = 16
FFN_HALF = 0.5
NORM_EPS = 1e-6
ROPE_THETA = 500000.0
ROPE_FRAC = 4
M_HEADS = 4
A_HEADS = 16
A_KV_HEADS = 4
IDX_HEADS = 16
IDX_DIM = 64
TOPK_MAX = 256
LOG_I_PAD = -1e30

LANES = 128
VMEM_LIMIT = 60 * 1024 * 1024
DSA_VARIANTS = 3
SEARCH_ROW_BLOCKS = 2
SEARCH_UNROLL = 4
ATT_HEADS_PER_TRIP = 4
INT_MIN = int(np.iinfo(np.int32).min)
NEG_INF_KEY = int(np.array(-np.inf, np.float32).view(np.int32) ^ np.int32(0x7FFFFFFF))


def _params(*sem):
    return pltpu.CompilerParams(dimension_semantics=sem, vmem_limit_bytes=VMEM_LIMIT)


def _row_block(tm, d):
    return pl.BlockSpec((tm, d), lambda i, j: (i, 0))


def _rms(x, g):
    return x * lax.rsqrt(jnp.mean(x * x, axis=-1, keepdims=True) + NORM_EPS) * g


def _dot(a, b):
    return jnp.dot(a, b, preferred_element_type=F32)


def _dot_nt(a, b):
    return lax.dot_general(a, b, (((1,), (1,)), ((), ())), preferred_element_type=F32)


def _ffn_kernel(h_ref, g_ref, wg_ref, wu_ref, wd_ref, o_ref, xn_ref):
    @pl.when(pl.program_id(1) == 0)
    def _():
        h = h_ref[...]
        xn_ref[...] = _rms(h, g_ref[...]).astype(BF16)
        o_ref[...] = h

    xn = xn_ref[...]
    gate = _dot(xn, wg_ref[...])
    up = _dot(xn, wu_ref[...])
    act = (gate * jax.nn.sigmoid(gate)) * (up * FFN_HALF)
    o_ref[...] += _dot(act.astype(BF16), wd_ref[...])


def _ffn(h, g, wg, wu, wd, tm, tf):
    M, D = h.shape
    F = wg.shape[1]
    return pl.pallas_call(
        _ffn_kernel,
        grid=(M // tm, F // tf),
        in_specs=[
            _row_block(tm, D),
            pl.BlockSpec((1, D), lambda i, j: (0, 0)),
            pl.BlockSpec((D, tf), lambda i, j: (0, j)),
            pl.BlockSpec((D, tf), lambda i, j: (0, j)),
            pl.BlockSpec((tf, D), lambda i, j: (j, 0)),
        ],
        out_specs=pl.BlockSpec((tm, D), lambda i, j: (i, 0)),
        out_shape=jax.ShapeDtypeStruct((M, D), F32),
        scratch_shapes=[pltpu.VMEM((tm, D), BF16)],
        compiler_params=_params("parallel", "arbitrary"),
        name="ffn",
    )(h, g.reshape(1, D), wg, wu, wd)


def _inproj_m_kernel(h_ref, g_ref, w_ref, ws_ref, p_ref, ps_ref, xn_ref):
    @pl.when(pl.program_id(1) == 0)
    def _():
        xn = _rms(h_ref[...], g_ref[...]).astype(BF16)
        xn_ref[...] = xn
        ps_ref[...] = _dot(xn, ws_ref[...])

    p_ref[...] = _dot(xn_ref[...], w_ref[...]).astype(BF16)


def _inproj_m(h, g, w, ws, tm, tn):
    M, D = h.shape
    N = w.shape[1]
    return pl.pallas_call(
        _inproj_m_kernel,
        grid=(M // tm, N // tn),
        in_specs=[
            _row_block(tm, D),
            pl.BlockSpec((1, D), lambda i, j: (0, 0)),
            pl.BlockSpec((D, tn), lambda i, j: (0, j)),
            pl.BlockSpec((D, LANES), lambda i, j: (0, 0)),
        ],
        out_specs=[
            pl.BlockSpec((tm, tn), lambda i, j: (i, j)),
            pl.BlockSpec((tm, LANES), lambda i, j: (i, 0)),
        ],
        out_shape=[jax.ShapeDtypeStruct((M, N), BF16), jax.ShapeDtypeStruct((M, LANES), F32)],
        scratch_shapes=[pltpu.VMEM((tm, D), BF16)],
        compiler_params=_params("parallel", "arbitrary"),
        name="mlstm_inproj",
    )(h, g.reshape(1, D), w, ws)


def _log_sigmoid(x):
    return jnp.minimum(x, 0.0) - jnp.log1p(jnp.exp(-jnp.abs(x)))


def _mlstm_kernel(q_ref, k_ref, v_ref, o_ref, ps_ref, bias_ref, gh_ref, c0_ref, n0_ref, m0_ref,
                  y_ref, c_ref, n_ref, m_ref, *, L, npad, H, DK, DV):
    @pl.when(pl.program_id(1) == 0)
    def _():
        c_ref[...] = c0_ref[...]
        n_ref[...] = n0_ref[...]
        m_ref[...] = m0_ref[...]

    gates = ps_ref[...] + bias_ref[...]
    logf = _log_sigmoid(gates)
    if npad:
        valid = lax.broadcasted_iota(jnp.int32, (L, 1), 0) >= npad
        logf = jnp.where(valid, logf, 0.0)
        gates = jnp.where(valid, gates, LOG_I_PAD)
    ii = lax.broadcasted_iota(jnp.int32, (L, L), 0)
    jj = lax.broadcasted_iota(jnp.int32, (L, L), 1)
    causal = jj <= ii
    cum = jnp.dot(causal.astype(F32), logf, preferred_element_type=F32,
                  precision=lax.Precision.HIGHEST)
    cum_t = cum.T
    gates_t = gates.T

    for h in range(H):
        b_col = cum[:, H + h:H + h + 1]
        b_row = cum_t[H + h:H + h + 1, :]
        li_col = gates[:, h:h + 1]
        li_row = gates_t[h:h + 1, :]
        m_prev = m_ref[h][:, 0:1]
        q = q_ref[:, h * DK:(h + 1) * DK] * (DK ** -0.5)
        k = k_ref[:, h * DK:(h + 1) * DK]
        v = v_ref[:, h * DV:(h + 1) * DV]
        c_old = c_ref[h]
        n_old = n_ref[h]

        dm = jnp.where(causal, b_col - b_row + li_row, -jnp.inf)
        inter = b_col + m_prev
        m_t = jnp.maximum(inter, jnp.max(dm, axis=-1, keepdims=True))
        s = _dot_nt(q, k) * jnp.exp(dm - m_t)
        w_inter = jnp.exp(inter - m_t)
        num = w_inter * _dot(q, c_old.astype(BF16)) + _dot(s.astype(BF16), v)
        qn = jnp.sum(q.astype(F32) * n_old, axis=-1, keepdims=True)
        den = w_inter * qn + jnp.sum(s, axis=-1, keepdims=True)
        hout = num / jnp.maximum(jnp.abs(den), jnp.exp(-m_t))

        b_last = b_col[L - 1:L, :]
        g_row = b_last - b_row + li_row
        g_col = b_last - b_col + li_col
        m_new = jnp.maximum(b_last + m_prev, jnp.max(g_row, axis=-1, keepdims=True))
        decay = jnp.exp(b_last + m_prev - m_new)
        kw = k.astype(F32) * jnp.exp(g_col - m_new)
        c_ref[h] = decay * c_old + _dot(kw.T.astype(BF16), v)
        n_ref[h] = decay * n_old + jnp.sum(kw, axis=0, keepdims=True)
        m_ref[h] = jnp.broadcast_to(m_new, (1, LANES))

        hn = _rms(hout, gh_ref[:, h * DV:(h + 1) * DV])
        og = jax.nn.sigmoid(o_ref[:, h * DV:(h + 1) * DV].astype(F32))
        y_ref[:, h * DV:(h + 1) * DV] = (og * hn).astype(BF16)


def _mlstm(p, ps, bias, gh, c0, n0, m0, B, L, npad):
    M = p.shape[0]
    H = M_HEADS
    DK, DV = c0.shape[1], c0.shape[2]
    NC = M // (B * L)
    row = lambda b, c: b * NC + c
    qk_w, vo_w = H * DK, H * DV
    assert vo_w == 2 * qk_w
    kern = functools.partial(_mlstm_kernel, L=L, npad=npad, H=H, DK=DK, DV=DV)
    return pl.pallas_call(
        kern,
        grid=(B, NC),
        in_specs=[
            pl.BlockSpec((L, qk_w), lambda b, c: (row(b, c), 0)),
            pl.BlockSpec((L, qk_w), lambda b, c: (row(b, c), 1)),
            pl.BlockSpec((L, vo_w), lambda b, c: (row(b, c), 1)),
            pl.BlockSpec((L, vo_w), lambda b, c: (row(b, c), 2)),
            pl.BlockSpec((L, LANES), lambda b, c: (row(b, c), 0)),
            pl.BlockSpec((1, LANES), lambda b, c: (0, 0)),
            pl.BlockSpec((1, vo_w), lambda b, c: (0, 0)),
            pl.BlockSpec((H, DK, DV), lambda b, c: (0, 0, 0)),
            pl.BlockSpec((H, 1, DK), lambda b, c: (0, 0, 0)),
            pl.BlockSpec((H, 1, LANES), lambda b, c: (0, 0, 0)),
        ],
        out_specs=[
            pl.BlockSpec((L, vo_w), lambda b, c: (row(b, c), 0)),
            pl.BlockSpec((None, H, DK, DV), lambda b, c: (b, 0, 0, 0)),
            pl.BlockSpec((None, H, 1, DK), lambda b, c: (b, 0, 0, 0)),
            pl.BlockSpec((None, H, 1, LANES), lambda b, c: (b, 0, 0, 0)),
        ],
        out_shape=[
            jax.ShapeDtypeStruct((M, vo_w), BF16),
            jax.ShapeDtypeStruct((B, H, DK, DV), F32),
            jax.ShapeDtypeStruct((B, H, 1, DK), F32),
            jax.ShapeDtypeStruct((B, H, 1, LANES), F32),
        ],
        compiler_params=_params("parallel", "arbitrary"),
        name="mlstm",
    )(p, p, p, p, ps, bias, gh, c0, n0, m0)


def _outproj_kernel(y_ref, w_ref, h_ref, o_ref):
    o_ref[...] = h_ref[...] + _dot(y_ref[...], w_ref[...])


def _outproj(y, w, h, tm):
    M, D = h.shape
    K = y.shape[1]
    return pl.pallas_call(
        _outproj_kernel,
        grid=(M // tm,),
        in_specs=[
            pl.BlockSpec((tm, K), lambda i: (i, 0)),
            pl.BlockSpec((K, D), lambda i: (0, 0)),
            pl.BlockSpec((tm, D), lambda i: (i, 0)),
        ],
        out_specs=pl.BlockSpec((tm, D), lambda i: (i, 0)),
        out_shape=jax.ShapeDtypeStruct((M, D), F32),
        compiler_params=_params("parallel"),
        name="outproj",
    )(y, w, h)


def _rope(x, tab, half):
    c = tab[:, 0:LANES]
    s1 = tab[:, LANES:2 * LANES]
    s2 = tab[:, 2 * LANES:3 * LANES]
    return x * c + pltpu.roll(x, LANES - half, 1) * s1 + pltpu.roll(x, half, 1) * s2


def _inproj_a_kernel(h_ref, g_ref, w_ref, ws_ref, gq_ref, gk_ref, t128_ref, t64_ref, tki_ref,
                     p_ref, ps_ref, xn_ref, acc_ref, *, nq_tiles, tn, half128, half64):
    j = pl.program_id(1)

    @pl.when(j == 0)
    def _():
        xn = _rms(h_ref[...], g_ref[...]).astype(BF16)
        xn_ref[...] = xn
        ps_ref[...] = _rope(_dot(xn, ws_ref[...]), tki_ref[...], half64)

    nh = tn // LANES
    n_tiles = nq_tiles + 2

    def epilogue(tile, acc):
        if tile < nq_tiles:
            t = t128_ref[...]
            for c in range(nh):
                xs = _rms(acc[:, c * LANES:(c + 1) * LANES], gq_ref[...])
                p_ref[:, c * LANES:(c + 1) * LANES] = _rope(xs, t, half128).astype(BF16)
        elif tile == nq_tiles:
            t = t128_ref[...]
            for c in range(nh // 2):
                xs = _rms(acc[:, c * LANES:(c + 1) * LANES], gk_ref[...])
                p_ref[:, c * LANES:(c + 1) * LANES] = _rope(xs, t, half128).astype(BF16)
            p_ref[:, tn // 2:] = acc[:, tn // 2:].astype(BF16)
        else:
            t = t64_ref[...]
            for c in range(nh):
                p_ref[:, c * LANES:(c + 1) * LANES] = _rope(acc[:, c * LANES:(c + 1) * LANES], t, half64).astype(BF16)

    for s in range(n_tiles + 1):
        @pl.when(j == s)
        def _(s=s):
            if s < n_tiles:
                acc_ref[s % 2] = _dot(xn_ref[...], w_ref[...])
            if s >= 1:
                epilogue(s - 1, acc_ref[(s - 1) % 2])


def _inproj_a(h, g, w, ws, gq, gk, t128, t64, tki, tm):
    M, D = h.shape
    N = w.shape[1]
    tn = 1024
    hd = D // A_HEADS
    assert hd == LANES and D // 4 * 2 == tn and N == D + 2 * tn
    nt = t128.shape[0] // tm
    n_tiles = N // tn
    kern = functools.partial(_inproj_a_kernel, nq_tiles=D // tn, tn=tn,
                             half128=hd // ROPE_FRAC // 2, half64=IDX_DIM // ROPE_FRAC // 2)
    tab = pl.BlockSpec((tm, 3 * LANES), lambda i, j: (i % nt, 0))
    return pl.pallas_call(
        kern,
        grid=(M // tm, n_tiles + 1),
        in_specs=[
            _row_block(tm, D),
            pl.BlockSpec((1, D), lambda i, j: (0, 0)),
            pl.BlockSpec((D, tn), lambda i, j: (0, jnp.minimum(j, n_tiles - 1))),
            pl.BlockSpec((D, LANES), lambda i, j: (0, 0)),
            pl.BlockSpec((1, LANES), lambda i, j: (0, 0)),
            pl.BlockSpec((1, LANES), lambda i, j: (0, 0)),
            tab, tab, tab,
        ],
        out_specs=[
            pl.BlockSpec((tm, tn), lambda i, j: (i, jnp.maximum(j - 1, 0))),
            pl.BlockSpec((tm, LANES), lambda i, j: (i, 0)),
        ],
        out_shape=[jax.ShapeDtypeStruct((M, N), BF16), jax.ShapeDtypeStruct((M, LANES), F32)],
        scratch_shapes=[pltpu.VMEM((tm, D), BF16), pltpu.VMEM((2, tm, tn), F32)],
        compiler_params=_params("parallel", "arbitrary"),
        name="dsa_inproj",
    )(h, g.reshape(1, D), w, ws, gq, gk, t128, t64, tki)


def _rope_table(pos, d, extra_scale_lanes=None, extra_scale=1.0, reps=None):
    rot = d // ROPE_FRAC
    half = rot // 2
    inv = 1.0 / (ROPE_THETA ** (jnp.arange(0, rot, 2, dtype=F32) / rot))
    ang = pos[:, None] * inv[None, :]
    cos, sin = jnp.cos(ang), jnp.sin(ang)
    T = pos.shape[0]
    one = jnp.ones((T, d - rot), F32)
    zero = lambda n: jnp.zeros((T, n), F32)
    c = jnp.concatenate([cos, cos, one], axis=1)
    s1 = jnp.concatenate([-sin, zero(d - half)], axis=1)
    s2 = jnp.concatenate([zero(half), sin, zero(d - rot)], axis=1)
    if reps is None:
        reps = LANES // d
    c, s1, s2 = (jnp.tile(a, (1, reps)) for a in (c, s1, s2))
    fill = LANES - reps * d
    if fill:
        tail = jnp.ones((T, fill), F32)
        if extra_scale_lanes is not None:
            lo, hi = extra_scale_lanes
            lane = jnp.arange(reps * d, LANES)
            tail = jnp.where((lane >= lo) & (lane < hi), extra_scale, 1.0)[None, :] * tail
        c = jnp.concatenate([c, tail], axis=1)
        s1 = jnp.concatenate([s1, zero(fill)], axis=1)
        s2 = jnp.concatenate([s2, zero(fill)], axis=1)
    return jnp.concatenate([c, s1, s2], axis=1)


def _float_key(x):
    bits = lax.bitcast_convert_type(x, jnp.int32)
    return jnp.where(bits < 0, bits ^ jnp.int32(0x7FFFFFFF), bits)


def _dsa_keys(qi_ref, psq_ref, kie, kio, key_ref, *, tq, row0, nk, n_meta):
    wi = psq_ref[...]
    ke = kie[0:nk, :]
    ko = kio[0:nk, :]
    score = jnp.zeros((tq, nk), F32)
    for p in range(IDX_HEADS // 2):
        qp = qi_ref[:, p * LANES:(p + 1) * LANES]
        c0 = IDX_DIM + 2 * p
        score += jnp.maximum(_dot_nt(qp, ke), 0.0) * wi[:, c0:c0 + 1]
        score += jnp.maximum(_dot_nt(qp, ko), 0.0) * wi[:, c0 + 1:c0 + 2]
    col = lax.broadcasted_iota(jnp.int32, (tq, nk), 1)
    row = lax.broadcasted_iota(jnp.int32, (tq, nk), 0) + row0
    adm = jnp.where(col < LANES, col - n_meta, col - LANES - row - 1) < 0
    score = jnp.where(score == 0.0, 0.0, score)
    key_ref[:, 0:nk] = _float_key(jnp.where(adm, score, -jnp.inf))


def _count_ge(key_ref, cand, r0, rows, nch):
    acc = jnp.zeros((rows, LANES), F32)
    for c in range(nch):
        acc += jnp.where(key_ref[r0:r0 + rows, c * LANES:(c + 1) * LANES] >= cand, 1.0, 0.0)
    return jnp.sum(acc, axis=-1, keepdims=True)


def _search_trip(key_ref, trip, bases, *, tq, nk, topk):
    rb = tq // SEARCH_ROW_BLOCKS
    bases = list(bases)
    for u in range(SEARCH_UNROLL):
        bit = jnp.int32(31) - (trip.astype(jnp.int32) * SEARCH_UNROLL + u)
        step = lax.shift_left(jnp.int32(1), bit)
        for blk in range(SEARCH_ROW_BLOCKS):
            cand = bases[blk] + step
            cnt = _count_ge(key_ref, cand, blk * rb, rb, nk // LANES)
            bases[blk] = jnp.where(cnt >= float(topk), cand, bases[blk])
    return tuple(bases)


def _search_init(tq):
    return tuple(jnp.full((tq // SEARCH_ROW_BLOCKS, 1), INT_MIN, jnp.int32) for _ in range(SEARCH_ROW_BLOCKS))


def _dsa_mask(key_ref, bias_ref, slot, bases, *, tq, nk, nk_all, topk):
    NCH = nk // LANES
    kf = float(topk)
    ones = jnp.ones((LANES, LANES), BF16)
    thr = jnp.maximum(jnp.concatenate(bases, axis=0), NEG_INF_KEY + 1)
    cnt = _count_ge(key_ref, thr, 0, tq, NCH)
    for c in range(NCH):
        sl = slice(c * LANES, (c + 1) * LANES)
        bias_ref[slot, :, sl] = jnp.where(key_ref[:, sl] >= thr, 0.0, -jnp.inf)
    if nk_all > nk:
        bias_ref[slot, :, nk:nk_all] = jnp.full((tq, nk_all - nk), -jnp.inf, F32)

    @pl.when(jnp.max(cnt) > kf)
    def _():
        need = kf - _count_ge(key_ref, thr + 1, 0, tq, NCH)
        rr = lax.broadcasted_iota(jnp.int32, (LANES, LANES), 0)
        cc = lax.broadcasted_iota(jnp.int32, (LANES, LANES), 1)
        tri = jnp.where(rr <= cc, 1.0, 0.0).astype(BF16)
        run = jnp.zeros((tq, LANES), F32)
        for c in range(NCH):
            sl = slice(c * LANES, (c + 1) * LANES)
            kc = key_ref[:, sl]
            eq = kc == thr
            eqb = jnp.where(eq, 1.0, 0.0).astype(BF16)
            rank = run + _dot(eqb, tri)
            keep_eq = jnp.where(rank <= need, 0.0, -jnp.inf)
            bias_ref[slot, :, sl] = jnp.where(kc > thr, 0.0, jnp.where(eq, keep_eq, -jnp.inf))
            run = run + _dot(eqb, ones)


def _attn_heads(qs, kcat, vcat, bias_ref, slot, os, trip, *, nk):
    hd = LANES
    per_group = (A_HEADS // A_KV_HEADS) // ATT_HEADS_PER_TRIP
    g = trip // per_group
    kg = kcat[g, 0:nk, :]
    vg = vcat[g, 0:nk, :]
    for r in range(ATT_HEADS_PER_TRIP):
        h = trip * ATT_HEADS_PER_TRIP + r
        logits = _dot_nt(qs[h], kg) + bias_ref[slot, :, 0:nk]
        mx = jnp.max(logits, axis=-1, keepdims=True)
        pv = _dot(jnp.exp2(logits - mx).astype(BF16), vg)
        os[h] = (pv[:, 0:hd] / pv[:, hd:hd + 1]).astype(BF16)


def _dsa_step(i, q_ref, qi_ref, psq_ref, o_ref, kcat, vcat, kie, kio, key_ref,
              bias_ref, qs, os, *, tq, nk, topk, n_meta):
    hd = LANES
    _dsa_keys(qi_ref, psq_ref, kie, kio, key_ref, tq=tq, row0=i * tq, nk=nk, n_meta=n_meta)
    bases = lax.fori_loop(0, 32 // SEARCH_UNROLL,
                          lambda t, b: _search_trip(key_ref, t, b, tq=tq, nk=nk, topk=topk), _search_init(tq))
    _dsa_mask(key_ref, bias_ref, 0, bases, tq=tq, nk=nk, nk_all=nk, topk=topk)

    for h in range(A_HEADS):
        qs[h] = q_ref[:, h * hd:(h + 1) * hd]

    def trip(t, carry):
        _attn_heads(qs, kcat, vcat, bias_ref, 0, os, t, nk=nk)
        return carry

    lax.fori_loop(0, A_HEADS // ATT_HEADS_PER_TRIP, trip, 0)
    for h in range(A_HEADS):
        o_ref[:, h * hd:(h + 1) * hd] = os[h]


def _dsa_kernel(q_ref, qi_ref, psq_ref, k_ref, v_ref, psk_ref, km_ref, vm_ref, psm_ref, o_ref,
                kcat, vcat, kie, kio, key_ref, bias_ref, qs, os, *, tq, S, topk, n_meta):
    i = pl.program_id(1)
    NK = LANES + S
    hd = LANES
    G = A_KV_HEADS
    nq = S // tq

    @pl.when(i == 0)
    def _():
        lane = lax.broadcasted_iota(jnp.int32, (NK, hd), 1)
        ones_col = jnp.where(lane == 0, 1.0, 0.0).astype(BF16)
        for g in range(G):
            sl = slice(g * hd, (g + 1) * hd)
            kcat[g, 0:LANES, :] = km_ref[:, sl]
            kcat[g, LANES:, :] = k_ref[:, sl]
            vcat[g, 0:LANES, 0:hd] = vm_ref[:, sl]
            vcat[g, LANES:, 0:hd] = v_ref[:, sl]
            vcat[g, :, hd:] = ones_col
        for dst0, dst1, src in ((0, LANES, psm_ref), (LANES, NK, psk_ref)):
            a = src[...]
            lane = lax.broadcasted_iota(jnp.int32, a.shape, 1)
            even = jnp.where(lane < IDX_DIM, a, 0.0)
            kie[dst0:dst1, :] = even.astype(BF16)
            kio[dst0:dst1, :] = pltpu.roll(even, IDX_DIM, 1).astype(BF16)

    bounds = sorted({-(-nq * (v + 1) // DSA_VARIANTS) for v in range(DSA_VARIANTS)})
    lo = 0
    for hi in bounds:
        pl.when((i >= lo) & (i < hi))(functools.partial(
            _dsa_step, i, q_ref, qi_ref, psq_ref, o_ref, kcat, vcat, kie, kio,
            key_ref, bias_ref, qs, os, tq=tq, nk=LANES + hi * tq, topk=topk, n_meta=n_meta))
        lo = hi


def _dsa(p, ps, pm, psm, B, S, tq, topk):
    M, N = p.shape
    D = N // 2
    kv_w = D // 4
    nq = S // tq
    NK = LANES + S
    qi_col = (D + 2 * kv_w) // 1024
    kern = functools.partial(_dsa_kernel, tq=tq, S=S, topk=topk, n_meta=N_META)
    return pl.pallas_call(
        kern,
        grid=(B, nq),
        in_specs=[
            pl.BlockSpec((tq, D), lambda b, i: (b * nq + i, 0)),
            pl.BlockSpec((tq, 1024), lambda b, i: (b * nq + i, qi_col)),
            pl.BlockSpec((tq, LANES), lambda b, i: (b * nq + i, 0)),
            pl.BlockSpec((S, kv_w), lambda b, i: (b, D // kv_w)),
            pl.BlockSpec((S, kv_w), lambda b, i: (b, D // kv_w + 1)),
            pl.BlockSpec((S, LANES), lambda b, i: (b, 0)),
            pl.BlockSpec((LANES, kv_w), lambda b, i: (0, D // kv_w)),
            pl.BlockSpec((LANES, kv_w), lambda b, i: (0, D // kv_w + 1)),
            pl.BlockSpec((LANES, LANES), lambda b, i: (0, 0)),
        ],
        out_specs=pl.BlockSpec((tq, D), lambda b, i: (b * nq + i, 0)),
        out_shape=jax.ShapeDtypeStruct((M, D), BF16),
        scratch_shapes=[
            pltpu.VMEM((A_KV_HEADS, NK, LANES), BF16),
            pltpu.VMEM((A_KV_HEADS, NK, 2 * LANES), BF16),
            pltpu.VMEM((NK, LANES), BF16),
            pltpu.VMEM((NK, LANES), BF16),
            pltpu.VMEM((tq, NK), jnp.int32),
            pltpu.VMEM((1, tq, NK), F32),
            pltpu.VMEM((A_HEADS, tq, LANES), BF16),
            pltpu.VMEM((A_HEADS, tq, LANES), BF16),
        ],
        compiler_params=_params("parallel", "arbitrary"),
        name="dsa_attention",
    )(p, p, ps, p, p, ps, pm, pm, psm)


def _row_tile(m, pref):
    return pref if m % pref == 0 else m


def kernel(x, meta_tokens, ffn1_norm, ffn1_w_gate, ffn1_w_up, ffn1_w_down, mix_norm, ffn2_norm, ffn2_w_gate, ffn2_w_up, ffn2_w_down, mlstm_w_in, mlstm_b_i, mlstm_b_f, mlstm_head_norm, mlstm_w_out, dsa_w_in, dsa_q_norm, dsa_k_norm, dsa_w_out):
    B, S, D = x.shape
    depth = ffn1_norm.shape[0]
    assert depth == 2 and meta_tokens.shape == (N_META, D)
    F = ffn1_w_gate.shape[-1]
    H, DK, DV = M_HEADS, D // (2 * M_HEADS), D // M_HEADS
    hd = D // A_HEADS
    L = 256
    TM = 1024
    TMO = 512
    TF = 512
    assert S % L == 0 and S % TM == 0 and (B * S) % TMO == 0 and F % TF == 0

    hx = x.reshape(B * S, D)
    hm = meta_tokens.astype(x.dtype)
    bf = lambda w: w.astype(BF16)

    def ffn(h, g, wg, wu, wd):
        return _ffn(h, g, wg, wu, wd, _row_tile(h.shape[0], TM), TF)

    def pad_cols(w):
        return jnp.pad(w, ((0, 0), (0, LANES - w.shape[1])))

    def pad_rows(a, n, front=False):
        r = n - a.shape[0]
        return jnp.pad(a, ((r, 0) if front else (0, r), (0, 0)))

    w = [bf(ffn1_w_gate[0]), bf(ffn1_w_up[0]), bf(ffn1_w_down[0])]
    hx = ffn(hx, ffn1_norm[0], *w)
    hm = ffn(hm, ffn1_norm[0], *w)

    n_wide = 2 * H * DK + 2 * H * DV
    w_wide = bf(mlstm_w_in[0][:, :n_wide])
    w_gate = bf(pad_cols(mlstm_w_in[0][:, n_wide:]))
    px, psx = _inproj_m(hx, mix_norm[0], w_wide, w_gate, TM, 1024)
    pm, psm = _inproj_m(hm, mix_norm[0], w_wide, w_gate, N_META, 1024)

    bias = pad_cols(jnp.concatenate([mlstm_b_i[0], mlstm_b_f[0]]).astype(F32)[None, :])
    gh = mlstm_head_norm[0].astype(F32).reshape(1, H * DV)
    zc = jnp.zeros((H, DK, DV), F32)
    zn = jnp.zeros((H, 1, DK), F32)
    zm = jnp.zeros((H, 1, LANES), F32)
    ym, c0, n0, m0 = _mlstm(pad_rows(pm, L, True), pad_rows(psm, L, True), bias, gh, zc, zn, zm,
                            1, L, L - N_META)
    yx, _, _, _ = _mlstm(px, psx, bias, gh, c0[0], n0[0], m0[0], B, L, 0)
    w_out = bf(mlstm_w_out[0])
    hx = _outproj(yx, w_out, hx, TMO)
    hm = _outproj(ym[L - N_META:], w_out, hm, N_META)

    w = [bf(ffn2_w_gate[0]), bf(ffn2_w_up[0]), bf(ffn2_w_down[0])]
    hx = ffn(hx, ffn2_norm[0], *w)
    hm = ffn(hm, ffn2_norm[0], *w)

    w = [bf(ffn1_w_gate[1]), bf(ffn1_w_up[1]), bf(ffn1_w_down[1])]
    hx = ffn(hx, ffn1_norm[1], *w)
    hm = ffn(hm, ffn1_norm[1], *w)

    n_wide = A_HEADS * hd + 2 * A_KV_HEADS * hd + IDX_HEADS * IDX_DIM
    w_wide = bf(dsa_w_in[0][:, :n_wide])
    w_idx = bf(pad_cols(dsa_w_in[0][:, n_wide:]))
    gq = (dsa_q_norm[0].astype(F32) * (hd ** -0.5 * float(np.log2(np.e))))[None, :]
    gk = dsa_k_norm[0].astype(F32)[None, :]
    wi_scale = IDX_HEADS ** -0.5 * IDX_DIM ** -0.5
    pos = jnp.arange(N_META + S, dtype=F32)
    t128 = _rope_table(pos, hd)
    t64 = _rope_table(pos, IDX_DIM)
    tki = _rope_table(pos, IDX_DIM, (IDX_DIM, IDX_DIM + IDX_HEADS), wi_scale, reps=1)
    tabs_x = [t[N_META:] for t in (t128, t64, tki)]
    tabs_m = [t[:N_META] for t in (t128, t64, tki)]
    px, psx = _inproj_a(hx, mix_norm[1], w_wide, w_idx, gq, gk, *tabs_x, TM)
    pm, psm = _inproj_a(hm, mix_norm[1], w_wide, w_idx, gq, gk, *tabs_m, N_META)

    topk = min(TOPK_MAX, (N_META + S - N_META) // 4)
    ox = _dsa(px, psx, pad_rows(pm, LANES), pad_rows(psm, LANES), B, S, 256, topk)
    hx = _outproj(ox, bf(dsa_w_out[0]), hx, TMO)

    w = [bf(ffn2_w_gate[1]), bf(ffn2_w_up[1]), bf(ffn2_w_down[1])]
    hx = ffn(hx, ffn2_norm[1], *w)
    return hx.reshape(B, S, D)
```

```python
import functools

import numpy as np
import jax
import jax.numpy as jnp
from jax import lax
from jax.experimental import pallas as pl
from jax.experimental.pallas import tpu as pltpu

F32 = jnp.float32
BF16 = jnp.bfloat16

N_META = 16
FFN_HALF = 0.5
NORM_EPS = 1e-6
ROPE_THETA = 500000.0
ROPE_FRAC = 4
M_HEADS = 4
A_HEADS = 16
A_KV_HEADS = 4
IDX_HEADS = 16
IDX_DIM = 64
TOPK_MAX = 256
LOG_I_PAD = -1e30

LANES = 128
VMEM_LIMIT = 60 * 1024 * 1024
DSA_VARIANTS = 3
SEARCH_ROW_BLOCKS = 2
SEARCH_UNROLL = 4
ATT_HEADS_PER_TRIP = 4
INT_MIN = int(np.iinfo(np.int32).min)
NEG_INF_KEY = int(np.array(-np.inf, np.float32).view(np.int32) ^ np.int32(0x7FFFFFFF))


def _params(*sem):
    return pltpu.CompilerParams(dimension_semantics=sem, vmem_limit_bytes=VMEM_LIMIT)


def _row_block(tm, d):
    return pl.BlockSpec((tm, d), lambda i, j: (i, 0))


def _rms(x, g):
    return x * lax.rsqrt(jnp.mean(x * x, axis=-1, keepdims=True) + NORM_EPS) * g


def _dot(a, b):
    return jnp.dot(a, b, preferred_element_type=F32)


def _dot_nt(a, b):
    return lax.dot_general(a, b, (((1,), (1,)), ((), ())), preferred_element_type=F32)


def _ffn_kernel(h_ref, g_ref, wg_ref, wu_ref, wd_ref, o_ref, xn_ref):
    @pl.when(pl.program_id(1) == 0)
    def _():
        h = h_ref[...]
        xn_ref[...] = _rms(h, g_ref[...]).astype(BF16)
        o_ref[...] = h

    xn = xn_ref[...]
    gate = _dot(xn, wg_ref[...])
    up = _dot(xn, wu_ref[...])
    act = (gate * jax.nn.sigmoid(gate)) * (up * FFN_HALF)
    o_ref[...] += _dot(act.astype(BF16), wd_ref[...])


def _cast_kernel(x_ref, o_ref):
    o_ref[...] = x_ref[...].astype(o_ref.dtype)


def _to_bf16(w, br):
    nl, R, C = w.shape
    spec = pl.BlockSpec((None, br, C), lambda l, r: (l, r, 0))
    return pl.pallas_call(
        _cast_kernel,
        grid=(nl, R // br),
        in_specs=[spec],
        out_specs=spec,
        out_shape=jax.ShapeDtypeStruct(w.shape, BF16),
        compiler_params=_params("parallel", "parallel"),
        name="cast_bf16",
    )(w)


def _ffn(h, g, wg, wu, wd, layer, tm, tf):
    M, D = h.shape
    F = wg.shape[2]
    return pl.pallas_call(
        _ffn_kernel,
        grid=(M // tm, F // tf),
        in_specs=[
            _row_block(tm, D),
            pl.BlockSpec((1, D), lambda i, j: (0, 0)),
            pl.BlockSpec((None, D, tf), lambda i, j: (layer, 0, j)),
            pl.BlockSpec((None, D, tf), lambda i, j: (layer, 0, j)),
            pl.BlockSpec((None, tf, D), lambda i, j: (layer, j, 0)),
        ],
        out_specs=pl.BlockSpec((tm, D), lambda i, j: (i, 0)),
        out_shape=jax.ShapeDtypeStruct((M, D), F32),
        scratch_shapes=[pltpu.VMEM((tm, D), BF16)],
        compiler_params=_params("parallel", "arbitrary"),
        name="ffn",
    )(h, g.reshape(1, D), wg, wu, wd)


def _inproj_m_kernel(h_ref, g_ref, w_ref, ws_ref, p_ref, ps_ref, xn_ref):
    @pl.when(pl.program_id(1) == 0)
    def _():
        xn = _rms(h_ref[...], g_ref[...]).astype(BF16)
        xn_ref[...] = xn
        ps_ref[...] = _dot(xn, ws_ref[...])

    p_ref[...] = _dot(xn_ref[...], w_ref[...]).astype(BF16)


def _inproj_m(h, g, w, ws, tm, tn):
    M, D = h.shape
    N = w.shape[1]
    return pl.pallas_call(
        _inproj_m_kernel,
        grid=(M // tm, N // tn),
        in_specs=[
            _row_block(tm, D),
            pl.BlockSpec((1, D), lambda i, j: (0, 0)),
            pl.BlockSpec((D, tn), lambda i, j: (0, j)),
            pl.BlockSpec((D, LANES), lambda i, j: (0, 0)),
        ],
        out_specs=[
            pl.BlockSpec((tm, tn), lambda i, j: (i, j)),
            pl.BlockSpec((tm, LANES), lambda i, j: (i, 0)),
        ],
        out_shape=[jax.ShapeDtypeStruct((M, N), BF16), jax.ShapeDtypeStruct((M, LANES), F32)],
        scratch_shapes=[pltpu.VMEM((tm, D), BF16)],
        compiler_params=_params("parallel", "arbitrary"),
        name="mlstm_inproj",
    )(h, g.reshape(1, D), w, ws)


def _log_sigmoid(x):
    return jnp.minimum(x, 0.0) - jnp.log1p(jnp.exp(-jnp.abs(x)))


def _mlstm_kernel(q_ref, k_ref, v_ref, o_ref, ps_ref, bias_ref, gh_ref, c0_ref, n0_ref, m0_ref,
                  y_ref, c_ref, n_ref, m_ref, *, L, npad, H, DK, DV):
    @pl.when(pl.program_id(1) == 0)
    def _():
        c_ref[...] = c0_ref[...]
        n_ref[...] = n0_ref[...]
        m_ref[...] = m0_ref[...]

    gates = ps_ref[...] + bias_ref[...]
    logf = _log_sigmoid(gates)
    if npad:
        valid = lax.broadcasted_iota(jnp.int32, (L, 1), 0) >= npad
        logf = jnp.where(valid, logf, 0.0)
        gates = jnp.where(valid, gates, LOG_I_PAD)
    ii = lax.broadcasted_iota(jnp.int32, (L, L), 0)
    jj = lax.broadcasted_iota(jnp.int32, (L, L), 1)
    causal = jj <= ii
    cum = jnp.dot(causal.astype(F32), logf, preferred_element_type=F32,
                  precision=lax.Precision.HIGHEST)
    cum_t = cum.T
    gates_t = gates.T

    for h in range(H):
        b_col = cum[:, H + h:H + h + 1]
        b_row = cum_t[H + h:H + h + 1, :]
        li_col = gates[:, h:h + 1]
        li_row = gates_t[h:h + 1, :]
        m_prev = m_ref[h][:, 0:1]
        q = q_ref[:, h * DK:(h + 1) * DK] * (DK ** -0.5)
        k = k_ref[:, h * DK:(h + 1) * DK]
        v = v_ref[:, h * DV:(h + 1) * DV]
        c_old = c_ref[h]
        n_old = n_ref[h]

        dm = jnp.where(causal, b_col - b_row + li_row, -jnp.inf)
        inter = b_col + m_prev
        m_t = jnp.maximum(inter, jnp.max(dm, axis=-1, keepdims=True))
        s = _dot_nt(q, k) * jnp.exp(dm - m_t)
        w_inter = jnp.exp(inter - m_t)
        num = w_inter * _dot(q, c_old.astype(BF16)) + _dot(s.astype(BF16), v)
        qn = jnp.sum(q.astype(F32) * n_old, axis=-1, keepdims=True)
        den = w_inter * qn + jnp.sum(s, axis=-1, keepdims=True)
        hout = num / jnp.maximum(jnp.abs(den), jnp.exp(-m_t))

        b_last = b_col[L - 1:L, :]
        g_row = b_last - b_row + li_row
        g_col = b_last - b_col + li_col
        m_new = jnp.maximum(b_last + m_prev, jnp.max(g_row, axis=-1, keepdims=True))
        decay = jnp.exp(b_last + m_prev - m_new)
        kw = k.astype(F32) * jnp.exp(g_col - m_new)
        c_ref[h] = decay * c_old + _dot(kw.T.astype(BF16), v)
        n_ref[h] = decay * n_old + jnp.sum(kw, axis=0, keepdims=True)
        m_ref[h] = jnp.broadcast_to(m_new, (1, LANES))

        hn = _rms(hout, gh_ref[:, h * DV:(h + 1) * DV])
        og = jax.nn.sigmoid(o_ref[:, h * DV:(h + 1) * DV].astype(F32))
        y_ref[:, h * DV:(h + 1) * DV] = (og * hn).astype(BF16)


def _mlstm(p, ps, bias, gh, c0, n0, m0, B, L, npad):
    M = p.shape[0]
    H = M_HEADS
    DK, DV = c0.shape[1], c0.shape[2]
    NC = M // (B * L)
    row = lambda b, c: b * NC + c
    qk_w, vo_w = H * DK, H * DV
    assert vo_w == 2 * qk_w
    kern = functools.partial(_mlstm_kernel, L=L, npad=npad, H=H, DK=DK, DV=DV)
    return pl.pallas_call(
        kern,
        grid=(B, NC),
        in_specs=[
            pl.BlockSpec((L, qk_w), lambda b, c: (row(b, c), 0)),
            pl.BlockSpec((L, qk_w), lambda b, c: (row(b, c), 1)),
            pl.BlockSpec((L, vo_w), lambda b, c: (row(b, c), 1)),
            pl.BlockSpec((L, vo_w), lambda b, c: (row(b, c), 2)),
            pl.BlockSpec((L, LANES), lambda b, c: (row(b, c), 0)),
            pl.BlockSpec((1, LANES), lambda b, c: (0, 0)),
            pl.BlockSpec((1, vo_w), lambda b, c: (0, 0)),
            pl.BlockSpec((H, DK, DV), lambda b, c: (0, 0, 0)),
            pl.BlockSpec((H, 1, DK), lambda b, c: (0, 0, 0)),
            pl.BlockSpec((H, 1, LANES), lambda b, c: (0, 0, 0)),
        ],
        out_specs=[
            pl.BlockSpec((L, vo_w), lambda b, c: (row(b, c), 0)),
            pl.BlockSpec((None, H, DK, DV), lambda b, c: (b, 0, 0, 0)),
            pl.BlockSpec((None, H, 1, DK), lambda b, c: (b, 0, 0, 0)),
            pl.BlockSpec((None, H, 1, LANES), lambda b, c: (b, 0, 0, 0)),
        ],
        out_shape=[
            jax.ShapeDtypeStruct((M, vo_w), BF16),
            jax.ShapeDtypeStruct((B, H, DK, DV), F32),
            jax.ShapeDtypeStruct((B, H, 1, DK), F32),
            jax.ShapeDtypeStruct((B, H, 1, LANES), F32),
        ],
        compiler_params=_params("parallel", "arbitrary"),
        name="mlstm",
    )(p, p, p, p, ps, bias, gh, c0, n0, m0)


def _outproj_kernel(y_ref, w_ref, h_ref, o_ref):
    o_ref[...] = h_ref[...] + _dot(y_ref[...], w_ref[...])


def _outproj(y, w, h, tm):
    M, D = h.shape
    K = y.shape[1]
    return pl.pallas_call(
        _outproj_kernel,
        grid=(M // tm,),
        in_specs=[
            pl.BlockSpec((tm, K), lambda i: (i, 0)),
            pl.BlockSpec((K, D), lambda i: (0, 0)),
            pl.BlockSpec((tm, D), lambda i: (i, 0)),
        ],
        out_specs=pl.BlockSpec((tm, D), lambda i: (i, 0)),
        out_shape=jax.ShapeDtypeStruct((M, D), F32),
        compiler_params=_params("parallel"),
        name="outproj",
    )(y, w, h)


def _rope(x, tab, half):
    c = tab[:, 0:LANES]
    s1 = tab[:, LANES:2 * LANES]
    s2 = tab[:, 2 * LANES:3 * LANES]
    return x * c + pltpu.roll(x, LANES - half, 1) * s1 + pltpu.roll(x, half, 1) * s2


def _inproj_a_kernel(h_ref, g_ref, w_ref, ws_ref, gq_ref, gk_ref, t128_ref, t64_ref, tki_ref,
                     p_ref, ps_ref, xn_ref, acc_ref, *, nq_tiles, tn, half128, half64):
    j = pl.program_id(1)

    @pl.when(j == 0)
    def _():
        xn = _rms(h_ref[...], g_ref[...]).astype(BF16)
        xn_ref[...] = xn
        ps_ref[...] = _rope(_dot(xn, ws_ref[...]), tki_ref[...], half64)

    nh = tn // LANES
    n_tiles = nq_tiles + 2

    def epilogue(tile, acc):
        if tile < nq_tiles:
            t = t128_ref[...]
            for c in range(nh):
                xs = _rms(acc[:, c * LANES:(c + 1) * LANES], gq_ref[...])
                p_ref[:, c * LANES:(c + 1) * LANES] = _rope(xs, t, half128).astype(BF16)
        elif tile == nq_tiles:
            t = t128_ref[...]
            for c in range(nh // 2):
                xs = _rms(acc[:, c * LANES:(c + 1) * LANES], gk_ref[...])
                p_ref[:, c * LANES:(c + 1) * LANES] = _rope(xs, t, half128).astype(BF16)
            p_ref[:, tn // 2:] = acc[:, tn // 2:].astype(BF16)
        else:
            t = t64_ref[...]
            for c in range(nh):
                p_ref[:, c * LANES:(c + 1) * LANES] = _rope(acc[:, c * LANES:(c + 1) * LANES], t, half64).astype(BF16)

    for s in range(n_tiles + 1):
        @pl.when(j == s)
        def _(s=s):
            if s < n_tiles:
                acc_ref[s % 2] = _dot(xn_ref[...], w_ref[...])
            if s >= 1:
                epilogue(s - 1, acc_ref[(s - 1) % 2])


def _inproj_a(h, g, w, ws, gq, gk, t128, t64, tki, tm):
    M, D = h.shape
    N = w.shape[1]
    tn = 1024
    hd = D // A_HEADS
    assert hd == LANES and D // 4 * 2 == tn and N == D + 2 * tn
    nt = t128.shape[0] // tm
    n_tiles = N // tn
    kern = functools.partial(_inproj_a_kernel, nq_tiles=D // tn, tn=tn,
                             half128=hd // ROPE_FRAC // 2, half64=IDX_DIM // ROPE_FRAC // 2)
    tab = pl.BlockSpec((tm, 3 * LANES), lambda i, j: (i % nt, 0))
    return pl.pallas_call(
        kern,
        grid=(M // tm, n_tiles + 1),
        in_specs=[
            _row_block(tm, D),
            pl.BlockSpec((1, D), lambda i, j: (0, 0)),
            pl.BlockSpec((D, tn), lambda i, j: (0, jnp.minimum(j, n_tiles - 1))),
            pl.BlockSpec((D, LANES), lambda i, j: (0, 0)),
            pl.BlockSpec((1, LANES), lambda i, j: (0, 0)),
            pl.BlockSpec((1, LANES), lambda i, j: (0, 0)),
            tab, tab, tab,
        ],
        out_specs=[
            pl.BlockSpec((tm, tn), lambda i, j: (i, jnp.maximum(j - 1, 0))),
            pl.BlockSpec((tm, LANES), lambda i, j: (i, 0)),
        ],
        out_shape=[jax.ShapeDtypeStruct((M, N), BF16), jax.ShapeDtypeStruct((M, LANES), F32)],
        scratch_shapes=[pltpu.VMEM((tm, D), BF16), pltpu.VMEM((2, tm, tn), F32)],
        compiler_params=_params("parallel", "arbitrary"),
        name="dsa_inproj",
    )(h, g.reshape(1, D), w, ws, gq, gk, t128, t64, tki)


def _rope_table(pos, d, extra_scale_lanes=None, extra_scale=1.0, reps=None):
    rot = d // ROPE_FRAC
    half = rot // 2
    inv = 1.0 / (ROPE_THETA ** (jnp.arange(0, rot, 2, dtype=F32) / rot))
    ang = pos[:, None] * inv[None, :]
    cos, sin = jnp.cos(ang), jnp.sin(ang)
    T = pos.shape[0]
    one = jnp.ones((T, d - rot), F32)
    zero = lambda n: jnp.zeros((T, n), F32)
    c = jnp.concatenate([cos, cos, one], axis=1)
    s1 = jnp.concatenate([-sin, zero(d - half)], axis=1)
    s2 = jnp.concatenate([zero(half), sin, zero(d - rot)], axis=1)
    if reps is None:
        reps = LANES // d
    c, s1, s2 = (jnp.tile(a, (1, reps)) for a in (c, s1, s2))
    fill = LANES - reps * d
    if fill:
        tail = jnp.ones((T, fill), F32)
        if extra_scale_lanes is not None:
            lo, hi = extra_scale_lanes
            lane = jnp.arange(reps * d, LANES)
            tail = jnp.where((lane >= lo) & (lane < hi), extra_scale, 1.0)[None, :] * tail
        c = jnp.concatenate([c, tail], axis=1)
        s1 = jnp.concatenate([s1, zero(fill)], axis=1)
        s2 = jnp.concatenate([s2, zero(fill)], axis=1)
    return jnp.concatenate([c, s1, s2], axis=1)


def _float_key(x):
    bits = lax.bitcast_convert_type(x, jnp.int32)
    return jnp.where(bits < 0, bits ^ jnp.int32(0x7FFFFFFF), bits)


def _dsa_keys(qi_ref, psq_ref, kie, kio, key_ref, *, tq, row0, nk, n_meta):
    wi = psq_ref[...]
    ke = kie[0:nk, :]
    ko = kio[0:nk, :]
    score = jnp.zeros((tq, nk), F32)
    for p in range(IDX_HEADS // 2):
        qp = qi_ref[:, p * LANES:(p + 1) * LANES]
        c0 = IDX_DIM + 2 * p
        score += jnp.maximum(_dot_nt(qp, ke), 0.0) * wi[:, c0:c0 + 1]
        score += jnp.maximum(_dot_nt(qp, ko), 0.0) * wi[:, c0 + 1:c0 + 2]
    col = lax.broadcasted_iota(jnp.int32, (tq, nk), 1)
    row = lax.broadcasted_iota(jnp.int32, (tq, nk), 0) + row0
    adm = jnp.where(col < LANES, col - n_meta, col - LANES - row - 1) < 0
    score = jnp.where(score == 0.0, 0.0, score)
    key_ref[:, 0:nk] = _float_key(jnp.where(adm, score, -jnp.inf))


def _count_ge(key_ref, cand, r0, rows, nch):
    acc = jnp.zeros((rows, LANES), F32)
    for c in range(nch):
        acc += jnp.where(key_ref[r0:r0 + rows, c * LANES:(c + 1) * LANES] >= cand, 1.0, 0.0)
    return jnp.sum(acc, axis=-1, keepdims=True)


def _search_trip(key_ref, trip, bases, *, tq, nk, topk):
    rb = tq // SEARCH_ROW_BLOCKS
    bases = list(bases)
    for u in range(SEARCH_UNROLL):
        bit = jnp.int32(31) - (trip.astype(jnp.int32) * SEARCH_UNROLL + u)
        step = lax.shift_left(jnp.int32(1), bit)
        for blk in range(SEARCH_ROW_BLOCKS):
            cand = bases[blk] + step
            cnt = _count_ge(key_ref, cand, blk * rb, rb, nk // LANES)
            bases[blk] = jnp.where(cnt >= float(topk), cand, bases[blk])
    return tuple(bases)


def _search_init(tq):
    return tuple(jnp.full((tq // SEARCH_ROW_BLOCKS, 1), INT_MIN, jnp.int32) for _ in range(SEARCH_ROW_BLOCKS))


def _dsa_mask(key_ref, bias_ref, slot, bases, *, tq, nk, nk_all, topk):
    NCH = nk // LANES
    kf = float(topk)
    ones = jnp.ones((LANES, LANES), BF16)
    thr = jnp.maximum(jnp.concatenate(bases, axis=0), NEG_INF_KEY + 1)
    cnt = _count_ge(key_ref, thr, 0, tq, NCH)
    for c in range(NCH):
        sl = slice(c * LANES, (c + 1) * LANES)
        bias_ref[slot, :, sl] = jnp.where(key_ref[:, sl] >= thr, 0.0, -jnp.inf)
    if nk_all > nk:
        bias_ref[slot, :, nk:nk_all] = jnp.full((tq, nk_all - nk), -jnp.inf, F32)

    @pl.when(jnp.max(cnt) > kf)
    def _():
        need = kf - _count_ge(key_ref, thr + 1, 0, tq, NCH)
        rr = lax.broadcasted_iota(jnp.int32, (LANES, LANES), 0)
        cc = lax.broadcasted_iota(jnp.int32, (LANES, LANES), 1)
        tri = jnp.where(rr <= cc, 1.0, 0.0).astype(BF16)
        run = jnp.zeros((tq, LANES), F32)
        for c in range(NCH):
            sl = slice(c * LANES, (c + 1) * LANES)
            kc = key_ref[:, sl]
            eq = kc == thr
            eqb = jnp.where(eq, 1.0, 0.0).astype(BF16)
            rank = run + _dot(eqb, tri)
            keep_eq = jnp.where(rank <= need, 0.0, -jnp.inf)
            bias_ref[slot, :, sl] = jnp.where(kc > thr, 0.0, jnp.where(eq, keep_eq, -jnp.inf))
            run = run + _dot(eqb, ones)


def _attn_heads(qs, kcat, vcat, bias_ref, slot, os, trip, *, nk):
    hd = LANES
    per_group = (A_HEADS // A_KV_HEADS) // ATT_HEADS_PER_TRIP
    g = trip // per_group
    kg = kcat[g, 0:nk, :]
    vg = vcat[g, 0:nk, :]
    for r in range(ATT_HEADS_PER_TRIP):
        h = trip * ATT_HEADS_PER_TRIP + r
        logits = _dot_nt(qs[h], kg) + bias_ref[slot, :, 0:nk]
        mx = jnp.max(logits, axis=-1, keepdims=True)
        pv = _dot(jnp.exp2(logits - mx).astype(BF16), vg)
        os[h] = (pv[:, 0:hd] / pv[:, hd:hd + 1]).astype(BF16)


def _dsa_step(i, q_ref, qi_ref, psq_ref, o_ref, kcat, vcat, kie, kio, key_ref,
              bias_ref, qs, os, *, tq, nk, topk, n_meta):
    hd = LANES
    _dsa_keys(qi_ref, psq_ref, kie, kio, key_ref, tq=tq, row0=i * tq, nk=nk, n_meta=n_meta)
    bases = lax.fori_loop(0, 32 // SEARCH_UNROLL,
                          lambda t, b: _search_trip(key_ref, t, b, tq=tq, nk=nk, topk=topk), _search_init(tq))
    _dsa_mask(key_ref, bias_ref, 0, bases, tq=tq, nk=nk, nk_all=nk, topk=topk)

    for h in range(A_HEADS):
        qs[h] = q_ref[:, h * hd:(h + 1) * hd]

    def trip(t, carry):
        _attn_heads(qs, kcat, vcat, bias_ref, 0, os, t, nk=nk)
        return carry

    lax.fori_loop(0, A_HEADS // ATT_HEADS_PER_TRIP, trip, 0)
    for h in range(A_HEADS):
        o_ref[:, h * hd:(h + 1) * hd] = os[h]


def _dsa_kernel(q_ref, qi_ref, psq_ref, k_ref, v_ref, psk_ref, km_ref, vm_ref, psm_ref, o_ref,
                kcat, vcat, kie, kio, key_ref, bias_ref, qs, os, *, tq, S, topk, n_meta):
    i = pl.program_id(1)
    NK = LANES + S
    hd = LANES
    G = A_KV_HEADS
    nq = S // tq

    @pl.when(i == 0)
    def _():
        lane = lax.broadcasted_iota(jnp.int32, (NK, hd), 1)
        ones_col = jnp.where(lane == 0, 1.0, 0.0).astype(BF16)
        for g in range(G):
            sl = slice(g * hd, (g + 1) * hd)
            kcat[g, 0:LANES, :] = km_ref[:, sl]
            kcat[g, LANES:, :] = k_ref[:, sl]
            vcat[g, 0:LANES, 0:hd] = vm_ref[:, sl]
            vcat[g, LANES:, 0:hd] = v_ref[:, sl]
            vcat[g, :, hd:] = ones_col
        for dst0, dst1, src in ((0, LANES, psm_ref), (LANES, NK, psk_ref)):
            a = src[...]
            lane = lax.broadcasted_iota(jnp.int32, a.shape, 1)
            even = jnp.where(lane < IDX_DIM, a, 0.0)
            kie[dst0:dst1, :] = even.astype(BF16)
            kio[dst0:dst1, :] = pltpu.roll(even, IDX_DIM, 1).astype(BF16)

    bounds = sorted({-(-nq * (v + 1) // DSA_VARIANTS) for v in range(DSA_VARIANTS)})
    lo = 0
    for hi in bounds:
        pl.when((i >= lo) & (i < hi))(functools.partial(
            _dsa_step, i, q_ref, qi_ref, psq_ref, o_ref, kcat, vcat, kie, kio,
            key_ref, bias_ref, qs, os, tq=tq, nk=LANES + hi * tq, topk=topk, n_meta=n_meta))
        lo = hi


def _dsa(p, ps, pm, psm, B, S, tq, topk):
    M, N = p.shape
    D = N // 2
    kv_w = D // 4
    nq = S // tq
    NK = LANES + S
    qi_col = (D + 2 * kv_w) // 1024
    kern = functools.partial(_dsa_kernel, tq=tq, S=S, topk=topk, n_meta=N_META)
    return pl.pallas_call(
        kern,
        grid=(B, nq),
        in_specs=[
            pl.BlockSpec((tq, D), lambda b, i: (b * nq + i, 0)),
            pl.BlockSpec((tq, 1024), lambda b, i: (b * nq + i, qi_col)),
            pl.BlockSpec((tq, LANES), lambda b, i: (b * nq + i, 0)),
            pl.BlockSpec((S, kv_w), lambda b, i: (b, D // kv_w)),
            pl.BlockSpec((S, kv_w), lambda b, i: (b, D // kv_w + 1)),
            pl.BlockSpec((S, LANES), lambda b, i: (b, 0)),
            pl.BlockSpec((LANES, kv_w), lambda b, i: (0, D // kv_w)),
            pl.BlockSpec((LANES, kv_w), lambda b, i: (0, D // kv_w + 1)),
            pl.BlockSpec((LANES, LANES), lambda b, i: (0, 0)),
        ],
        out_specs=pl.BlockSpec((tq, D), lambda b, i: (b * nq + i, 0)),
        out_shape=jax.ShapeDtypeStruct((M, D), BF16),
        scratch_shapes=[
            pltpu.VMEM((A_KV_HEADS, NK, LANES), BF16),
            pltpu.VMEM((A_KV_HEADS, NK, 2 * LANES), BF16),
            pltpu.VMEM((NK, LANES), BF16),
            pltpu.VMEM((NK, LANES), BF16),
            pltpu.VMEM((tq, NK), jnp.int32),
            pltpu.VMEM((1, tq, NK), F32),
            pltpu.VMEM((A_HEADS, tq, LANES), BF16),
            pltpu.VMEM((A_HEADS, tq, LANES), BF16),
        ],
        compiler_params=_params("parallel", "arbitrary"),
        name="dsa_attention",
    )(p, p, ps, p, p, ps, pm, pm, psm)


def _row_tile(m, pref):
    return pref if m % pref == 0 else m


def kernel(x, meta_tokens, ffn1_norm, ffn1_w_gate, ffn1_w_up, ffn1_w_down, mix_norm, ffn2_norm, ffn2_w_gate, ffn2_w_up, ffn2_w_down, mlstm_w_in, mlstm_b_i, mlstm_b_f, mlstm_head_norm, mlstm_w_out, dsa_w_in, dsa_q_norm, dsa_k_norm, dsa_w_out):
    B, S, D = x.shape
    depth = ffn1_norm.shape[0]
    assert depth == 2 and meta_tokens.shape == (N_META, D)
    F = ffn1_w_gate.shape[-1]
    H, DK, DV = M_HEADS, D // (2 * M_HEADS), D // M_HEADS
    hd = D // A_HEADS
    L = 256
    TM = 1024
    TMO = 512
    TF = 512
    CAST_ROWS = 256
    assert S % L == 0 and S % TM == 0 and (B * S) % TMO == 0 and F % TF == 0

    hx = x.reshape(B * S, D)
    hm = meta_tokens.astype(x.dtype)
    bf = lambda w: w.astype(BF16)

    f1 = (_to_bf16(ffn1_w_gate, CAST_ROWS), _to_bf16(ffn1_w_up, CAST_ROWS), _to_bf16(ffn1_w_down, CAST_ROWS))
    f2 = (_to_bf16(ffn2_w_gate, CAST_ROWS), _to_bf16(ffn2_w_up, CAST_ROWS), _to_bf16(ffn2_w_down, CAST_ROWS))

    def ffn(h, g, ws, layer):
        return _ffn(h, g, *ws, layer, _row_tile(h.shape[0], TM), TF)

    def pad_cols(w):
        return jnp.pad(w, ((0, 0), (0, LANES - w.shape[1])))

    def pad_rows(a, n, front=False):
        r = n - a.shape[0]
        return jnp.pad(a, ((r, 0) if front else (0, r), (0, 0)))

    hx = ffn(hx, ffn1_norm[0], f1, 0)
    hm = ffn(hm, ffn1_norm[0], f1, 0)

    n_wide = 2 * H * DK + 2 * H * DV
    w_wide = bf(mlstm_w_in[0][:, :n_wide])
    w_gate = bf(pad_cols(mlstm_w_in[0][:, n_wide:]))
    px, psx = _inproj_m(hx, mix_norm[0], w_wide, w_gate, TM, 1024)
    pm, psm = _inproj_m(hm, mix_norm[0], w_wide, w_gate, N_META, 1024)

    bias = pad_cols(jnp.concatenate([mlstm_b_i[0], mlstm_b_f[0]]).astype(F32)[None, :])
    gh = mlstm_head_norm[0].astype(F32).reshape(1, H * DV)
    zc = jnp.zeros((H, DK, DV), F32)
    zn = jnp.zeros((H, 1, DK), F32)
    zm = jnp.zeros((H, 1, LANES), F32)
    ym, c0, n0, m0 = _mlstm(pad_rows(pm, L, True), pad_rows(psm, L, True), bias, gh, zc, zn, zm,
                            1, L, L - N_META)
    yx, _, _, _ = _mlstm(px, psx, bias, gh, c0[0], n0[0], m0[0], B, L, 0)
    w_out = bf(mlstm_w_out[0])
    hx = _outproj(yx, w_out, hx, TMO)
    hm = _outproj(ym[L - N_META:], w_out, hm, N_META)

    hx = ffn(hx, ffn2_norm[0], f2, 0)
    hm = ffn(hm, ffn2_norm[0], f2, 0)

    hx = ffn(hx, ffn1_norm[1], f1, 1)
    hm = ffn(hm, ffn1_norm[1], f1, 1)

    n_wide = A_HEADS * hd + 2 * A_KV_HEADS * hd + IDX_HEADS * IDX_DIM
    w_wide = bf(dsa_w_in[0][:, :n_wide])
    w_idx = bf(pad_cols(dsa_w_in[0][:, n_wide:]))
    gq = (dsa_q_norm[0].astype(F32) * (hd ** -0.5 * float(np.log2(np.e))))[None, :]
    gk = dsa_k_norm[0].astype(F32)[None, :]
    wi_scale = IDX_HEADS ** -0.5 * IDX_DIM ** -0.5
    pos = jnp.arange(N_META + S, dtype=F32)
    t128 = _rope_table(pos, hd)
    t64 = _rope_table(pos, IDX_DIM)
    tki = _rope_table(pos, IDX_DIM, (IDX_DIM, IDX_DIM + IDX_HEADS), wi_scale, reps=1)
    tabs_x = [t[N_META:] for t in (t128, t64, tki)]
    tabs_m = [t[:N_META] for t in (t128, t64, tki)]
    px, psx = _inproj_a(hx, mix_norm[1], w_wide, w_idx, gq, gk, *tabs_x, TM)
    pm, psm = _inproj_a(hm, mix_norm[1], w_wide, w_idx, gq, gk, *tabs_m, N_META)

    topk = min(TOPK_MAX, (N_META + S - N_META) // 4)
    ox = _dsa(px, psx, pad_rows(pm, LANES), pad_rows(psm, LANES), B, S, 256, topk)
    hx = _outproj(ox, bf(dsa_w_out[0]), hx, TMO)

    hx = ffn(hx, ffn2_norm[1], f2, 1)
    return hx.reshape(B, S, D)
```

```python
import functools

import numpy as np
import jax
import jax.numpy as jnp
from jax import lax
from jax.experimental import pallas as pl
from jax.experimental.pallas import tpu as pltpu

F32 = jnp.float32
BF16 = jnp.bfloat16

N_META = 16
FFN_HALF = 0.5
NORM_EPS = 1e-6
ROPE_THETA = 500000.0
ROPE_FRAC = 4
M_HEADS = 4
A_HEADS = 16
A_KV_HEADS = 4
IDX_HEADS = 16
IDX_DIM = 64
TOPK_MAX = 256
LOG_I_PAD = -1e30

LANES = 128
VMEM_LIMIT = 60 * 1024 * 1024
DSA_VARIANTS = 3
SEARCH_ROW_BLOCKS = 2
SEARCH_UNROLL = 4
ATT_HEADS_PER_TRIP = 4
FFN_SPLIT = 2
INT_MIN = int(np.iinfo(np.int32).min)
NEG_INF_KEY = int(np.array(-np.inf, np.float32).view(np.int32) ^ np.int32(0x7FFFFFFF))


def _params(*sem):
    return pltpu.CompilerParams(dimension_semantics=sem, vmem_limit_bytes=VMEM_LIMIT)


def _row_block(tm, d):
    return pl.BlockSpec((tm, d), lambda i, j: (i, 0))


def _rms(x, g):
    return x * lax.rsqrt(jnp.mean(x * x, axis=-1, keepdims=True) + NORM_EPS) * g


def _dot(a, b):
    return jnp.dot(a, b, preferred_element_type=F32)


def _dot_nt(a, b):
    return lax.dot_general(a, b, (((1,), (1,)), ((), ())), preferred_element_type=F32)


def _swiglu_half(xn, wg_ref, wu_ref, wd_ref):
    w = wg_ref.shape[1] // FFN_SPLIT
    acc = None
    for c in range(FFN_SPLIT):
        sl = slice(c * w, (c + 1) * w)
        gate = _dot(xn, wg_ref[:, sl])
        up = _dot(xn, wu_ref[:, sl])
        act = (gate * jax.nn.sigmoid(gate)) * (up * FFN_HALF)
        part = _dot(act.astype(BF16), wd_ref[sl, :])
        acc = part if acc is None else acc + part
    return acc


def _ffn_kernel(h_ref, g_ref, wg_ref, wu_ref, wd_ref, o_ref, xn_ref):
    j = pl.program_id(1)

    @pl.when(j == 0)
    def _():
        h = h_ref[...]
        xn = _rms(h, g_ref[...]).astype(BF16)
        xn_ref[...] = xn
        o_ref[...] = h + _swiglu_half(xn, wg_ref, wu_ref, wd_ref)

    @pl.when(j > 0)
    def _():
        o_ref[...] += _swiglu_half(xn_ref[...], wg_ref, wu_ref, wd_ref)


def _cast_kernel(x_ref, o_ref):
    o_ref[...] = x_ref[...].astype(o_ref.dtype)


def _to_bf16(w, br):
    nl, R, C = w.shape
    spec = pl.BlockSpec((None, br, C), lambda l, r: (l, r, 0))
    return pl.pallas_call(
        _cast_kernel,
        grid=(nl, R // br),
        in_specs=[spec],
        out_specs=spec,
        out_shape=jax.ShapeDtypeStruct(w.shape, BF16),
        compiler_params=_params("parallel", "parallel"),
        name="cast_bf16",
    )(w)


def _ffn(h, g, wg, wu, wd, layer, tm, tf):
    M, D = h.shape
    F = wg.shape[2]
    return pl.pallas_call(
        _ffn_kernel,
        grid=(M // tm, F // tf),
        in_specs=[
            _row_block(tm, D),
            pl.BlockSpec((1, D), lambda i, j: (0, 0)),
            pl.BlockSpec((None, D, tf), lambda i, j: (layer, 0, j)),
            pl.BlockSpec((None, D, tf), lambda i, j: (layer, 0, j)),
            pl.BlockSpec((None, tf, D), lambda i, j: (layer, j, 0)),
        ],
        out_specs=pl.BlockSpec((tm, D), lambda i, j: (i, 0)),
        out_shape=jax.ShapeDtypeStruct((M, D), F32),
        scratch_shapes=[pltpu.VMEM((tm, D), BF16)],
        compiler_params=_params("parallel", "arbitrary"),
        name="ffn",
    )(h, g.reshape(1, D), wg, wu, wd)


def _inproj_m_kernel(h_ref, g_ref, w_ref, ws_ref, p_ref, ps_ref, xn_ref):
    j = pl.program_id(1)

    @pl.when(j == 0)
    def _():
        xn = _rms(h_ref[...], g_ref[...]).astype(BF16)
        xn_ref[...] = xn
        ps_ref[...] = _dot(xn, ws_ref[...])
        p_ref[...] = _dot(xn, w_ref[...]).astype(BF16)

    @pl.when(j > 0)
    def _():
        p_ref[...] = _dot(xn_ref[...], w_ref[...]).astype(BF16)


def _inproj_m(h, g, w, ws, tm, tn):
    M, D = h.shape
    N = w.shape[1]
    return pl.pallas_call(
        _inproj_m_kernel,
        grid=(M // tm, N // tn),
        in_specs=[
            _row_block(tm, D),
            pl.BlockSpec((1, D), lambda i, j: (0, 0)),
            pl.BlockSpec((D, tn), lambda i, j: (0, j)),
            pl.BlockSpec((D, LANES), lambda i, j: (0, 0)),
        ],
        out_specs=[
            pl.BlockSpec((tm, tn), lambda i, j: (i, j)),
            pl.BlockSpec((tm, LANES), lambda i, j: (i, 0)),
        ],
        out_shape=[jax.ShapeDtypeStruct((M, N), BF16), jax.ShapeDtypeStruct((M, LANES), F32)],
        scratch_shapes=[pltpu.VMEM((tm, D), BF16)],
        compiler_params=_params("parallel", "arbitrary"),
        name="mlstm_inproj",
    )(h, g.reshape(1, D), w, ws)


def _log_sigmoid(x):
    return jnp.minimum(x, 0.0) - jnp.log1p(jnp.exp(-jnp.abs(x)))


def _mlstm_kernel(q_ref, k_ref, v_ref, o_ref, ps_ref, bias_ref, gh_ref, c0_ref, n0_ref, m0_ref,
                  y_ref, c_ref, n_ref, m_ref, *, L, npad, H, DK, DV):
    @pl.when(pl.program_id(1) == 0)
    def _():
        c_ref[...] = c0_ref[...]
        n_ref[...] = n0_ref[...]
        m_ref[...] = m0_ref[...]

    gates = ps_ref[...] + bias_ref[...]
    logf = _log_sigmoid(gates)
    if npad:
        valid = lax.broadcasted_iota(jnp.int32, (L, 1), 0) >= npad
        logf = jnp.where(valid, logf, 0.0)
        gates = jnp.where(valid, gates, LOG_I_PAD)
    ii = lax.broadcasted_iota(jnp.int32, (L, L), 0)
    jj = lax.broadcasted_iota(jnp.int32, (L, L), 1)
    causal = jj <= ii
    cum = jnp.dot(causal.astype(F32), logf, preferred_element_type=F32,
                  precision=lax.Precision.HIGHEST)
    cum_t = cum.T
    gates_t = gates.T

    updates = []
    for h in range(H):
        b_col = cum[:, H + h:H + h + 1]
        b_row = cum_t[H + h:H + h + 1, :]
        li_col = gates[:, h:h + 1]
        li_row = gates_t[h:h + 1, :]
        m_prev = m_ref[h][:, 0:1]
        q = q_ref[:, h * DK:(h + 1) * DK] * (DK ** -0.5)
        k = k_ref[:, h * DK:(h + 1) * DK]
        v = v_ref[:, h * DV:(h + 1) * DV]
        c_old = c_ref[h]
        n_old = n_ref[h]

        dm = jnp.where(causal, b_col - b_row + li_row, -jnp.inf)
        inter = b_col + m_prev
        m_t = jnp.maximum(inter, jnp.max(dm, axis=-1, keepdims=True))
        decay_mat = jnp.exp(dm - m_t)
        w_inter = jnp.exp(inter - m_t)
        b_last = b_col[L - 1:L, :]
        g_row = b_last - b_row + li_row
        g_col = b_last - b_col + li_col
        m_new = jnp.maximum(b_last + m_prev, jnp.max(g_row, axis=-1, keepdims=True))
        decay = jnp.exp(b_last + m_prev - m_new)
        updates.append((m_new, decay, jnp.exp(g_col - m_new)))

        s = _dot_nt(q, k) * decay_mat
        num = w_inter * _dot(q, c_old.astype(BF16)) + _dot(s.astype(BF16), v)
        qn = jnp.sum(q.astype(F32) * n_old, axis=-1, keepdims=True)
        den = w_inter * qn + jnp.sum(s, axis=-1, keepdims=True)
        hout = num / jnp.maximum(jnp.abs(den), jnp.exp(-m_t))

        hn = _rms(hout, gh_ref[:, h * DV:(h + 1) * DV])
        og = jax.nn.sigmoid(o_ref[:, h * DV:(h + 1) * DV].astype(F32))
        y_ref[:, h * DV:(h + 1) * DV] = (og * hn).astype(BF16)

    for h, (m_new, decay, wk) in enumerate(updates):
        k = k_ref[:, h * DK:(h + 1) * DK]
        v = v_ref[:, h * DV:(h + 1) * DV]
        kw = k.astype(F32) * wk
        c_ref[h] = decay * c_ref[h] + _dot(kw.T.astype(BF16), v)
        n_ref[h] = decay * n_ref[h] + jnp.sum(kw, axis=0, keepdims=True)
        m_ref[h] = jnp.broadcast_to(m_new, (1, LANES))


def _mlstm(p, ps, bias, gh, c0, n0, m0, B, L, npad):
    M = p.shape[0]
    H = M_HEADS
    DK, DV = c0.shape[1], c0.shape[2]
    NC = M // (B * L)
    row = lambda b, c: b * NC + c
    qk_w, vo_w = H * DK, H * DV
    assert vo_w == 2 * qk_w
    kern = functools.partial(_mlstm_kernel, L=L, npad=npad, H=H, DK=DK, DV=DV)
    return pl.pallas_call(
        kern,
        grid=(B, NC),
        in_specs=[
            pl.BlockSpec((L, qk_w), lambda b, c: (row(b, c), 0)),
            pl.BlockSpec((L, qk_w), lambda b, c: (row(b, c), 1)),
            pl.BlockSpec((L, vo_w), lambda b, c: (row(b, c), 1)),
            pl.BlockSpec((L, vo_w), lambda b, c: (row(b, c), 2)),
            pl.BlockSpec((L, LANES), lambda b, c: (row(b, c), 0)),
            pl.BlockSpec((1, LANES), lambda b, c: (0, 0)),
            pl.BlockSpec((1, vo_w), lambda b, c: (0, 0)),
            pl.BlockSpec((H, DK, DV), lambda b, c: (0, 0, 0)),
            pl.BlockSpec((H, 1, DK), lambda b, c: (0, 0, 0)),
            pl.BlockSpec((H, 1, LANES), lambda b, c: (0, 0, 0)),
        ],
        out_specs=[
            pl.BlockSpec((L, vo_w), lambda b, c: (row(b, c), 0)),
            pl.BlockSpec((None, H, DK, DV), lambda b, c: (b, 0, 0, 0)),
            pl.BlockSpec((None, H, 1, DK), lambda b, c: (b, 0, 0, 0)),
            pl.BlockSpec((None, H, 1, LANES), lambda b, c: (b, 0, 0, 0)),
        ],
        out_shape=[
            jax.ShapeDtypeStruct((M, vo_w), BF16),
            jax.ShapeDtypeStruct((B, H, DK, DV), F32),
            jax.ShapeDtypeStruct((B, H, 1, DK), F32),
            jax.ShapeDtypeStruct((B, H, 1, LANES), F32),
        ],
        compiler_params=_params("parallel", "arbitrary"),
        name="mlstm",
    )(p, p, p, p, ps, bias, gh, c0, n0, m0)


def _outproj_kernel(y_ref, w_ref, h_ref, o_ref):
    o_ref[...] = h_ref[...] + _dot(y_ref[...], w_ref[...])


def _outproj(y, w, h, tm):
    M, D = h.shape
    K = y.shape[1]
    return pl.pallas_call(
        _outproj_kernel,
        grid=(M // tm,),
        in_specs=[
            pl.BlockSpec((tm, K), lambda i: (i, 0)),
            pl.BlockSpec((K, D), lambda i: (0, 0)),
            pl.BlockSpec((tm, D), lambda i: (i, 0)),
        ],
        out_specs=pl.BlockSpec((tm, D), lambda i: (i, 0)),
        out_shape=jax.ShapeDtypeStruct((M, D), F32),
        compiler_params=_params("parallel"),
        name="outproj",
    )(y, w, h)


def _rope(x, tab, half):
    c = tab[:, 0:LANES]
    s1 = tab[:, LANES:2 * LANES]
    s2 = tab[:, 2 * LANES:3 * LANES]
    return x * c + pltpu.roll(x, LANES - half, 1) * s1 + pltpu.roll(x, half, 1) * s2


def _inproj_a_kernel(h_ref, g_ref, w_ref, ws_ref, gq_ref, gk_ref, t128_ref, t64_ref, tki_ref,
                     p_ref, ps_ref, xn_ref, acc_ref, *, nq_tiles, tn, half128, half64):
    j = pl.program_id(1)
    nh = tn // LANES
    n_tiles = nq_tiles + 2

    def epilogue(tile, acc):
        if tile < nq_tiles:
            t = t128_ref[...]
            for c in range(nh):
                xs = _rms(acc[:, c * LANES:(c + 1) * LANES], gq_ref[...])
                p_ref[:, c * LANES:(c + 1) * LANES] = _rope(xs, t, half128).astype(BF16)
        elif tile == nq_tiles:
            t = t128_ref[...]
            for c in range(nh // 2):
                xs = _rms(acc[:, c * LANES:(c + 1) * LANES], gk_ref[...])
                p_ref[:, c * LANES:(c + 1) * LANES] = _rope(xs, t, half128).astype(BF16)
            p_ref[:, tn // 2:] = acc[:, tn // 2:].astype(BF16)
        else:
            t = t64_ref[...]
            for c in range(nh):
                p_ref[:, c * LANES:(c + 1) * LANES] = _rope(acc[:, c * LANES:(c + 1) * LANES], t, half64).astype(BF16)

    for s in range(n_tiles + 1):
        @pl.when(j == s)
        def _(s=s):
            if s == 0:
                xn = _rms(h_ref[...], g_ref[...]).astype(BF16)
                xn_ref[...] = xn
                ps_ref[...] = _rope(_dot(xn, ws_ref[...]), tki_ref[...], half64)
                acc_ref[0] = _dot(xn, w_ref[...])
            elif s < n_tiles:
                acc_ref[s % 2] = _dot(xn_ref[...], w_ref[...])
            if s >= 1:
                epilogue(s - 1, acc_ref[(s - 1) % 2])


def _inproj_a(h, g, w, ws, gq, gk, t128, t64, tki, tm):
    M, D = h.shape
    N = w.shape[1]
    tn = 1024
    hd = D // A_HEADS
    assert hd == LANES and D // 4 * 2 == tn and N == D + 2 * tn
    nt = t128.shape[0] // tm
    n_tiles = N // tn
    kern = functools.partial(_inproj_a_kernel, nq_tiles=D // tn, tn=tn,
                             half128=hd // ROPE_FRAC // 2, half64=IDX_DIM // ROPE_FRAC // 2)
    tab = pl.BlockSpec((tm, 3 * LANES), lambda i, j: (i % nt, 0))
    return pl.pallas_call(
        kern,
        grid=(M // tm, n_tiles + 1),
        in_specs=[
            _row_block(tm, D),
            pl.BlockSpec((1, D), lambda i, j: (0, 0)),
            pl.BlockSpec((D, tn), lambda i, j: (0, jnp.minimum(j, n_tiles - 1))),
            pl.BlockSpec((D, LANES), lambda i, j: (0, 0)),
            pl.BlockSpec((1, LANES), lambda i, j: (0, 0)),
            pl.BlockSpec((1, LANES), lambda i, j: (0, 0)),
            tab, tab, tab,
        ],
        out_specs=[
            pl.BlockSpec((tm, tn), lambda i, j: (i, jnp.maximum(j - 1, 0))),
            pl.BlockSpec((tm, LANES), lambda i, j: (i, 0)),
        ],
        out_shape=[jax.ShapeDtypeStruct((M, N), BF16), jax.ShapeDtypeStruct((M, LANES), F32)],
        scratch_shapes=[pltpu.VMEM((tm, D), BF16), pltpu.VMEM((2, tm, tn), F32)],
        compiler_params=_params("parallel", "arbitrary"),
        name="dsa_inproj",
    )(h, g.reshape(1, D), w, ws, gq, gk, t128, t64, tki)


def _rope_table(pos, d, extra_scale_lanes=None, extra_scale=1.0, reps=None):
    rot = d // ROPE_FRAC
    half = rot // 2
    inv = 1.0 / (ROPE_THETA ** (jnp.arange(0, rot, 2, dtype=F32) / rot))
    ang = pos[:, None] * inv[None, :]
    cos, sin = jnp.cos(ang), jnp.sin(ang)
    T = pos.shape[0]
    one = jnp.ones((T, d - rot), F32)
    zero = lambda n: jnp.zeros((T, n), F32)
    c = jnp.concatenate([cos, cos, one], axis=1)
    s1 = jnp.concatenate([-sin, zero(d - half)], axis=1)
    s2 = jnp.concatenate([zero(half), sin, zero(d - rot)], axis=1)
    if reps is None:
        reps = LANES // d
    c, s1, s2 = (jnp.tile(a, (1, reps)) for a in (c, s1, s2))
    fill = LANES - reps * d
    if fill:
        tail = jnp.ones((T, fill), F32)
        if extra_scale_lanes is not None:
            lo, hi = extra_scale_lanes
            lane = jnp.arange(reps * d, LANES)
            tail = jnp.where((lane >= lo) & (lane < hi), extra_scale, 1.0)[None, :] * tail
        c = jnp.concatenate([c, tail], axis=1)
        s1 = jnp.concatenate([s1, zero(fill)], axis=1)
        s2 = jnp.concatenate([s2, zero(fill)], axis=1)
    return jnp.concatenate([c, s1, s2], axis=1)


def _float_key(x):
    bits = lax.bitcast_convert_type(x, jnp.int32)
    return jnp.where(bits < 0, bits ^ jnp.int32(0x7FFFFFFF), bits)


def _dsa_keys(qi_ref, psq_ref, kie, kio, key_ref, *, tq, row0, nk, n_meta):
    wi = psq_ref[...]
    ke = kie[0:nk, :]
    ko = kio[0:nk, :]
    score = jnp.zeros((tq, nk), F32)
    for p in range(IDX_HEADS // 2):
        qp = qi_ref[:, p * LANES:(p + 1) * LANES]
        c0 = IDX_DIM + 2 * p
        score += jnp.maximum(_dot_nt(qp, ke), 0.0) * wi[:, c0:c0 + 1]
        score += jnp.maximum(_dot_nt(qp, ko), 0.0) * wi[:, c0 + 1:c0 + 2]
    col = lax.broadcasted_iota(jnp.int32, (tq, nk), 1)
    row = lax.broadcasted_iota(jnp.int32, (tq, nk), 0) + row0
    adm = jnp.where(col < LANES, col - n_meta, col - LANES - row - 1) < 0
    score = jnp.where(score == 0.0, 0.0, score)
    key_ref[:, 0:nk] = _float_key(jnp.where(adm, score, -jnp.inf))


def _count_ge(key_ref, cand, r0, rows, nch):
    acc = jnp.zeros((rows, LANES), F32)
    for c in range(nch):
        acc += jnp.where(key_ref[r0:r0 + rows, c * LANES:(c + 1) * LANES] >= cand, 1.0, 0.0)
    return jnp.sum(acc, axis=-1, keepdims=True)


def _search_trip(key_ref, trip, bases, *, tq, nk, topk):
    rb = tq // SEARCH_ROW_BLOCKS
    bases = list(bases)
    for u in range(SEARCH_UNROLL):
        bit = jnp.int32(31) - (trip.astype(jnp.int32) * SEARCH_UNROLL + u)
        step = lax.shift_left(jnp.int32(1), bit)
        for blk in range(SEARCH_ROW_BLOCKS):
            cand = bases[blk] + step
            cnt = _count_ge(key_ref, cand, blk * rb, rb, nk // LANES)
            bases[blk] = jnp.where(cnt >= float(topk), cand, bases[blk])
    return tuple(bases)


def _search_init(tq):
    return tuple(jnp.full((tq // SEARCH_ROW_BLOCKS, 1), INT_MIN, jnp.int32) for _ in range(SEARCH_ROW_BLOCKS))


def _dsa_mask(key_ref, bias_ref, slot, bases, *, tq, nk, nk_all, topk):
    NCH = nk // LANES
    kf = float(topk)
    ones = jnp.ones((LANES, LANES), BF16)
    thr = jnp.maximum(jnp.concatenate(bases, axis=0), NEG_INF_KEY + 1)
    cnt = _count_ge(key_ref, thr, 0, tq, NCH)
    for c in range(NCH):
        sl = slice(c * LANES, (c + 1) * LANES)
        bias_ref[slot, :, sl] = jnp.where(key_ref[:, sl] >= thr, 0.0, -jnp.inf)
    if nk_all > nk:
        bias_ref[slot, :, nk:nk_all] = jnp.full((tq, nk_all - nk), -jnp.inf, F32)

    @pl.when(jnp.max(cnt) > kf)
    def _():
        need = kf - _count_ge(key_ref, thr + 1, 0, tq, NCH)
        rr = lax.broadcasted_iota(jnp.int32, (LANES, LANES), 0)
        cc = lax.broadcasted_iota(jnp.int32, (LANES, LANES), 1)
        tri = jnp.where(rr <= cc, 1.0, 0.0).astype(BF16)
        run = jnp.zeros((tq, LANES), F32)
        for c in range(NCH):
            sl = slice(c * LANES, (c + 1) * LANES)
            kc = key_ref[:, sl]
            eq = kc == thr
            eqb = jnp.where(eq, 1.0, 0.0).astype(BF16)
            rank = run + _dot(eqb, tri)
            keep_eq = jnp.where(rank <= need, 0.0, -jnp.inf)
            bias_ref[slot, :, sl] = jnp.where(kc > thr, 0.0, jnp.where(eq, keep_eq, -jnp.inf))
            run = run + _dot(eqb, ones)


def _attn_heads(qs, kcat, vcat, bias_ref, slot, os, trip, *, nk):
    hd = LANES
    per_group = (A_HEADS // A_KV_HEADS) // ATT_HEADS_PER_TRIP
    g = trip // per_group
    kg = kcat[g, 0:nk, :]
    vg = vcat[g, 0:nk, :]
    heads = [trip * ATT_HEADS_PER_TRIP + r for r in range(ATT_HEADS_PER_TRIP)]
    probs = []
    for h in heads:
        logits = _dot_nt(qs[h], kg) + bias_ref[slot, :, 0:nk]
        mx = jnp.max(logits, axis=-1, keepdims=True)
        probs.append(jnp.exp2(logits - mx).astype(BF16))
    for h, p in zip(heads, probs):
        pv = _dot(p, vg)
        os[h] = (pv[:, 0:hd] / pv[:, hd:hd + 1]).astype(BF16)


def _dsa_step(i, q_ref, qi_ref, psq_ref, o_ref, kcat, vcat, kie, kio, key_ref,
              bias_ref, qs, os, *, tq, nk, topk, n_meta):
    hd = LANES
    _dsa_keys(qi_ref, psq_ref, kie, kio, key_ref, tq=tq, row0=i * tq, nk=nk, n_meta=n_meta)
    bases = lax.fori_loop(0, 32 // SEARCH_UNROLL,
                          lambda t, b: _search_trip(key_ref, t, b, tq=tq, nk=nk, topk=topk), _search_init(tq))
    _dsa_mask(key_ref, bias_ref, 0, bases, tq=tq, nk=nk, nk_all=nk, topk=topk)

    for h in range(A_HEADS):
        qs[h] = q_ref[:, h * hd:(h + 1) * hd]

    def trip(t, carry):
        _attn_heads(qs, kcat, vcat, bias_ref, 0, os, t, nk=nk)
        return carry

    lax.fori_loop(0, A_HEADS // ATT_HEADS_PER_TRIP, trip, 0)
    for h in range(A_HEADS):
        o_ref[:, h * hd:(h + 1) * hd] = os[h]


def _dsa_kernel(q_ref, qi_ref, psq_ref, k_ref, v_ref, psk_ref, km_ref, vm_ref, psm_ref, o_ref,
                kcat, vcat, kie, kio, key_ref, bias_ref, qs, os, *, tq, S, topk, n_meta):
    i = pl.program_id(1)
    NK = LANES + S
    hd = LANES
    G = A_KV_HEADS
    nq = S // tq

    @pl.when(i == 0)
    def _():
        lane = lax.broadcasted_iota(jnp.int32, (NK, hd), 1)
        ones_col = jnp.where(lane == 0, 1.0, 0.0).astype(BF16)
        for g in range(G):
            sl = slice(g * hd, (g + 1) * hd)
            kcat[g, 0:LANES, :] = km_ref[:, sl]
            kcat[g, LANES:, :] = k_ref[:, sl]
            vcat[g, 0:LANES, 0:hd] = vm_ref[:, sl]
            vcat[g, LANES:, 0:hd] = v_ref[:, sl]
            vcat[g, :, hd:] = ones_col
        for dst0, dst1, src in ((0, LANES, psm_ref), (LANES, NK, psk_ref)):
            a = src[...]
            lane = lax.broadcasted_iota(jnp.int32, a.shape, 1)
            even = jnp.where(lane < IDX_DIM, a, 0.0)
            kie[dst0:dst1, :] = even.astype(BF16)
            kio[dst0:dst1, :] = pltpu.roll(even, IDX_DIM, 1).astype(BF16)

    bounds = sorted({-(-nq * (v + 1) // DSA_VARIANTS) for v in range(DSA_VARIANTS)})
    lo = 0
    for hi in bounds:
        pl.when((i >= lo) & (i < hi))(functools.partial(
            _dsa_step, i, q_ref, qi_ref, psq_ref, o_ref, kcat, vcat, kie, kio,
            key_ref, bias_ref, qs, os, tq=tq, nk=LANES + hi * tq, topk=topk, n_meta=n_meta))
        lo = hi


def _dsa(p, ps, pm, psm, B, S, tq, topk):
    M, N = p.shape
    D = N // 2
    kv_w = D // 4
    nq = S // tq
    NK = LANES + S
    qi_col = (D + 2 * kv_w) // 1024
    kern = functools.partial(_dsa_kernel, tq=tq, S=S, topk=topk, n_meta=N_META)
    return pl.pallas_call(
        kern,
        grid=(B, nq),
        in_specs=[
            pl.BlockSpec((tq, D), lambda b, i: (b * nq + i, 0)),
            pl.BlockSpec((tq, 1024), lambda b, i: (b * nq + i, qi_col)),
            pl.BlockSpec((tq, LANES), lambda b, i: (b * nq + i, 0)),
            pl.BlockSpec((S, kv_w), lambda b, i: (b, D // kv_w)),
            pl.BlockSpec((S, kv_w), lambda b, i: (b, D // kv_w + 1)),
            pl.BlockSpec((S, LANES), lambda b, i: (b, 0)),
            pl.BlockSpec((LANES, kv_w), lambda b, i: (0, D // kv_w)),
            pl.BlockSpec((LANES, kv_w), lambda b, i: (0, D // kv_w + 1)),
            pl.BlockSpec((LANES, LANES), lambda b, i: (0, 0)),
        ],
        out_specs=pl.BlockSpec((tq, D), lambda b, i: (b * nq + i, 0)),
        out_shape=jax.ShapeDtypeStruct((M, D), BF16),
        scratch_shapes=[
            pltpu.VMEM((A_KV_HEADS, NK, LANES), BF16),
            pltpu.VMEM((A_KV_HEADS, NK, 2 * LANES), BF16),
            pltpu.VMEM((NK, LANES), BF16),
            pltpu.VMEM((NK, LANES), BF16),
            pltpu.VMEM((tq, NK), jnp.int32),
            pltpu.VMEM((1, tq, NK), F32),
            pltpu.VMEM((A_HEADS, tq, LANES), BF16),
            pltpu.VMEM((A_HEADS, tq, LANES), BF16),
        ],
        compiler_params=_params("parallel", "arbitrary"),
        name="dsa_attention",
    )(p, p, ps, p, p, ps, pm, pm, psm)


def _row_tile(m, pref):
    return pref if m % pref == 0 else m


def kernel(x, meta_tokens, ffn1_norm, ffn1_w_gate, ffn1_w_up, ffn1_w_down, mix_norm, ffn2_norm, ffn2_w_gate, ffn2_w_up, ffn2_w_down, mlstm_w_in, mlstm_b_i, mlstm_b_f, mlstm_head_norm, mlstm_w_out, dsa_w_in, dsa_q_norm, dsa_k_norm, dsa_w_out):
    B, S, D = x.shape
    depth = ffn1_norm.shape[0]
    assert depth == 2 and meta_tokens.shape == (N_META, D)
    F = ffn1_w_gate.shape[-1]
    H, DK, DV = M_HEADS, D // (2 * M_HEADS), D // M_HEADS
    hd = D // A_HEADS
    L = 256
    TM = 1024
    TMO = 512
    TF = 512
    CAST_ROWS = 256
    assert S % L == 0 and S % TM == 0 and (B * S) % TMO == 0 and F % TF == 0

    hx = x.reshape(B * S, D)
    hm = meta_tokens.astype(x.dtype)
    bf = lambda w: w.astype(BF16)

    f1 = (_to_bf16(ffn1_w_gate, CAST_ROWS), _to_bf16(ffn1_w_up, CAST_ROWS), _to_bf16(ffn1_w_down, CAST_ROWS))
    f2 = (_to_bf16(ffn2_w_gate, CAST_ROWS), _to_bf16(ffn2_w_up, CAST_ROWS), _to_bf16(ffn2_w_down, CAST_ROWS))

    def ffn(h, g, ws, layer):
        return _ffn(h, g, *ws, layer, _row_tile(h.shape[0], TM), TF)

    def pad_cols(w):
        return jnp.pad(w, ((0, 0), (0, LANES - w.shape[1])))

    def pad_rows(a, n, front=False):
        r = n - a.shape[0]
        return jnp.pad(a, ((r, 0) if front else (0, r), (0, 0)))

    hx = ffn(hx, ffn1_norm[0], f1, 0)
    hm = ffn(hm, ffn1_norm[0], f1, 0)

    n_wide = 2 * H * DK + 2 * H * DV
    w_wide = bf(mlstm_w_in[0][:, :n_wide])
    w_gate = bf(pad_cols(mlstm_w_in[0][:, n_wide:]))
    px, psx = _inproj_m(hx, mix_norm[0], w_wide, w_gate, TM, 1024)
    pm, psm = _inproj_m(hm, mix_norm[0], w_wide, w_gate, N_META, 1024)

    bias = pad_cols(jnp.concatenate([mlstm_b_i[0], mlstm_b_f[0]]).astype(F32)[None, :])
    gh = mlstm_head_norm[0].astype(F32).reshape(1, H * DV)
    zc = jnp.zeros((H, DK, DV), F32)
    zn = jnp.zeros((H, 1, DK), F32)
    zm = jnp.zeros((H, 1, LANES), F32)
    ym, c0, n0, m0 = _mlstm(pad_rows(pm, L, True), pad_rows(psm, L, True), bias, gh, zc, zn, zm,
                            1, L, L - N_META)
    yx, _, _, _ = _mlstm(px, psx, bias, gh, c0[0], n0[0], m0[0], B, L, 0)
    w_out = bf(mlstm_w_out[0])
    hx = _outproj(yx, w_out, hx, TMO)
    hm = _outproj(ym[L - N_META:], w_out, hm, N_META)

    hx = ffn(hx, ffn2_norm[0], f2, 0)
    hm = ffn(hm, ffn2_norm[0], f2, 0)

    hx = ffn(hx, ffn1_norm[1], f1, 1)
    hm = ffn(hm, ffn1_norm[1], f1, 1)

    n_wide = A_HEADS * hd + 2 * A_KV_HEADS * hd + IDX_HEADS * IDX_DIM
    w_wide = bf(dsa_w_in[0][:, :n_wide])
    w_idx = bf(pad_cols(dsa_w_in[0][:, n_wide:]))
    gq = (dsa_q_norm[0].astype(F32) * (hd ** -0.5 * float(np.log2(np.e))))[None, :]
    gk = dsa_k_norm[0].astype(F32)[None, :]
    wi_scale = IDX_HEADS ** -0.5 * IDX_DIM ** -0.5
    pos = jnp.arange(N_META + S, dtype=F32)
    t128 = _rope_table(pos, hd)
    t64 = _rope_table(pos, IDX_DIM)
    tki = _rope_table(pos, IDX_DIM, (IDX_DIM, IDX_DIM + IDX_HEADS), wi_scale, reps=1)
    tabs_x = [t[N_META:] for t in (t128, t64, tki)]
    tabs_m = [t[:N_META] for t in (t128, t64, tki)]
    px, psx = _inproj_a(hx, mix_norm[1], w_wide, w_idx, gq, gk, *tabs_x, TM)
    pm, psm = _inproj_a(hm, mix_norm[1], w_wide, w_idx, gq, gk, *tabs_m, N_META)

    topk = min(TOPK_MAX, (N_META + S - N_META) // 4)
    ox = _dsa(px, psx, pad_rows(pm, LANES), pad_rows(psm, LANES), B, S, 256, topk)
    hx = _outproj(ox, bf(dsa_w_out[0]), hx, TMO)

    hx = ffn(hx, ffn2_norm[1], f2, 1)
    return hx.reshape(B, S, D)
```

```python
import functools

import numpy as np
import jax
import jax.numpy as jnp
from jax import lax
from jax.experimental import pallas as pl
from jax.experimental.pallas import tpu as pltpu

F32 = jnp.float32
BF16 = jnp.bfloat16

N_META = 16
FFN_HALF = 0.5
NORM_EPS = 1e-6
ROPE_THETA = 500000.0
ROPE_FRAC = 4
M_HEADS = 4
A_HEADS = 16
A_KV_HEADS = 4
IDX_HEADS = 16
IDX_DIM = 64
TOPK_MAX = 256
LOG_I_PAD = -1e30

LANES = 128
VMEM_LIMIT = 60 * 1024 * 1024
DSA_VARIANTS = 3
SEARCH_ROW_BLOCKS = 2
SEARCH_UNROLL = 4
ATT_HEADS_PER_TRIP = 4
FFN_SPLIT = 2
MLSTM_CHUNK = 256
ROW_TILE = 1024
OUT_ROW_TILE = 512
FF_TILE = 512
PROJ_COL_TILE = 1024
DSA_Q_TILE = 256
CAST_ROWS = 256
INT_MIN = int(np.iinfo(np.int32).min)
NEG_INF_KEY = int(np.array(-np.inf, np.float32).view(np.int32) ^ np.int32(0x7FFFFFFF))


def _params(*sem):
    return pltpu.CompilerParams(dimension_semantics=sem, vmem_limit_bytes=VMEM_LIMIT)


def _row_block(tm, d):
    return pl.BlockSpec((tm, d), lambda i, j: (i, 0))


def _rms(x, g):
    return x * lax.rsqrt(jnp.mean(x * x, axis=-1, keepdims=True) + NORM_EPS) * g


def _dot(a, b):
    return jnp.dot(a, b, preferred_element_type=F32)


def _dot_nt(a, b):
    return lax.dot_general(a, b, (((1,), (1,)), ((), ())), preferred_element_type=F32)


def _swiglu_half(xn, wg_ref, wu_ref, wd_ref):
    w = wg_ref.shape[1] // FFN_SPLIT
    acc = None
    for c in range(FFN_SPLIT):
        sl = slice(c * w, (c + 1) * w)
        gate = _dot(xn, wg_ref[:, sl])
        up = _dot(xn, wu_ref[:, sl])
        act = (gate * jax.nn.sigmoid(gate)) * (up * FFN_HALF)
        part = _dot(act.astype(BF16), wd_ref[sl, :])
        acc = part if acc is None else acc + part
    return acc


def _ffn_kernel(h_ref, g_ref, wg_ref, wu_ref, wd_ref, o_ref, xn_ref):
    j = pl.program_id(1)

    @pl.when(j == 0)
    def _():
        h = h_ref[...]
        xn = _rms(h, g_ref[...]).astype(BF16)
        xn_ref[...] = xn
        o_ref[...] = h + _swiglu_half(xn, wg_ref, wu_ref, wd_ref)

    @pl.when(j > 0)
    def _():
        o_ref[...] += _swiglu_half(xn_ref[...], wg_ref, wu_ref, wd_ref)


def _cast_kernel(x_ref, o_ref):
    o_ref[...] = x_ref[...].astype(o_ref.dtype)


def _to_bf16(w, br):
    nl, R, C = w.shape
    spec = pl.BlockSpec((None, br, C), lambda l, r: (l, r, 0))
    return pl.pallas_call(
        _cast_kernel,
        grid=(nl, R // br),
        in_specs=[spec],
        out_specs=spec,
        out_shape=jax.ShapeDtypeStruct(w.shape, BF16),
        compiler_params=_params("parallel", "parallel"),
        name="cast_bf16",
    )(w)


def _ffn(h, g, wg, wu, wd, layer, tm, tf):
    M, D = h.shape
    F = wg.shape[2]
    return pl.pallas_call(
        _ffn_kernel,
        grid=(M // tm, F // tf),
        in_specs=[
            _row_block(tm, D),
            pl.BlockSpec((1, D), lambda i, j: (0, 0)),
            pl.BlockSpec((None, D, tf), lambda i, j: (layer, 0, j)),
            pl.BlockSpec((None, D, tf), lambda i, j: (layer, 0, j)),
            pl.BlockSpec((None, tf, D), lambda i, j: (layer, j, 0)),
        ],
        out_specs=pl.BlockSpec((tm, D), lambda i, j: (i, 0)),
        out_shape=jax.ShapeDtypeStruct((M, D), F32),
        scratch_shapes=[pltpu.VMEM((tm, D), BF16)],
        compiler_params=_params("parallel", "arbitrary"),
        name="ffn",
    )(h, g.reshape(1, D), wg, wu, wd)


def _inproj_m_kernel(h_ref, g_ref, w_ref, ws_ref, p_ref, ps_ref, xn_ref):
    j = pl.program_id(1)

    @pl.when(j == 0)
    def _():
        xn = _rms(h_ref[...], g_ref[...]).astype(BF16)
        xn_ref[...] = xn
        ps_ref[...] = _dot(xn, ws_ref[...])
        p_ref[...] = _dot(xn, w_ref[...]).astype(BF16)

    @pl.when(j > 0)
    def _():
        p_ref[...] = _dot(xn_ref[...], w_ref[...]).astype(BF16)


def _inproj_m(h, g, w, ws, tm, tn):
    M, D = h.shape
    N = w.shape[1]
    return pl.pallas_call(
        _inproj_m_kernel,
        grid=(M // tm, N // tn),
        in_specs=[
            _row_block(tm, D),
            pl.BlockSpec((1, D), lambda i, j: (0, 0)),
            pl.BlockSpec((D, tn), lambda i, j: (0, j)),
            pl.BlockSpec((D, LANES), lambda i, j: (0, 0)),
        ],
        out_specs=[
            pl.BlockSpec((tm, tn), lambda i, j: (i, j)),
            pl.BlockSpec((tm, LANES), lambda i, j: (i, 0)),
        ],
        out_shape=[jax.ShapeDtypeStruct((M, N), BF16), jax.ShapeDtypeStruct((M, LANES), F32)],
        scratch_shapes=[pltpu.VMEM((tm, D), BF16)],
        compiler_params=_params("parallel", "arbitrary"),
        name="mlstm_inproj",
    )(h, g.reshape(1, D), w, ws)


def _log_sigmoid(x):
    return jnp.minimum(x, 0.0) - jnp.log1p(jnp.exp(-jnp.abs(x)))


def _mlstm_kernel(q_ref, k_ref, v_ref, o_ref, ps_ref, bias_ref, gh_ref, c0_ref, n0_ref, m0_ref,
                  y_ref, c_ref, n_ref, m_ref, *, L, npad, H, DK, DV):
    @pl.when(pl.program_id(1) == 0)
    def _():
        c_ref[...] = c0_ref[...]
        n_ref[...] = n0_ref[...]
        m_ref[...] = m0_ref[...]

    gates = ps_ref[...] + bias_ref[...]
    logf = _log_sigmoid(gates)
    if npad:
        valid = lax.broadcasted_iota(jnp.int32, (L, 1), 0) >= npad
        logf = jnp.where(valid, logf, 0.0)
        gates = jnp.where(valid, gates, LOG_I_PAD)
    ii = lax.broadcasted_iota(jnp.int32, (L, L), 0)
    jj = lax.broadcasted_iota(jnp.int32, (L, L), 1)
    causal = jj <= ii
    cum = jnp.dot(causal.astype(F32), logf, preferred_element_type=F32,
                  precision=lax.Precision.HIGHEST)
    cum_t = cum.T
    gates_t = gates.T

    updates = []
    for h in range(H):
        b_col = cum[:, H + h:H + h + 1]
        b_row = cum_t[H + h:H + h + 1, :]
        li_col = gates[:, h:h + 1]
        li_row = gates_t[h:h + 1, :]
        m_prev = m_ref[h][:, 0:1]
        q = q_ref[:, h * DK:(h + 1) * DK] * (DK ** -0.5)
        k = k_ref[:, h * DK:(h + 1) * DK]
        v = v_ref[:, h * DV:(h + 1) * DV]
        c_old = c_ref[h]
        n_old = n_ref[h]

        dm = jnp.where(causal, b_col - b_row + li_row, -jnp.inf)
        inter = b_col + m_prev
        m_t = jnp.maximum(inter, jnp.max(dm, axis=-1, keepdims=True))
        decay_mat = jnp.exp(dm - m_t)
        w_inter = jnp.exp(inter - m_t)
        b_last = b_col[L - 1:L, :]
        g_row = b_last - b_row + li_row
        g_col = b_last - b_col + li_col
        m_new = jnp.maximum(b_last + m_prev, jnp.max(g_row, axis=-1, keepdims=True))
        decay = jnp.exp(b_last + m_prev - m_new)
        updates.append((m_new, decay, jnp.exp(g_col - m_new)))

        s = _dot_nt(q, k) * decay_mat
        num = w_inter * _dot(q, c_old.astype(BF16)) + _dot(s.astype(BF16), v)
        qn = jnp.sum(q.astype(F32) * n_old, axis=-1, keepdims=True)
        den = w_inter * qn + jnp.sum(s, axis=-1, keepdims=True)
        hout = num / jnp.maximum(jnp.abs(den), jnp.exp(-m_t))

        hn = _rms(hout, gh_ref[:, h * DV:(h + 1) * DV])
        og = jax.nn.sigmoid(o_ref[:, h * DV:(h + 1) * DV].astype(F32))
        y_ref[:, h * DV:(h + 1) * DV] = (og * hn).astype(BF16)

    for h, (m_new, decay, wk) in enumerate(updates):
        k = k_ref[:, h * DK:(h + 1) * DK]
        v = v_ref[:, h * DV:(h + 1) * DV]
        kw = k.astype(F32) * wk
        c_ref[h] = decay * c_ref[h] + _dot(kw.T.astype(BF16), v)
        n_ref[h] = decay * n_ref[h] + jnp.sum(kw, axis=0, keepdims=True)
        m_ref[h] = jnp.broadcast_to(m_new, (1, LANES))


def _mlstm(p, ps, bias, gh, c0, n0, m0, B, L, npad):
    M = p.shape[0]
    H = M_HEADS
    DK, DV = c0.shape[1], c0.shape[2]
    NC = M // (B * L)
    row = lambda b, c: b * NC + c
    qk_w, vo_w = H * DK, H * DV
    assert vo_w == 2 * qk_w
    kern = functools.partial(_mlstm_kernel, L=L, npad=npad, H=H, DK=DK, DV=DV)
    return pl.pallas_call(
        kern,
        grid=(B, NC),
        in_specs=[
            pl.BlockSpec((L, qk_w), lambda b, c: (row(b, c), 0)),
            pl.BlockSpec((L, qk_w), lambda b, c: (row(b, c), 1)),
            pl.BlockSpec((L, vo_w), lambda b, c: (row(b, c), 1)),
            pl.BlockSpec((L, vo_w), lambda b, c: (row(b, c), 2)),
            pl.BlockSpec((L, LANES), lambda b, c: (row(b, c), 0)),
            pl.BlockSpec((1, LANES), lambda b, c: (0, 0)),
            pl.BlockSpec((1, vo_w), lambda b, c: (0, 0)),
            pl.BlockSpec((H, DK, DV), lambda b, c: (0, 0, 0)),
            pl.BlockSpec((H, 1, DK), lambda b, c: (0, 0, 0)),
            pl.BlockSpec((H, 1, LANES), lambda b, c: (0, 0, 0)),
        ],
        out_specs=[
            pl.BlockSpec((L, vo_w), lambda b, c: (row(b, c), 0)),
            pl.BlockSpec((None, H, DK, DV), lambda b, c: (b, 0, 0, 0)),
            pl.BlockSpec((None, H, 1, DK), lambda b, c: (b, 0, 0, 0)),
            pl.BlockSpec((None, H, 1, LANES), lambda b, c: (b, 0, 0, 0)),
        ],
        out_shape=[
            jax.ShapeDtypeStruct((M, vo_w), BF16),
            jax.ShapeDtypeStruct((B, H, DK, DV), F32),
            jax.ShapeDtypeStruct((B, H, 1, DK), F32),
            jax.ShapeDtypeStruct((B, H, 1, LANES), F32),
        ],
        compiler_params=_params("parallel", "arbitrary"),
        name="mlstm",
    )(p, p, p, p, ps, bias, gh, c0, n0, m0)


def _outproj_kernel(y_ref, w_ref, h_ref, o_ref):
    o_ref[...] = h_ref[...] + _dot(y_ref[...], w_ref[...])


def _outproj(y, w, h, tm):
    M, D = h.shape
    K = y.shape[1]
    return pl.pallas_call(
        _outproj_kernel,
        grid=(M // tm,),
        in_specs=[
            pl.BlockSpec((tm, K), lambda i: (i, 0)),
            pl.BlockSpec((K, D), lambda i: (0, 0)),
            pl.BlockSpec((tm, D), lambda i: (i, 0)),
        ],
        out_specs=pl.BlockSpec((tm, D), lambda i: (i, 0)),
        out_shape=jax.ShapeDtypeStruct((M, D), F32),
        compiler_params=_params("parallel"),
        name="outproj",
    )(y, w, h)


def _rope(x, tab, half):
    c = tab[:, 0:LANES]
    s1 = tab[:, LANES:2 * LANES]
    s2 = tab[:, 2 * LANES:3 * LANES]
    return x * c + pltpu.roll(x, LANES - half, 1) * s1 + pltpu.roll(x, half, 1) * s2


def _inproj_a_kernel(h_ref, g_ref, w_ref, ws_ref, gq_ref, gk_ref, t128_ref, t64_ref, tki_ref,
                     p_ref, ps_ref, xn_ref, acc_ref, *, nq_tiles, tn, half128, half64):
    j = pl.program_id(1)
    nh = tn // LANES
    n_tiles = nq_tiles + 2

    def epilogue(tile, acc):
        if tile < nq_tiles:
            t = t128_ref[...]
            for c in range(nh):
                xs = _rms(acc[:, c * LANES:(c + 1) * LANES], gq_ref[...])
                p_ref[:, c * LANES:(c + 1) * LANES] = _rope(xs, t, half128).astype(BF16)
        elif tile == nq_tiles:
            t = t128_ref[...]
            for c in range(nh // 2):
                xs = _rms(acc[:, c * LANES:(c + 1) * LANES], gk_ref[...])
                p_ref[:, c * LANES:(c + 1) * LANES] = _rope(xs, t, half128).astype(BF16)
            p_ref[:, tn // 2:] = acc[:, tn // 2:].astype(BF16)
        else:
            t = t64_ref[...]
            for c in range(nh):
                p_ref[:, c * LANES:(c + 1) * LANES] = _rope(acc[:, c * LANES:(c + 1) * LANES], t, half64).astype(BF16)

    for s in range(n_tiles + 1):
        @pl.when(j == s)
        def _(s=s):
            if s == 0:
                xn = _rms(h_ref[...], g_ref[...]).astype(BF16)
                xn_ref[...] = xn
                ps_ref[...] = _rope(_dot(xn, ws_ref[...]), tki_ref[...], half64)
                acc_ref[0] = _dot(xn, w_ref[...])
            elif s < n_tiles:
                acc_ref[s % 2] = _dot(xn_ref[...], w_ref[...])
            if s >= 1:
                epilogue(s - 1, acc_ref[(s - 1) % 2])


def _inproj_a(h, g, w, ws, gq, gk, t128, t64, tki, tm):
    M, D = h.shape
    N = w.shape[1]
    tn = PROJ_COL_TILE
    hd = D // A_HEADS
    assert hd == LANES and D // 4 * 2 == tn and N == D + 2 * tn
    nt = t128.shape[0] // tm
    n_tiles = N // tn
    kern = functools.partial(_inproj_a_kernel, nq_tiles=D // tn, tn=tn,
                             half128=hd // ROPE_FRAC // 2, half64=IDX_DIM // ROPE_FRAC // 2)
    tab = pl.BlockSpec((tm, 3 * LANES), lambda i, j: (i % nt, 0))
    return pl.pallas_call(
        kern,
        grid=(M // tm, n_tiles + 1),
        in_specs=[
            _row_block(tm, D),
            pl.BlockSpec((1, D), lambda i, j: (0, 0)),
            pl.BlockSpec((D, tn), lambda i, j: (0, jnp.minimum(j, n_tiles - 1))),
            pl.BlockSpec((D, LANES), lambda i, j: (0, 0)),
            pl.BlockSpec((1, LANES), lambda i, j: (0, 0)),
            pl.BlockSpec((1, LANES), lambda i, j: (0, 0)),
            tab, tab, tab,
        ],
        out_specs=[
            pl.BlockSpec((tm, tn), lambda i, j: (i, jnp.maximum(j - 1, 0))),
            pl.BlockSpec((tm, LANES), lambda i, j: (i, 0)),
        ],
        out_shape=[jax.ShapeDtypeStruct((M, N), BF16), jax.ShapeDtypeStruct((M, LANES), F32)],
        scratch_shapes=[pltpu.VMEM((tm, D), BF16), pltpu.VMEM((2, tm, tn), F32)],
        compiler_params=_params("parallel", "arbitrary"),
        name="dsa_inproj",
    )(h, g.reshape(1, D), w, ws, gq, gk, t128, t64, tki)


def _rope_table(pos, d, extra_scale_lanes=None, extra_scale=1.0, reps=None):
    rot = d // ROPE_FRAC
    half = rot // 2
    inv = 1.0 / (ROPE_THETA ** (jnp.arange(0, rot, 2, dtype=F32) / rot))
    ang = pos[:, None] * inv[None, :]
    cos, sin = jnp.cos(ang), jnp.sin(ang)
    T = pos.shape[0]
    one = jnp.ones((T, d - rot), F32)
    zero = lambda n: jnp.zeros((T, n), F32)
    c = jnp.concatenate([cos, cos, one], axis=1)
    s1 = jnp.concatenate([-sin, zero(d - half)], axis=1)
    s2 = jnp.concatenate([zero(half), sin, zero(d - rot)], axis=1)
    if reps is None:
        reps = LANES // d
    c, s1, s2 = (jnp.tile(a, (1, reps)) for a in (c, s1, s2))
    fill = LANES - reps * d
    if fill:
        tail = jnp.ones((T, fill), F32)
        if extra_scale_lanes is not None:
            lo, hi = extra_scale_lanes
            lane = jnp.arange(reps * d, LANES)
            tail = jnp.where((lane >= lo) & (lane < hi), extra_scale, 1.0)[None, :] * tail
        c = jnp.concatenate([c, tail], axis=1)
        s1 = jnp.concatenate([s1, zero(fill)], axis=1)
        s2 = jnp.concatenate([s2, zero(fill)], axis=1)
    return jnp.concatenate([c, s1, s2], axis=1)


def _float_key(x):
    bits = lax.bitcast_convert_type(x, jnp.int32)
    return jnp.where(bits < 0, bits ^ jnp.int32(0x7FFFFFFF), bits)


def _dsa_keys(qi_ref, psq_ref, kie, kio, key_ref, *, tq, row0, nk, n_meta):
    wi = psq_ref[...]
    ke = kie[0:nk, :]
    ko = kio[0:nk, :]
    score = jnp.zeros((tq, nk), F32)
    for p in range(IDX_HEADS // 2):
        qp = qi_ref[:, p * LANES:(p + 1) * LANES]
        c0 = IDX_DIM + 2 * p
        score += jnp.maximum(_dot_nt(qp, ke), 0.0) * wi[:, c0:c0 + 1]
        score += jnp.maximum(_dot_nt(qp, ko), 0.0) * wi[:, c0 + 1:c0 + 2]
    col = lax.broadcasted_iota(jnp.int32, (tq, nk), 1)
    row = lax.broadcasted_iota(jnp.int32, (tq, nk), 0) + row0
    adm = jnp.where(col < LANES, col - n_meta, col - LANES - row - 1) < 0
    score = jnp.where(score == 0.0, 0.0, score)
    key_ref[:, 0:nk] = _float_key(jnp.where(adm, score, -jnp.inf))


def _count_ge(key_ref, cand, r0, rows, nch):
    acc = jnp.zeros((rows, LANES), F32)
    for c in range(nch):
        acc += jnp.where(key_ref[r0:r0 + rows, c * LANES:(c + 1) * LANES] >= cand, 1.0, 0.0)
    return jnp.sum(acc, axis=-1, keepdims=True)


def _search_trip(key_ref, trip, bases, *, tq, nk, topk):
    rb = tq // SEARCH_ROW_BLOCKS
    bases = list(bases)
    for u in range(SEARCH_UNROLL):
        bit = jnp.int32(31) - (trip.astype(jnp.int32) * SEARCH_UNROLL + u)
        step = lax.shift_left(jnp.int32(1), bit)
        for blk in range(SEARCH_ROW_BLOCKS):
            cand = bases[blk] + step
            cnt = _count_ge(key_ref, cand, blk * rb, rb, nk // LANES)
            bases[blk] = jnp.where(cnt >= float(topk), cand, bases[blk])
    return tuple(bases)


def _search_init(tq):
    return tuple(jnp.full((tq // SEARCH_ROW_BLOCKS, 1), INT_MIN, jnp.int32) for _ in range(SEARCH_ROW_BLOCKS))


def _dsa_mask(key_ref, bias_ref, bases, *, tq, nk, topk):
    NCH = nk // LANES
    kf = float(topk)
    ones = jnp.ones((LANES, LANES), BF16)
    thr = jnp.maximum(jnp.concatenate(bases, axis=0), NEG_INF_KEY + 1)
    cnt = _count_ge(key_ref, thr, 0, tq, NCH)
    for c in range(NCH):
        sl = slice(c * LANES, (c + 1) * LANES)
        bias_ref[:, sl] = jnp.where(key_ref[:, sl] >= thr, 0.0, -jnp.inf)

    @pl.when(jnp.max(cnt) > kf)
    def _():
        need = kf - _count_ge(key_ref, thr + 1, 0, tq, NCH)
        rr = lax.broadcasted_iota(jnp.int32, (LANES, LANES), 0)
        cc = lax.broadcasted_iota(jnp.int32, (LANES, LANES), 1)
        tri = jnp.where(rr <= cc, 1.0, 0.0).astype(BF16)
        run = jnp.zeros((tq, LANES), F32)
        for c in range(NCH):
            sl = slice(c * LANES, (c + 1) * LANES)
            kc = key_ref[:, sl]
            eq = kc == thr
            eqb = jnp.where(eq, 1.0, 0.0).astype(BF16)
            rank = run + _dot(eqb, tri)
            keep_eq = jnp.where(rank <= need, 0.0, -jnp.inf)
            bias_ref[:, sl] = jnp.where(kc > thr, 0.0, jnp.where(eq, keep_eq, -jnp.inf))
            run = run + _dot(eqb, ones)


def _attn_heads(qs, kcat, vcat, bias_ref, os, trip, *, nk):
    hd = LANES
    per_group = (A_HEADS // A_KV_HEADS) // ATT_HEADS_PER_TRIP
    g = trip // per_group
    kg = kcat[g, 0:nk, :]
    vg = vcat[g, 0:nk, :]
    heads = [trip * ATT_HEADS_PER_TRIP + r for r in range(ATT_HEADS_PER_TRIP)]
    probs = []
    for h in heads:
        logits = _dot_nt(qs[h], kg) + bias_ref[:, 0:nk]
        mx = jnp.max(logits, axis=-1, keepdims=True)
        probs.append(jnp.exp2(logits - mx).astype(BF16))
    for h, p in zip(heads, probs):
        pv = _dot(p, vg)
        os[h] = (pv[:, 0:hd] / pv[:, hd:hd + 1]).astype(BF16)


def _dsa_step(i, q_ref, qi_ref, psq_ref, o_ref, kcat, vcat, kie, kio, key_ref,
              bias_ref, qs, os, *, tq, nk, topk, n_meta):
    hd = LANES
    _dsa_keys(qi_ref, psq_ref, kie, kio, key_ref, tq=tq, row0=i * tq, nk=nk, n_meta=n_meta)
    bases = lax.fori_loop(0, 32 // SEARCH_UNROLL,
                          lambda t, b: _search_trip(key_ref, t, b, tq=tq, nk=nk, topk=topk), _search_init(tq))
    _dsa_mask(key_ref, bias_ref, bases, tq=tq, nk=nk, topk=topk)

    for h in range(A_HEADS):
        qs[h] = q_ref[:, h * hd:(h + 1) * hd]

    def trip(t, carry):
        _attn_heads(qs, kcat, vcat, bias_ref, os, t, nk=nk)
        return carry

    lax.fori_loop(0, A_HEADS // ATT_HEADS_PER_TRIP, trip, 0)
    for h in range(A_HEADS):
        o_ref[:, h * hd:(h + 1) * hd] = os[h]


def _dsa_kernel(q_ref, qi_ref, psq_ref, k_ref, v_ref, psk_ref, km_ref, vm_ref, psm_ref, o_ref,
                kcat, vcat, kie, kio, key_ref, bias_ref, qs, os, *, tq, S, topk, n_meta):
    i = pl.program_id(1)
    NK = LANES + S
    hd = LANES
    G = A_KV_HEADS
    nq = S // tq

    @pl.when(i == 0)
    def _():
        lane = lax.broadcasted_iota(jnp.int32, (NK, hd), 1)
        ones_col = jnp.where(lane == 0, 1.0, 0.0).astype(BF16)
        for g in range(G):
            sl = slice(g * hd, (g + 1) * hd)
            kcat[g, 0:LANES, :] = km_ref[:, sl]
            kcat[g, LANES:, :] = k_ref[:, sl]
            vcat[g, 0:LANES, 0:hd] = vm_ref[:, sl]
            vcat[g, LANES:, 0:hd] = v_ref[:, sl]
            vcat[g, :, hd:] = ones_col
        for dst0, dst1, src in ((0, LANES, psm_ref), (LANES, NK, psk_ref)):
            a = src[...]
            lane = lax.broadcasted_iota(jnp.int32, a.shape, 1)
            even = jnp.where(lane < IDX_DIM, a, 0.0)
            kie[dst0:dst1, :] = even.astype(BF16)
            kio[dst0:dst1, :] = pltpu.roll(even, IDX_DIM, 1).astype(BF16)

    bounds = sorted({-(-nq * (v + 1) // DSA_VARIANTS) for v in range(DSA_VARIANTS)})
    lo = 0
    for hi in bounds:
        pl.when((i >= lo) & (i < hi))(functools.partial(
            _dsa_step, i, q_ref, qi_ref, psq_ref, o_ref, kcat, vcat, kie, kio,
            key_ref, bias_ref, qs, os, tq=tq, nk=LANES + hi * tq, topk=topk, n_meta=n_meta))
        lo = hi


def _dsa(p, ps, pm, psm, B, S, tq, topk):
    M, N = p.shape
    D = N // 2
    kv_w = D // 4
    nq = S // tq
    NK = LANES + S
    qi_w = IDX_HEADS * IDX_DIM
    qi_col = (D + 2 * kv_w) // qi_w
    kern = functools.partial(_dsa_kernel, tq=tq, S=S, topk=topk, n_meta=N_META)
    return pl.pallas_call(
        kern,
        grid=(B, nq),
        in_specs=[
            pl.BlockSpec((tq, D), lambda b, i: (b * nq + i, 0)),
            pl.BlockSpec((tq, qi_w), lambda b, i: (b * nq + i, qi_col)),
            pl.BlockSpec((tq, LANES), lambda b, i: (b * nq + i, 0)),
            pl.BlockSpec((S, kv_w), lambda b, i: (b, D // kv_w)),
            pl.BlockSpec((S, kv_w), lambda b, i: (b, D // kv_w + 1)),
            pl.BlockSpec((S, LANES), lambda b, i: (b, 0)),
            pl.BlockSpec((LANES, kv_w), lambda b, i: (0, D // kv_w)),
            pl.BlockSpec((LANES, kv_w), lambda b, i: (0, D // kv_w + 1)),
            pl.BlockSpec((LANES, LANES), lambda b, i: (0, 0)),
        ],
        out_specs=pl.BlockSpec((tq, D), lambda b, i: (b * nq + i, 0)),
        out_shape=jax.ShapeDtypeStruct((M, D), BF16),
        scratch_shapes=[
            pltpu.VMEM((A_KV_HEADS, NK, LANES), BF16),
            pltpu.VMEM((A_KV_HEADS, NK, 2 * LANES), BF16),
            pltpu.VMEM((NK, LANES), BF16),
            pltpu.VMEM((NK, LANES), BF16),
            pltpu.VMEM((tq, NK), jnp.int32),
            pltpu.VMEM((tq, NK), F32),
            pltpu.VMEM((A_HEADS, tq, LANES), BF16),
            pltpu.VMEM((A_HEADS, tq, LANES), BF16),
        ],
        compiler_params=_params("parallel", "arbitrary"),
        name="dsa_attention",
    )(p, p, ps, p, p, ps, pm, pm, psm)


def _row_tile(m, pref):
    return pref if m % pref == 0 else m


def kernel(x, meta_tokens, ffn1_norm, ffn1_w_gate, ffn1_w_up, ffn1_w_down, mix_norm, ffn2_norm, ffn2_w_gate, ffn2_w_up, ffn2_w_down, mlstm_w_in, mlstm_b_i, mlstm_b_f, mlstm_head_norm, mlstm_w_out, dsa_w_in, dsa_q_norm, dsa_k_norm, dsa_w_out):
    B, S, D = x.shape
    depth = ffn1_norm.shape[0]
    assert depth == 2 and meta_tokens.shape == (N_META, D)
    F = ffn1_w_gate.shape[-1]
    H, DK, DV = M_HEADS, D // (2 * M_HEADS), D // M_HEADS
    hd = D // A_HEADS
    L, TM, TMO, TF = MLSTM_CHUNK, ROW_TILE, OUT_ROW_TILE, FF_TILE
    assert S % L == 0 and S % TM == 0 and (B * S) % TMO == 0 and F % TF == 0

    hx = x.reshape(B * S, D)
    hm = meta_tokens.astype(x.dtype)
    bf = lambda w: w.astype(BF16)

    f1 = (_to_bf16(ffn1_w_gate, CAST_ROWS), _to_bf16(ffn1_w_up, CAST_ROWS), _to_bf16(ffn1_w_down, CAST_ROWS))
    f2 = (_to_bf16(ffn2_w_gate, CAST_ROWS), _to_bf16(ffn2_w_up, CAST_ROWS), _to_bf16(ffn2_w_down, CAST_ROWS))

    def ffn(h, g, ws, layer):
        return _ffn(h, g, *ws, layer, _row_tile(h.shape[0], TM), TF)

    def pad_cols(w):
        return jnp.pad(w, ((0, 0), (0, LANES - w.shape[1])))

    def pad_rows(a, n, front=False):
        r = n - a.shape[0]
        return jnp.pad(a, ((r, 0) if front else (0, r), (0, 0)))

    hx = ffn(hx, ffn1_norm[0], f1, 0)
    hm = ffn(hm, ffn1_norm[0], f1, 0)

    n_wide = 2 * H * DK + 2 * H * DV
    w_wide = bf(mlstm_w_in[0][:, :n_wide])
    w_gate = bf(pad_cols(mlstm_w_in[0][:, n_wide:]))
    px, psx = _inproj_m(hx, mix_norm[0], w_wide, w_gate, TM, PROJ_COL_TILE)
    pm, psm = _inproj_m(hm, mix_norm[0], w_wide, w_gate, N_META, PROJ_COL_TILE)

    bias = pad_cols(jnp.concatenate([mlstm_b_i[0], mlstm_b_f[0]]).astype(F32)[None, :])
    gh = mlstm_head_norm[0].astype(F32).reshape(1, H * DV)
    zc = jnp.zeros((H, DK, DV), F32)
    zn = jnp.zeros((H, 1, DK), F32)
    zm = jnp.zeros((H, 1, LANES), F32)
    ym, c0, n0, m0 = _mlstm(pad_rows(pm, L, True), pad_rows(psm, L, True), bias, gh, zc, zn, zm,
                            1, L, L - N_META)
    yx, _, _, _ = _mlstm(px, psx, bias, gh, c0[0], n0[0], m0[0], B, L, 0)
    w_out = bf(mlstm_w_out[0])
    hx = _outproj(yx, w_out, hx, TMO)
    hm = _outproj(ym[L - N_META:], w_out, hm, N_META)

    hx = ffn(hx, ffn2_norm[0], f2, 0)
    hm = ffn(hm, ffn2_norm[0], f2, 0)

    hx = ffn(hx, ffn1_norm[1], f1, 1)
    hm = ffn(hm, ffn1_norm[1], f1, 1)

    n_wide = A_HEADS * hd + 2 * A_KV_HEADS * hd + IDX_HEADS * IDX_DIM
    w_wide = bf(dsa_w_in[0][:, :n_wide])
    w_idx = bf(pad_cols(dsa_w_in[0][:, n_wide:]))
    gq = (dsa_q_norm[0].astype(F32) * (hd ** -0.5 * float(np.log2(np.e))))[None, :]
    gk = dsa_k_norm[0].astype(F32)[None, :]
    wi_scale = IDX_HEADS ** -0.5 * IDX_DIM ** -0.5
    pos = jnp.arange(N_META + S, dtype=F32)
    t128 = _rope_table(pos, hd)
    t64 = _rope_table(pos, IDX_DIM)
    tki = _rope_table(pos, IDX_DIM, (IDX_DIM, IDX_DIM + IDX_HEADS), wi_scale, reps=1)
    tabs_x = [t[N_META:] for t in (t128, t64, tki)]
    tabs_m = [t[:N_META] for t in (t128, t64, tki)]
    px, psx = _inproj_a(hx, mix_norm[1], w_wide, w_idx, gq, gk, *tabs_x, TM)
    pm, psm = _inproj_a(hm, mix_norm[1], w_wide, w_idx, gq, gk, *tabs_m, N_META)

    topk = min(TOPK_MAX, (N_META + S - N_META) // 4)
    ox = _dsa(px, psx, pad_rows(pm, LANES), pad_rows(psm, LANES), B, S, DSA_Q_TILE, topk)
    hx = _outproj(ox, bf(dsa_w_out[0]), hx, TMO)

    hx = ffn(hx, ffn2_norm[1], f2, 1)
    return hx.reshape(B, S, D)
```

```python
import functools

import numpy as np
import jax
import jax.numpy as jnp
from jax import lax
from jax.experimental import pallas as pl
from jax.experimental.pallas import tpu as pltpu

F32 = jnp.float32
BF16 = jnp.bfloat16

N_META = 16
FFN_HALF = 0.5
NORM_EPS = 1e-6
ROPE_THETA = 500000.0
ROPE_FRAC = 4
M_HEADS = 4
A_HEADS = 16
A_KV_HEADS = 4
IDX_HEADS = 16
IDX_DIM = 64
TOPK_MAX = 256
LOG_I_PAD = -1e30

LANES = 128
VMEM_LIMIT = 60 * 1024 * 1024
DSA_VARIANTS = 3
SEARCH_ROW_BLOCKS = 2
SEARCH_UNROLL = 4
ATT_HEADS_PER_TRIP = 4
FFN_SPLIT = 2
MLSTM_CHUNK = 256
ROW_TILE = 1024
OUT_ROW_TILE = 512
FF_TILE = 512
PROJ_COL_TILE = 1024
MLSTM_PROJ_COL_TILE = 2048
DSA_Q_TILE = 256
CAST_ROWS = 256
INT_MIN = int(np.iinfo(np.int32).min)
NEG_INF_KEY = int(np.array(-np.inf, np.float32).view(np.int32) ^ np.int32(0x7FFFFFFF))


def _params(*sem):
    return pltpu.CompilerParams(dimension_semantics=sem, vmem_limit_bytes=VMEM_LIMIT)


def _row_block(tm, d):
    return pl.BlockSpec((tm, d), lambda i, j: (i, 0))


def _rms(x, g):
    return x * lax.rsqrt(jnp.mean(x * x, axis=-1, keepdims=True) + NORM_EPS) * g


def _dot(a, b):
    return jnp.dot(a, b, preferred_element_type=F32)


def _dot_nt(a, b):
    return lax.dot_general(a, b, (((1,), (1,)), ((), ())), preferred_element_type=F32)


def _swiglu_half(xn, wg_ref, wu_ref, wd_ref):
    w = wg_ref.shape[1] // FFN_SPLIT
    acc = None
    for c in range(FFN_SPLIT):
        sl = slice(c * w, (c + 1) * w)
        gate = _dot(xn, wg_ref[:, sl])
        up = _dot(xn, wu_ref[:, sl])
        act = (gate * jax.nn.sigmoid(gate)) * (up * FFN_HALF)
        part = _dot(act.astype(BF16), wd_ref[sl, :])
        acc = part if acc is None else acc + part
    return acc


def _ffn_kernel(h_ref, g_ref, wg_ref, wu_ref, wd_ref, o_ref, xn_ref):
    j = pl.program_id(1)

    @pl.when(j == 0)
    def _():
        h = h_ref[...]
        xn = _rms(h, g_ref[...]).astype(BF16)
        xn_ref[...] = xn
        o_ref[...] = h + _swiglu_half(xn, wg_ref, wu_ref, wd_ref)

    @pl.when(j > 0)
    def _():
        o_ref[...] += _swiglu_half(xn_ref[...], wg_ref, wu_ref, wd_ref)


def _cast_kernel(x_ref, o_ref):
    o_ref[...] = x_ref[...].astype(o_ref.dtype)


def _to_bf16(w, br):
    nl, R, C = w.shape
    spec = pl.BlockSpec((None, br, C), lambda l, r: (l, r, 0))
    return pl.pallas_call(
        _cast_kernel,
        grid=(nl, R // br),
        in_specs=[spec],
        out_specs=spec,
        out_shape=jax.ShapeDtypeStruct(w.shape, BF16),
        compiler_params=_params("parallel", "parallel"),
        name="cast_bf16",
    )(w)


def _ffn(h, g, wg, wu, wd, layer, tm, tf):
    M, D = h.shape
    F = wg.shape[2]
    return pl.pallas_call(
        _ffn_kernel,
        grid=(M // tm, F // tf),
        in_specs=[
            _row_block(tm, D),
            pl.BlockSpec((1, D), lambda i, j: (0, 0)),
            pl.BlockSpec((None, D, tf), lambda i, j: (layer, 0, j)),
            pl.BlockSpec((None, D, tf), lambda i, j: (layer, 0, j)),
            pl.BlockSpec((None, tf, D), lambda i, j: (layer, j, 0)),
        ],
        out_specs=pl.BlockSpec((tm, D), lambda i, j: (i, 0)),
        out_shape=jax.ShapeDtypeStruct((M, D), F32),
        scratch_shapes=[pltpu.VMEM((tm, D), BF16)],
        compiler_params=_params("parallel", "arbitrary"),
        name="ffn",
    )(h, g.reshape(1, D), wg, wu, wd)


def _inproj_m_kernel(h_ref, g_ref, w_ref, ws_ref, p_ref, ps_ref, xn_ref):
    j = pl.program_id(1)

    @pl.when(j == 0)
    def _():
        xn = _rms(h_ref[...], g_ref[...]).astype(BF16)
        xn_ref[...] = xn
        ps_ref[...] = _dot(xn, ws_ref[...])
        p_ref[...] = _dot(xn, w_ref[...]).astype(BF16)

    @pl.when(j > 0)
    def _():
        p_ref[...] = _dot(xn_ref[...], w_ref[...]).astype(BF16)


def _inproj_m(h, g, w, ws, tm, tn):
    M, D = h.shape
    N = w.shape[1]
    return pl.pallas_call(
        _inproj_m_kernel,
        grid=(M // tm, N // tn),
        in_specs=[
            _row_block(tm, D),
            pl.BlockSpec((1, D), lambda i, j: (0, 0)),
            pl.BlockSpec((D, tn), lambda i, j: (0, j)),
            pl.BlockSpec((D, LANES), lambda i, j: (0, 0)),
        ],
        out_specs=[
            pl.BlockSpec((tm, tn), lambda i, j: (i, j)),
            pl.BlockSpec((tm, LANES), lambda i, j: (i, 0)),
        ],
        out_shape=[jax.ShapeDtypeStruct((M, N), BF16), jax.ShapeDtypeStruct((M, LANES), F32)],
        scratch_shapes=[pltpu.VMEM((tm, D), BF16)],
        compiler_params=_params("parallel", "arbitrary"),
        name="mlstm_inproj",
    )(h, g.reshape(1, D), w, ws)


def _log_sigmoid(x):
    return jnp.minimum(x, 0.0) - jnp.log1p(jnp.exp(-jnp.abs(x)))


def _mlstm_kernel(q_ref, k_ref, v_ref, o_ref, ps_ref, bias_ref, gh_ref, c0_ref, n0_ref, m0_ref,
                  y_ref, c_ref, n_ref, m_ref, *, L, npad, H, DK, DV):
    @pl.when(pl.program_id(1) == 0)
    def _():
        c_ref[...] = c0_ref[...]
        n_ref[...] = n0_ref[...]
        m_ref[...] = m0_ref[...]

    gates = ps_ref[...] + bias_ref[...]
    logf = _log_sigmoid(gates)
    if npad:
        valid = lax.broadcasted_iota(jnp.int32, (L, 1), 0) >= npad
        logf = jnp.where(valid, logf, 0.0)
        gates = jnp.where(valid, gates, LOG_I_PAD)
    ii = lax.broadcasted_iota(jnp.int32, (L, L), 0)
    jj = lax.broadcasted_iota(jnp.int32, (L, L), 1)
    causal = jj <= ii
    cum = jnp.dot(causal.astype(F32), logf, preferred_element_type=F32,
                  precision=lax.Precision.HIGHEST)
    cum_t = cum.T
    gates_t = gates.T

    updates = []
    for h in range(H):
        b_col = cum[:, H + h:H + h + 1]
        b_row = cum_t[H + h:H + h + 1, :]
        li_col = gates[:, h:h + 1]
        li_row = gates_t[h:h + 1, :]
        m_prev = m_ref[h][:, 0:1]
        q = q_ref[:, h * DK:(h + 1) * DK] * (DK ** -0.5)
        k = k_ref[:, h * DK:(h + 1) * DK]
        v = v_ref[:, h * DV:(h + 1) * DV]
        c_old = c_ref[h]
        n_old = n_ref[h]

        dm = jnp.where(causal, b_col - b_row + li_row, -jnp.inf)
        inter = b_col + m_prev
        m_t = jnp.maximum(inter, jnp.max(dm, axis=-1, keepdims=True))
        decay_mat = jnp.exp(dm - m_t)
        w_inter = jnp.exp(inter - m_t)
        b_last = b_col[L - 1:L, :]
        g_row = b_last - b_row + li_row
        g_col = b_last - b_col + li_col
        m_new = jnp.maximum(b_last + m_prev, jnp.max(g_row, axis=-1, keepdims=True))
        decay = jnp.exp(b_last + m_prev - m_new)
        updates.append((m_new, decay, jnp.exp(g_col - m_new)))

        s = _dot_nt(q, k) * decay_mat
        num = w_inter * _dot(q, c_old.astype(BF16)) + _dot(s.astype(BF16), v)
        qn = jnp.sum(q.astype(F32) * n_old, axis=-1, keepdims=True)
        den = w_inter * qn + jnp.sum(s, axis=-1, keepdims=True)
        hout = num / jnp.maximum(jnp.abs(den), jnp.exp(-m_t))

        hn = _rms(hout, gh_ref[:, h * DV:(h + 1) * DV])
        og = jax.nn.sigmoid(o_ref[:, h * DV:(h + 1) * DV].astype(F32))
        y_ref[:, h * DV:(h + 1) * DV] = (og * hn).astype(BF16)

    for h, (m_new, decay, wk) in enumerate(updates):
        k = k_ref[:, h * DK:(h + 1) * DK]
        v = v_ref[:, h * DV:(h + 1) * DV]
        kw = k.astype(F32) * wk
        c_ref[h] = decay * c_ref[h] + _dot(kw.T.astype(BF16), v)
        n_ref[h] = decay * n_ref[h] + jnp.sum(kw, axis=0, keepdims=True)
        m_ref[h] = jnp.broadcast_to(m_new, (1, LANES))


def _mlstm(p, ps, bias, gh, c0, n0, m0, B, L, npad):
    M = p.shape[0]
    H = M_HEADS
    DK, DV = c0.shape[1], c0.shape[2]
    NC = M // (B * L)
    row = lambda b, c: b * NC + c
    qk_w, vo_w = H * DK, H * DV
    assert vo_w == 2 * qk_w
    kern = functools.partial(_mlstm_kernel, L=L, npad=npad, H=H, DK=DK, DV=DV)
    return pl.pallas_call(
        kern,
        grid=(B, NC),
        in_specs=[
            pl.BlockSpec((L, qk_w), lambda b, c: (row(b, c), 0)),
            pl.BlockSpec((L, qk_w), lambda b, c: (row(b, c), 1)),
            pl.BlockSpec((L, vo_w), lambda b, c: (row(b, c), 1)),
            pl.BlockSpec((L, vo_w), lambda b, c: (row(b, c), 2)),
            pl.BlockSpec((L, LANES), lambda b, c: (row(b, c), 0)),
            pl.BlockSpec((1, LANES), lambda b, c: (0, 0)),
            pl.BlockSpec((1, vo_w), lambda b, c: (0, 0)),
            pl.BlockSpec((H, DK, DV), lambda b, c: (0, 0, 0)),
            pl.BlockSpec((H, 1, DK), lambda b, c: (0, 0, 0)),
            pl.BlockSpec((H, 1, LANES), lambda b, c: (0, 0, 0)),
        ],
        out_specs=[
            pl.BlockSpec((L, vo_w), lambda b, c: (row(b, c), 0)),
            pl.BlockSpec((None, H, DK, DV), lambda b, c: (b, 0, 0, 0)),
            pl.BlockSpec((None, H, 1, DK), lambda b, c: (b, 0, 0, 0)),
            pl.BlockSpec((None, H, 1, LANES), lambda b, c: (b, 0, 0, 0)),
        ],
        out_shape=[
            jax.ShapeDtypeStruct((M, vo_w), BF16),
            jax.ShapeDtypeStruct((B, H, DK, DV), F32),
            jax.ShapeDtypeStruct((B, H, 1, DK), F32),
            jax.ShapeDtypeStruct((B, H, 1, LANES), F32),
        ],
        compiler_params=_params("parallel", "arbitrary"),
        name="mlstm",
    )(p, p, p, p, ps, bias, gh, c0, n0, m0)


def _outproj_kernel(y_ref, w_ref, h_ref, o_ref):
    o_ref[...] = h_ref[...] + _dot(y_ref[...], w_ref[...])


def _outproj(y, w, h, tm):
    M, D = h.shape
    K = y.shape[1]
    return pl.pallas_call(
        _outproj_kernel,
        grid=(M // tm,),
        in_specs=[
            pl.BlockSpec((tm, K), lambda i: (i, 0)),
            pl.BlockSpec((K, D), lambda i: (0, 0)),
            pl.BlockSpec((tm, D), lambda i: (i, 0)),
        ],
        out_specs=pl.BlockSpec((tm, D), lambda i: (i, 0)),
        out_shape=jax.ShapeDtypeStruct((M, D), F32),
        compiler_params=_params("parallel"),
        name="outproj",
    )(y, w, h)


def _rope(x, tab, half):
    c = tab[:, 0:LANES]
    s1 = tab[:, LANES:2 * LANES]
    s2 = tab[:, 2 * LANES:3 * LANES]
    return x * c + pltpu.roll(x, LANES - half, 1) * s1 + pltpu.roll(x, half, 1) * s2


def _inproj_a_kernel(h_ref, g_ref, w_ref, ws_ref, gq_ref, gk_ref, t128_ref, t64_ref, tki_ref,
                     p_ref, pqi_ref, ps_ref, xn_ref, acc_ref, *, nq_tiles, tn, half128, half64):
    j = pl.program_id(1)
    nh = tn // LANES
    n_tiles = nq_tiles + 2

    def epilogue(tile, acc, p_ref):
        if tile < nq_tiles:
            t = t128_ref[...]
            for c in range(nh):
                xs = _rms(acc[:, c * LANES:(c + 1) * LANES], gq_ref[...])
                p_ref[:, c * LANES:(c + 1) * LANES] = _rope(xs, t, half128).astype(BF16)
        elif tile == nq_tiles:
            t = t128_ref[...]
            for c in range(nh // 2):
                xs = _rms(acc[:, c * LANES:(c + 1) * LANES], gk_ref[...])
                p_ref[:, c * LANES:(c + 1) * LANES] = _rope(xs, t, half128).astype(BF16)
            p_ref[:, tn // 2:] = acc[:, tn // 2:].astype(BF16)
        else:
            t = t64_ref[...]
            for c in range(nh):
                p_ref[:, c * LANES:(c + 1) * LANES] = _rope(acc[:, c * LANES:(c + 1) * LANES], t, half64).astype(BF16)

    for s in range(n_tiles):
        @pl.when(j == s)
        def _(s=s):
            if s == 0:
                xn = _rms(h_ref[...], g_ref[...]).astype(BF16)
                xn_ref[...] = xn
                ps_ref[...] = _rope(_dot(xn, ws_ref[...]), tki_ref[...], half64)
                acc_ref[0] = _dot(xn, w_ref[...])
            else:
                acc_ref[s % 2] = _dot(xn_ref[...], w_ref[...])
                epilogue(s - 1, acc_ref[(s - 1) % 2], p_ref)
            if s == n_tiles - 1:
                epilogue(s, acc_ref[s % 2], pqi_ref)


def _inproj_a(h, g, w, ws, gq, gk, t128, t64, tki, tm):
    M, D = h.shape
    N = w.shape[1]
    tn = PROJ_COL_TILE
    hd = D // A_HEADS
    assert hd == LANES and D // 4 * 2 == tn and N == D + 2 * tn
    nt = t128.shape[0] // tm
    n_tiles = N // tn
    kern = functools.partial(_inproj_a_kernel, nq_tiles=D // tn, tn=tn,
                             half128=hd // ROPE_FRAC // 2, half64=IDX_DIM // ROPE_FRAC // 2)
    tab = pl.BlockSpec((tm, 3 * LANES), lambda i, j: (i % nt, 0))
    return pl.pallas_call(
        kern,
        grid=(M // tm, n_tiles),
        in_specs=[
            _row_block(tm, D),
            pl.BlockSpec((1, D), lambda i, j: (0, 0)),
            pl.BlockSpec((D, tn), lambda i, j: (0, j)),
            pl.BlockSpec((D, LANES), lambda i, j: (0, 0)),
            pl.BlockSpec((1, LANES), lambda i, j: (0, 0)),
            pl.BlockSpec((1, LANES), lambda i, j: (0, 0)),
            tab, tab, tab,
        ],
        out_specs=[
            pl.BlockSpec((tm, tn), lambda i, j: (i, jnp.maximum(j - 1, 0))),
            pl.BlockSpec((tm, tn), lambda i, j: (i, 0)),
            pl.BlockSpec((tm, LANES), lambda i, j: (i, 0)),
        ],
        out_shape=[jax.ShapeDtypeStruct((M, N - tn), BF16), jax.ShapeDtypeStruct((M, tn), BF16),
                   jax.ShapeDtypeStruct((M, LANES), F32)],
        scratch_shapes=[pltpu.VMEM((tm, D), BF16), pltpu.VMEM((2, tm, tn), F32)],
        compiler_params=_params("parallel", "arbitrary"),
        name="dsa_inproj",
    )(h, g.reshape(1, D), w, ws, gq, gk, t128, t64, tki)


def _rope_table(pos, d, extra_scale_lanes=None, extra_scale=1.0, reps=None):
    rot = d // ROPE_FRAC
    half = rot // 2
    inv = 1.0 / (ROPE_THETA ** (jnp.arange(0, rot, 2, dtype=F32) / rot))
    ang = pos[:, None] * inv[None, :]
    cos, sin = jnp.cos(ang), jnp.sin(ang)
    T = pos.shape[0]
    one = jnp.ones((T, d - rot), F32)
    zero = lambda n: jnp.zeros((T, n), F32)
    c = jnp.concatenate([cos, cos, one], axis=1)
    s1 = jnp.concatenate([-sin, zero(d - half)], axis=1)
    s2 = jnp.concatenate([zero(half), sin, zero(d - rot)], axis=1)
    if reps is None:
        reps = LANES // d
    c, s1, s2 = (jnp.tile(a, (1, reps)) for a in (c, s1, s2))
    fill = LANES - reps * d
    if fill:
        tail = jnp.ones((T, fill), F32)
        if extra_scale_lanes is not None:
            lo, hi = extra_scale_lanes
            lane = jnp.arange(reps * d, LANES)
            tail = jnp.where((lane >= lo) & (lane < hi), extra_scale, 1.0)[None, :] * tail
        c = jnp.concatenate([c, tail], axis=1)
        s1 = jnp.concatenate([s1, zero(fill)], axis=1)
        s2 = jnp.concatenate([s2, zero(fill)], axis=1)
    return jnp.concatenate([c, s1, s2], axis=1)


def _float_key(x):
    bits = lax.bitcast_convert_type(x, jnp.int32)
    return jnp.where(bits < 0, bits ^ jnp.int32(0x7FFFFFFF), bits)


def _dsa_keys(qi_ref, psq_ref, kie, kio, key_ref, *, tq, row0, nk, n_meta):
    wi = psq_ref[...]
    ke = kie[0:nk, :]
    ko = kio[0:nk, :]
    score = jnp.zeros((tq, nk), F32)
    for p in range(IDX_HEADS // 2):
        qp = qi_ref[:, p * LANES:(p + 1) * LANES]
        c0 = IDX_DIM + 2 * p
        score += jnp.maximum(_dot_nt(qp, ke), 0.0) * wi[:, c0:c0 + 1]
        score += jnp.maximum(_dot_nt(qp, ko), 0.0) * wi[:, c0 + 1:c0 + 2]
    col = lax.broadcasted_iota(jnp.int32, (tq, nk), 1)
    row = lax.broadcasted_iota(jnp.int32, (tq, nk), 0) + row0
    adm = jnp.where(col < LANES, col - n_meta, col - LANES - row - 1) < 0
    score = jnp.where(score == 0.0, 0.0, score)
    key_ref[:, 0:nk] = _float_key(jnp.where(adm, score, -jnp.inf))


def _count_ge(key_ref, cand, r0, rows, nch):
    acc = jnp.zeros((rows, LANES), F32)
    for c in range(nch):
        acc += jnp.where(key_ref[r0:r0 + rows, c * LANES:(c + 1) * LANES] >= cand, 1.0, 0.0)
    return jnp.sum(acc, axis=-1, keepdims=True)


def _search_trip(key_ref, trip, bases, *, tq, nk, topk):
    rb = tq // SEARCH_ROW_BLOCKS
    bases = list(bases)
    for u in range(SEARCH_UNROLL):
        bit = jnp.int32(31) - (trip.astype(jnp.int32) * SEARCH_UNROLL + u)
        step = lax.shift_left(jnp.int32(1), bit)
        for blk in range(SEARCH_ROW_BLOCKS):
            cand = bases[blk] + step
            cnt = _count_ge(key_ref, cand, blk * rb, rb, nk // LANES)
            bases[blk] = jnp.where(cnt >= float(topk), cand, bases[blk])
    return tuple(bases)


def _search_init(tq):
    return tuple(jnp.full((tq // SEARCH_ROW_BLOCKS, 1), INT_MIN, jnp.int32) for _ in range(SEARCH_ROW_BLOCKS))


def _dsa_mask(key_ref, bias_ref, bases, *, tq, nk, topk):
    NCH = nk // LANES
    kf = float(topk)
    ones = jnp.ones((LANES, LANES), BF16)
    thr = jnp.maximum(jnp.concatenate(bases, axis=0), NEG_INF_KEY + 1)
    cnt = _count_ge(key_ref, thr, 0, tq, NCH)
    for c in range(NCH):
        sl = slice(c * LANES, (c + 1) * LANES)
        bias_ref[:, sl] = jnp.where(key_ref[:, sl] >= thr, 0.0, -jnp.inf)

    @pl.when(jnp.max(cnt) > kf)
    def _():
        need = kf - _count_ge(key_ref, thr + 1, 0, tq, NCH)
        rr = lax.broadcasted_iota(jnp.int32, (LANES, LANES), 0)
        cc = lax.broadcasted_iota(jnp.int32, (LANES, LANES), 1)
        tri = jnp.where(rr <= cc, 1.0, 0.0).astype(BF16)
        run = jnp.zeros((tq, LANES), F32)
        for c in range(NCH):
            sl = slice(c * LANES, (c + 1) * LANES)
            kc = key_ref[:, sl]
            eq = kc == thr
            eqb = jnp.where(eq, 1.0, 0.0).astype(BF16)
            rank = run + _dot(eqb, tri)
            keep_eq = jnp.where(rank <= need, 0.0, -jnp.inf)
            bias_ref[:, sl] = jnp.where(kc > thr, 0.0, jnp.where(eq, keep_eq, -jnp.inf))
            run = run + _dot(eqb, ones)


def _attn_heads(qs, kcat, vcat, bias_ref, os, trip, *, nk):
    hd = LANES
    per_group = (A_HEADS // A_KV_HEADS) // ATT_HEADS_PER_TRIP
    g = trip // per_group
    kg = kcat[g, 0:nk, :]
    vg = vcat[g, 0:nk, :]
    heads = [trip * ATT_HEADS_PER_TRIP + r for r in range(ATT_HEADS_PER_TRIP)]
    probs = []
    for h in heads:
        logits = _dot_nt(qs[h], kg) + bias_ref[:, 0:nk]
        mx = jnp.max(logits, axis=-1, keepdims=True)
        probs.append(jnp.exp2(logits - mx).astype(BF16))
    for h, p in zip(heads, probs):
        pv = _dot(p, vg)
        os[h] = (pv[:, 0:hd] / pv[:, hd:hd + 1]).astype(BF16)


def _dsa_step(i, q_ref, qi_ref, psq_ref, o_ref, kcat, vcat, kie, kio, key_ref,
              bias_ref, qs, os, *, tq, nk, topk, n_meta):
    hd = LANES
    _dsa_keys(qi_ref, psq_ref, kie, kio, key_ref, tq=tq, row0=i * tq, nk=nk, n_meta=n_meta)
    bases = lax.fori_loop(0, 32 // SEARCH_UNROLL,
                          lambda t, b: _search_trip(key_ref, t, b, tq=tq, nk=nk, topk=topk), _search_init(tq))
    _dsa_mask(key_ref, bias_ref, bases, tq=tq, nk=nk, topk=topk)

    for h in range(A_HEADS):
        qs[h] = q_ref[:, h * hd:(h + 1) * hd]

    def trip(t, carry):
        _attn_heads(qs, kcat, vcat, bias_ref, os, t, nk=nk)
        return carry

    lax.fori_loop(0, A_HEADS // ATT_HEADS_PER_TRIP, trip, 0)
    for h in range(A_HEADS):
        o_ref[:, h * hd:(h + 1) * hd] = os[h]


def _dsa_kernel(q_ref, qi_ref, psq_ref, k_ref, v_ref, psk_ref, km_ref, vm_ref, psm_ref, o_ref,
                kcat, vcat, kie, kio, key_ref, bias_ref, qs, os, *, tq, S, topk, n_meta):
    i = pl.program_id(1)
    NK = LANES + S
    hd = LANES
    G = A_KV_HEADS
    nq = S // tq

    @pl.when(i == 0)
    def _():
        lane = lax.broadcasted_iota(jnp.int32, (NK, hd), 1)
        ones_col = jnp.where(lane == 0, 1.0, 0.0).astype(BF16)
        for g in range(G):
            sl = slice(g * hd, (g + 1) * hd)
            kcat[g, 0:LANES, :] = km_ref[:, sl]
            kcat[g, LANES:, :] = k_ref[:, sl]
            vcat[g, 0:LANES, 0:hd] = vm_ref[:, sl]
            vcat[g, LANES:, 0:hd] = v_ref[:, sl]
            vcat[g, :, hd:] = ones_col
        for dst0, dst1, src in ((0, LANES, psm_ref), (LANES, NK, psk_ref)):
            a = src[...]
            lane = lax.broadcasted_iota(jnp.int32, a.shape, 1)
            even = jnp.where(lane < IDX_DIM, a, 0.0)
            kie[dst0:dst1, :] = even.astype(BF16)
            kio[dst0:dst1, :] = pltpu.roll(even, IDX_DIM, 1).astype(BF16)

    bounds = sorted({-(-nq * (v + 1) // DSA_VARIANTS) for v in range(DSA_VARIANTS)})
    lo = 0
    for hi in bounds:
        pl.when((i >= lo) & (i < hi))(functools.partial(
            _dsa_step, i, q_ref, qi_ref, psq_ref, o_ref, kcat, vcat, kie, kio,
            key_ref, bias_ref, qs, os, tq=tq, nk=LANES + hi * tq, topk=topk, n_meta=n_meta))
        lo = hi


def _dsa(p, pqi, ps, pm, psm, B, S, tq, topk):
    M, N = p.shape
    D = N * 2 // 3
    kv_w = D // 4
    nq = S // tq
    NK = LANES + S
    qi_w = IDX_HEADS * IDX_DIM
    kern = functools.partial(_dsa_kernel, tq=tq, S=S, topk=topk, n_meta=N_META)
    return pl.pallas_call(
        kern,
        grid=(B, nq),
        in_specs=[
            pl.BlockSpec((tq, D), lambda b, i: (b * nq + i, 0)),
            pl.BlockSpec((tq, qi_w), lambda b, i: (b * nq + i, 0)),
            pl.BlockSpec((tq, LANES), lambda b, i: (b * nq + i, 0)),
            pl.BlockSpec((S, kv_w), lambda b, i: (b, D // kv_w)),
            pl.BlockSpec((S, kv_w), lambda b, i: (b, D // kv_w + 1)),
            pl.BlockSpec((S, LANES), lambda b, i: (b, 0)),
            pl.BlockSpec((LANES, kv_w), lambda b, i: (0, D // kv_w)),
            pl.BlockSpec((LANES, kv_w), lambda b, i: (0, D // kv_w + 1)),
            pl.BlockSpec((LANES, LANES), lambda b, i: (0, 0)),
        ],
        out_specs=pl.BlockSpec((tq, D), lambda b, i: (b * nq + i, 0)),
        out_shape=jax.ShapeDtypeStruct((M, D), BF16),
        scratch_shapes=[
            pltpu.VMEM((A_KV_HEADS, NK, LANES), BF16),
            pltpu.VMEM((A_KV_HEADS, NK, 2 * LANES), BF16),
            pltpu.VMEM((NK, LANES), BF16),
            pltpu.VMEM((NK, LANES), BF16),
            pltpu.VMEM((tq, NK), jnp.int32),
            pltpu.VMEM((tq, NK), F32),
            pltpu.VMEM((A_HEADS, tq, LANES), BF16),
            pltpu.VMEM((A_HEADS, tq, LANES), BF16),
        ],
        compiler_params=_params("parallel", "arbitrary"),
        name="dsa_attention",
    )(p, pqi, ps, p, p, ps, pm, pm, psm)


def _row_tile(m, pref):
    return pref if m % pref == 0 else m


def kernel(x, meta_tokens, ffn1_norm, ffn1_w_gate, ffn1_w_up, ffn1_w_down, mix_norm, ffn2_norm, ffn2_w_gate, ffn2_w_up, ffn2_w_down, mlstm_w_in, mlstm_b_i, mlstm_b_f, mlstm_head_norm, mlstm_w_out, dsa_w_in, dsa_q_norm, dsa_k_norm, dsa_w_out):
    B, S, D = x.shape
    depth = ffn1_norm.shape[0]
    assert depth == 2 and meta_tokens.shape == (N_META, D)
    F = ffn1_w_gate.shape[-1]
    H, DK, DV = M_HEADS, D // (2 * M_HEADS), D // M_HEADS
    hd = D // A_HEADS
    L, TM, TMO, TF = MLSTM_CHUNK, ROW_TILE, OUT_ROW_TILE, FF_TILE
    assert S % L == 0 and S % TM == 0 and (B * S) % TMO == 0 and F % TF == 0

    hx = x.reshape(B * S, D)
    hm = meta_tokens.astype(x.dtype)
    bf = lambda w: w.astype(BF16)

    f1 = (_to_bf16(ffn1_w_gate, CAST_ROWS), _to_bf16(ffn1_w_up, CAST_ROWS), _to_bf16(ffn1_w_down, CAST_ROWS))
    f2 = (_to_bf16(ffn2_w_gate, CAST_ROWS), _to_bf16(ffn2_w_up, CAST_ROWS), _to_bf16(ffn2_w_down, CAST_ROWS))

    def ffn(h, g, ws, layer):
        return _ffn(h, g, *ws, layer, _row_tile(h.shape[0], TM), TF)

    def pad_cols(w):
        return jnp.pad(w, ((0, 0), (0, LANES - w.shape[1])))

    def pad_rows(a, n, front=False):
        r = n - a.shape[0]
        return jnp.pad(a, ((r, 0) if front else (0, r), (0, 0)))

    hx = ffn(hx, ffn1_norm[0], f1, 0)
    hm = ffn(hm, ffn1_norm[0], f1, 0)

    n_wide = 2 * H * DK + 2 * H * DV
    w_wide = bf(mlstm_w_in[0][:, :n_wide])
    w_gate = bf(pad_cols(mlstm_w_in[0][:, n_wide:]))
    px, psx = _inproj_m(hx, mix_norm[0], w_wide, w_gate, TM, MLSTM_PROJ_COL_TILE)
    pm, psm = _inproj_m(hm, mix_norm[0], w_wide, w_gate, N_META, MLSTM_PROJ_COL_TILE)

    bias = pad_cols(jnp.concatenate([mlstm_b_i[0], mlstm_b_f[0]]).astype(F32)[None, :])
    gh = mlstm_head_norm[0].astype(F32).reshape(1, H * DV)
    zc = jnp.zeros((H, DK, DV), F32)
    zn = jnp.zeros((H, 1, DK), F32)
    zm = jnp.zeros((H, 1, LANES), F32)
    ym, c0, n0, m0 = _mlstm(pad_rows(pm, L, True), pad_rows(psm, L, True), bias, gh, zc, zn, zm,
                            1, L, L - N_META)
    yx, _, _, _ = _mlstm(px, psx, bias, gh, c0[0], n0[0], m0[0], B, L, 0)
    w_out = bf(mlstm_w_out[0])
    hx = _outproj(yx, w_out, hx, TMO)
    hm = _outproj(ym[L - N_META:], w_out, hm, N_META)

    hx = ffn(hx, ffn2_norm[0], f2, 0)
    hm = ffn(hm, ffn2_norm[0], f2, 0)

    hx = ffn(hx, ffn1_norm[1], f1, 1)
    hm = ffn(hm, ffn1_norm[1], f1, 1)

    n_wide = A_HEADS * hd + 2 * A_KV_HEADS * hd + IDX_HEADS * IDX_DIM
    w_wide = bf(dsa_w_in[0][:, :n_wide])
    w_idx = bf(pad_cols(dsa_w_in[0][:, n_wide:]))
    gq = (dsa_q_norm[0].astype(F32) * (hd ** -0.5 * float(np.log2(np.e))))[None, :]
    gk = dsa_k_norm[0].astype(F32)[None, :]
    wi_scale = IDX_HEADS ** -0.5 * IDX_DIM ** -0.5
    pos = jnp.arange(N_META + S, dtype=F32)
    t128 = _rope_table(pos, hd)
    t64 = _rope_table(pos, IDX_DIM)
    tki = _rope_table(pos, IDX_DIM, (IDX_DIM, IDX_DIM + IDX_HEADS), wi_scale, reps=1)
    tabs_x = [t[N_META:] for t in (t128, t64, tki)]
    tabs_m = [t[:N_META] for t in (t128, t64, tki)]
    px, pqx, psx = _inproj_a(hx, mix_norm[1], w_wide, w_idx, gq, gk, *tabs_x, TM)
    pm, _, psm = _inproj_a(hm, mix_norm[1], w_wide, w_idx, gq, gk, *tabs_m, N_META)

    topk = min(TOPK_MAX, (N_META + S - N_META) // 4)
    ox = _dsa(px, pqx, psx, pad_rows(pm, LANES), pad_rows(psm, LANES), B, S, DSA_Q_TILE, topk)
    hx = _outproj(ox, bf(dsa_w_out[0]), hx, TMO)

    hx = ffn(hx, ffn2_norm[1], f2, 1)
    return hx.reshape(B, S, D)
```

```python
import functools

import numpy as np
import jax
import jax.numpy as jnp
from jax import lax
from jax.experimental import pallas as pl
from jax.experimental.pallas import tpu as pltpu

F32 = jnp.float32
BF16 = jnp.bfloat16

N_META = 16
FFN_HALF = 0.5
NORM_EPS = 1e-6
ROPE_THETA = 500000.0
ROPE_FRAC = 4
M_HEADS = 4
A_HEADS = 16
A_KV_HEADS = 4
IDX_HEADS = 16
IDX_DIM = 64
TOPK_MAX = 256
LOG_I_PAD = -1e30

LANES = 128
VMEM_LIMIT = 60 * 1024 * 1024
DSA_VARIANTS = 3
SEARCH_ROW_BLOCKS = 2
SEARCH_UNROLL = 4
ATT_HEADS_PER_TRIP = 4
FFN_SPLIT = 2
MLSTM_CHUNK = 256
ROW_TILE = 1024
OUT_ROW_TILE = 512
FF_TILE = 512
PROJ_COL_TILE = 1024
MLSTM_PROJ_COL_TILE = 2048
DSA_Q_TILE = 256
INT_MIN = int(np.iinfo(np.int32).min)
NEG_INF_KEY = int(np.array(-np.inf, np.float32).view(np.int32) ^ np.int32(0x7FFFFFFF))


def _params(*sem):
    return pltpu.CompilerParams(dimension_semantics=sem, vmem_limit_bytes=VMEM_LIMIT)


def _row_block(tm, d):
    return pl.BlockSpec((tm, d), lambda i, j: (i, 0))


def _rms(x, g):
    return x * lax.rsqrt(jnp.mean(x * x, axis=-1, keepdims=True) + NORM_EPS) * g


def _dot(a, b):
    return jnp.dot(a, b, preferred_element_type=F32)


def _dot_nt(a, b):
    return lax.dot_general(a, b, (((1,), (1,)), ((), ())), preferred_element_type=F32)


def _swiglu_half(xn, wg_ref, wu_ref, wd_ref):
    w = wg_ref.shape[1] // FFN_SPLIT
    acc = None
    for c in range(FFN_SPLIT):
        sl = slice(c * w, (c + 1) * w)
        gate = _dot(xn, wg_ref[:, sl])
        up = _dot(xn, wu_ref[:, sl])
        act = (gate * jax.nn.sigmoid(gate)) * (up * FFN_HALF)
        part = _dot(act.astype(BF16), wd_ref[sl, :])
        acc = part if acc is None else acc + part
    return acc


def _ffn_kernel(h_ref, g_ref, wg_ref, wu_ref, wd_ref, o_ref, xn_ref):
    j = pl.program_id(1)

    @pl.when(j == 0)
    def _():
        h = h_ref[...]
        xn = _rms(h, g_ref[...]).astype(BF16)
        xn_ref[...] = xn
        o_ref[...] = h + _swiglu_half(xn, wg_ref, wu_ref, wd_ref)

    @pl.when(j > 0)
    def _():
        o_ref[...] += _swiglu_half(xn_ref[...], wg_ref, wu_ref, wd_ref)


def _ffn(h, g, wg, wu, wd, layer, tm, tf):
    M, D = h.shape
    F = wg.shape[2]
    return pl.pallas_call(
        _ffn_kernel,
        grid=(M // tm, F // tf),
        in_specs=[
            _row_block(tm, D),
            pl.BlockSpec((1, D), lambda i, j: (0, 0)),
            pl.BlockSpec((None, D, tf), lambda i, j: (layer, 0, j)),
            pl.BlockSpec((None, D, tf), lambda i, j: (layer, 0, j)),
            pl.BlockSpec((None, tf, D), lambda i, j: (layer, j, 0)),
        ],
        out_specs=pl.BlockSpec((tm, D), lambda i, j: (i, 0)),
        out_shape=jax.ShapeDtypeStruct((M, D), F32),
        scratch_shapes=[pltpu.VMEM((tm, D), BF16)],
        compiler_params=_params("parallel", "arbitrary"),
        name="ffn",
    )(h, g.reshape(1, D), wg, wu, wd)


def _ffn_cast_kernel(h_ref, g_ref, wg_ref, wu_ref, wd_ref, o_ref, wgb_ref, wub_ref, wdb_ref, xn_ref):
    wgb_ref[...] = wg_ref[...].astype(BF16)
    wub_ref[...] = wu_ref[...].astype(BF16)
    wdb_ref[...] = wd_ref[...].astype(BF16)

    @pl.when(pl.program_id(0) == 0)
    def _():
        h = h_ref[...]
        xn_ref[...] = _rms(h, g_ref[...]).astype(BF16)
        o_ref[...] = h

    o_ref[...] += _swiglu_half(xn_ref[...], wgb_ref, wub_ref, wdb_ref)


def _ffn_cast(h, g, wg, wu, wd, layer, tf):
    M, D = h.shape
    F = wg.shape[2]
    return pl.pallas_call(
        _ffn_cast_kernel,
        grid=(F // tf,),
        in_specs=[
            pl.BlockSpec((M, D), lambda j: (0, 0)),
            pl.BlockSpec((1, D), lambda j: (0, 0)),
            pl.BlockSpec((None, D, tf), lambda j: (layer, 0, j)),
            pl.BlockSpec((None, D, tf), lambda j: (layer, 0, j)),
            pl.BlockSpec((None, tf, D), lambda j: (layer, j, 0)),
        ],
        out_specs=[
            pl.BlockSpec((M, D), lambda j: (0, 0)),
            pl.BlockSpec((D, tf), lambda j: (0, j)),
            pl.BlockSpec((D, tf), lambda j: (0, j)),
            pl.BlockSpec((tf, D), lambda j: (j, 0)),
        ],
        out_shape=[
            jax.ShapeDtypeStruct((M, D), F32),
            jax.ShapeDtypeStruct((D, F), BF16),
            jax.ShapeDtypeStruct((D, F), BF16),
            jax.ShapeDtypeStruct((F, D), BF16),
        ],
        scratch_shapes=[pltpu.VMEM((M, D), BF16)],
        compiler_params=_params("arbitrary"),
        name="ffn_meta_cast",
    )(h, g.reshape(1, D), wg, wu, wd)


def _inproj_m_kernel(h_ref, g_ref, w_ref, ws_ref, p_ref, ps_ref, xn_ref):
    j = pl.program_id(1)

    @pl.when(j == 0)
    def _():
        xn = _rms(h_ref[...], g_ref[...]).astype(BF16)
        xn_ref[...] = xn
        ps_ref[...] = _dot(xn, ws_ref[...])
        p_ref[...] = _dot(xn, w_ref[...]).astype(BF16)

    @pl.when(j > 0)
    def _():
        p_ref[...] = _dot(xn_ref[...], w_ref[...]).astype(BF16)


def _inproj_m(h, g, w, ws, tm, tn):
    M, D = h.shape
    N = w.shape[1]
    return pl.pallas_call(
        _inproj_m_kernel,
        grid=(M // tm, N // tn),
        in_specs=[
            _row_block(tm, D),
            pl.BlockSpec((1, D), lambda i, j: (0, 0)),
            pl.BlockSpec((D, tn), lambda i, j: (0, j)),
            pl.BlockSpec((D, LANES), lambda i, j: (0, 0)),
        ],
        out_specs=[
            pl.BlockSpec((tm, tn), lambda i, j: (i, j)),
            pl.BlockSpec((tm, LANES), lambda i, j: (i, 0)),
        ],
        out_shape=[jax.ShapeDtypeStruct((M, N), BF16), jax.ShapeDtypeStruct((M, LANES), F32)],
        scratch_shapes=[pltpu.VMEM((tm, D), BF16)],
        compiler_params=_params("parallel", "arbitrary"),
        name="mlstm_inproj",
    )(h, g.reshape(1, D), w, ws)


def _log_sigmoid(x):
    return jnp.minimum(x, 0.0) - jnp.log1p(jnp.exp(-jnp.abs(x)))


def _mlstm_kernel(q_ref, k_ref, v_ref, o_ref, ps_ref, bias_ref, gh_ref, c0_ref, n0_ref, m0_ref,
                  y_ref, c_ref, n_ref, m_ref, *, L, npad, H, DK, DV):
    @pl.when(pl.program_id(1) == 0)
    def _():
        c_ref[...] = c0_ref[...]
        n_ref[...] = n0_ref[...]
        m_ref[...] = m0_ref[...]

    gates = ps_ref[...] + bias_ref[...]
    logf = _log_sigmoid(gates)
    if npad:
        valid = lax.broadcasted_iota(jnp.int32, (L, 1), 0) >= npad
        logf = jnp.where(valid, logf, 0.0)
        gates = jnp.where(valid, gates, LOG_I_PAD)
    ii = lax.broadcasted_iota(jnp.int32, (L, L), 0)
    jj = lax.broadcasted_iota(jnp.int32, (L, L), 1)
    causal = jj <= ii
    cum = jnp.dot(causal.astype(F32), logf, preferred_element_type=F32,
                  precision=lax.Precision.HIGHEST)
    cum_t = cum.T
    gates_t = gates.T

    updates = []
    for h in range(H):
        b_col = cum[:, H + h:H + h + 1]
        b_row = cum_t[H + h:H + h + 1, :]
        li_col = gates[:, h:h + 1]
        li_row = gates_t[h:h + 1, :]
        m_prev = m_ref[h][:, 0:1]
        q = q_ref[:, h * DK:(h + 1) * DK] * (DK ** -0.5)
        k = k_ref[:, h * DK:(h + 1) * DK]
        v = v_ref[:, h * DV:(h + 1) * DV]
        c_old = c_ref[h]
        n_old = n_ref[h]

        dm = jnp.where(causal, b_col - b_row + li_row, -jnp.inf)
        inter = b_col + m_prev
        m_t = jnp.maximum(inter, jnp.max(dm, axis=-1, keepdims=True))
        decay_mat = jnp.exp(dm - m_t)
        w_inter = jnp.exp(inter - m_t)
        b_last = b_col[L - 1:L, :]
        g_row = b_last - b_row + li_row
        g_col = b_last - b_col + li_col
        m_new = jnp.maximum(b_last + m_prev, jnp.max(g_row, axis=-1, keepdims=True))
        decay = jnp.exp(b_last + m_prev - m_new)
        updates.append((m_new, decay, jnp.exp(g_col - m_new)))

        s = _dot_nt(q, k) * decay_mat
        num = w_inter * _dot(q, c_old.astype(BF16)) + _dot(s.astype(BF16), v)
        qn = jnp.sum(q.astype(F32) * n_old, axis=-1, keepdims=True)
        den = w_inter * qn + jnp.sum(s, axis=-1, keepdims=True)
        hout = num / jnp.maximum(jnp.abs(den), jnp.exp(-m_t))

        hn = _rms(hout, gh_ref[:, h * DV:(h + 1) * DV])
        og = jax.nn.sigmoid(o_ref[:, h * DV:(h + 1) * DV].astype(F32))
        y_ref[:, h * DV:(h + 1) * DV] = (og * hn).astype(BF16)

    for h, (m_new, decay, wk) in enumerate(updates):
        k = k_ref[:, h * DK:(h + 1) * DK]
        v = v_ref[:, h * DV:(h + 1) * DV]
        kw = k.astype(F32) * wk
        c_ref[h] = decay * c_ref[h] + _dot(kw.T.astype(BF16), v)
        n_ref[h] = decay * n_ref[h] + jnp.sum(kw, axis=0, keepdims=True)
        m_ref[h] = jnp.broadcast_to(m_new, (1, LANES))


def _mlstm(p, ps, bias, gh, c0, n0, m0, B, L, npad):
    M = p.shape[0]
    H = M_HEADS
    DK, DV = c0.shape[1], c0.shape[2]
    NC = M // (B * L)
    row = lambda b, c: b * NC + c
    qk_w, vo_w = H * DK, H * DV
    assert vo_w == 2 * qk_w
    kern = functools.partial(_mlstm_kernel, L=L, npad=npad, H=H, DK=DK, DV=DV)
    return pl.pallas_call(
        kern,
        grid=(B, NC),
        in_specs=[
            pl.BlockSpec((L, qk_w), lambda b, c: (row(b, c), 0)),
            pl.BlockSpec((L, qk_w), lambda b, c: (row(b, c), 1)),
            pl.BlockSpec((L, vo_w), lambda b, c: (row(b, c), 1)),
            pl.BlockSpec((L, vo_w), lambda b, c: (row(b, c), 2)),
            pl.BlockSpec((L, LANES), lambda b, c: (row(b, c), 0)),
            pl.BlockSpec((1, LANES), lambda b, c: (0, 0)),
            pl.BlockSpec((1, vo_w), lambda b, c: (0, 0)),
            pl.BlockSpec((H, DK, DV), lambda b, c: (0, 0, 0)),
            pl.BlockSpec((H, 1, DK), lambda b, c: (0, 0, 0)),
            pl.BlockSpec((H, 1, LANES), lambda b, c: (0, 0, 0)),
        ],
        out_specs=[
            pl.BlockSpec((L, vo_w), lambda b, c: (row(b, c), 0)),
            pl.BlockSpec((None, H, DK, DV), lambda b, c: (b, 0, 0, 0)),
            pl.BlockSpec((None, H, 1, DK), lambda b, c: (b, 0, 0, 0)),
            pl.BlockSpec((None, H, 1, LANES), lambda b, c: (b, 0, 0, 0)),
        ],
        out_shape=[
            jax.ShapeDtypeStruct((M, vo_w), BF16),
            jax.ShapeDtypeStruct((B, H, DK, DV), F32),
            jax.ShapeDtypeStruct((B, H, 1, DK), F32),
            jax.ShapeDtypeStruct((B, H, 1, LANES), F32),
        ],
        compiler_params=_params("parallel", "arbitrary"),
        name="mlstm",
    )(p, p, p, p, ps, bias, gh, c0, n0, m0)


def _outproj_kernel(y_ref, w_ref, h_ref, o_ref):
    o_ref[...] = h_ref[...] + _dot(y_ref[...], w_ref[...])


def _outproj(y, w, h, tm):
    M, D = h.shape
    K = y.shape[1]
    return pl.pallas_call(
        _outproj_kernel,
        grid=(M // tm,),
        in_specs=[
            pl.BlockSpec((tm, K), lambda i: (i, 0)),
            pl.BlockSpec((K, D), lambda i: (0, 0)),
            pl.BlockSpec((tm, D), lambda i: (i, 0)),
        ],
        out_specs=pl.BlockSpec((tm, D), lambda i: (i, 0)),
        out_shape=jax.ShapeDtypeStruct((M, D), F32),
        compiler_params=_params("parallel"),
        name="outproj",
    )(y, w, h)


def _rope(x, tab, half):
    c = tab[:, 0:LANES]
    s1 = tab[:, LANES:2 * LANES]
    s2 = tab[:, 2 * LANES:3 * LANES]
    return x * c + pltpu.roll(x, LANES - half, 1) * s1 + pltpu.roll(x, half, 1) * s2


def _inproj_a_kernel(h_ref, g_ref, w_ref, ws_ref, gq_ref, gk_ref, t128_ref, t64_ref, tki_ref,
                     p_ref, pqi_ref, ps_ref, xn_ref, acc_ref, *, nq_tiles, tn, half128, half64):
    j = pl.program_id(1)
    nh = tn // LANES
    n_tiles = nq_tiles + 2

    def epilogue(tile, acc, p_ref):
        if tile < nq_tiles:
            t = t128_ref[...]
            for c in range(nh):
                xs = _rms(acc[:, c * LANES:(c + 1) * LANES], gq_ref[...])
                p_ref[:, c * LANES:(c + 1) * LANES] = _rope(xs, t, half128).astype(BF16)
        elif tile == nq_tiles:
            t = t128_ref[...]
            for c in range(nh // 2):
                xs = _rms(acc[:, c * LANES:(c + 1) * LANES], gk_ref[...])
                p_ref[:, c * LANES:(c + 1) * LANES] = _rope(xs, t, half128).astype(BF16)
            p_ref[:, tn // 2:] = acc[:, tn // 2:].astype(BF16)
        else:
            t = t64_ref[...]
            for c in range(nh):
                p_ref[:, c * LANES:(c + 1) * LANES] = _rope(acc[:, c * LANES:(c + 1) * LANES], t, half64).astype(BF16)

    for s in range(n_tiles):
        @pl.when(j == s)
        def _(s=s):
            if s == 0:
                xn = _rms(h_ref[...], g_ref[...]).astype(BF16)
                xn_ref[...] = xn
                ps_ref[...] = _rope(_dot(xn, ws_ref[...]), tki_ref[...], half64)
                acc_ref[0] = _dot(xn, w_ref[...])
            else:
                acc_ref[s % 2] = _dot(xn_ref[...], w_ref[...])
                epilogue(s - 1, acc_ref[(s - 1) % 2], p_ref)
            if s == n_tiles - 1:
                epilogue(s, acc_ref[s % 2], pqi_ref)


def _inproj_a(h, g, w, ws, gq, gk, t128, t64, tki, tm):
    M, D = h.shape
    N = w.shape[1]
    tn = PROJ_COL_TILE
    hd = D // A_HEADS
    assert hd == LANES and D // 4 * 2 == tn and N == D + 2 * tn
    nt = t128.shape[0] // tm
    n_tiles = N // tn
    kern = functools.partial(_inproj_a_kernel, nq_tiles=D // tn, tn=tn,
                             half128=hd // ROPE_FRAC // 2, half64=IDX_DIM // ROPE_FRAC // 2)
    tab = pl.BlockSpec((tm, 3 * LANES), lambda i, j: (i % nt, 0))
    return pl.pallas_call(
        kern,
        grid=(M // tm, n_tiles),
        in_specs=[
            _row_block(tm, D),
            pl.BlockSpec((1, D), lambda i, j: (0, 0)),
            pl.BlockSpec((D, tn), lambda i, j: (0, j)),
            pl.BlockSpec((D, LANES), lambda i, j: (0, 0)),
            pl.BlockSpec((1, LANES), lambda i, j: (0, 0)),
            pl.BlockSpec((1, LANES), lambda i, j: (0, 0)),
            tab, tab, tab,
        ],
        out_specs=[
            pl.BlockSpec((tm, tn), lambda i, j: (i, jnp.maximum(j - 1, 0))),
            pl.BlockSpec((tm, tn), lambda i, j: (i, 0)),
            pl.BlockSpec((tm, LANES), lambda i, j: (i, 0)),
        ],
        out_shape=[jax.ShapeDtypeStruct((M, N - tn), BF16), jax.ShapeDtypeStruct((M, tn), BF16),
                   jax.ShapeDtypeStruct((M, LANES), F32)],
        scratch_shapes=[pltpu.VMEM((tm, D), BF16), pltpu.VMEM((2, tm, tn), F32)],
        compiler_params=_params("parallel", "arbitrary"),
        name="dsa_inproj",
    )(h, g.reshape(1, D), w, ws, gq, gk, t128, t64, tki)


def _rope_table(pos, d, extra_scale_lanes=None, extra_scale=1.0, reps=None):
    rot = d // ROPE_FRAC
    half = rot // 2
    inv = 1.0 / (ROPE_THETA ** (jnp.arange(0, rot, 2, dtype=F32) / rot))
    ang = pos[:, None] * inv[None, :]
    cos, sin = jnp.cos(ang), jnp.sin(ang)
    T = pos.shape[0]
    one = jnp.ones((T, d - rot), F32)
    zero = lambda n: jnp.zeros((T, n), F32)
    c = jnp.concatenate([cos, cos, one], axis=1)
    s1 = jnp.concatenate([-sin, zero(d - half)], axis=1)
    s2 = jnp.concatenate([zero(half), sin, zero(d - rot)], axis=1)
    if reps is None:
        reps = LANES // d
    c, s1, s2 = (jnp.tile(a, (1, reps)) for a in (c, s1, s2))
    fill = LANES - reps * d
    if fill:
        tail = jnp.ones((T, fill), F32)
        if extra_scale_lanes is not None:
            lo, hi = extra_scale_lanes
            lane = jnp.arange(reps * d, LANES)
            tail = jnp.where((lane >= lo) & (lane < hi), extra_scale, 1.0)[None, :] * tail
        c = jnp.concatenate([c, tail], axis=1)
        s1 = jnp.concatenate([s1, zero(fill)], axis=1)
        s2 = jnp.concatenate([s2, zero(fill)], axis=1)
    return jnp.concatenate([c, s1, s2], axis=1)


def _float_key(x):
    bits = lax.bitcast_convert_type(x, jnp.int32)
    return jnp.where(bits < 0, bits ^ jnp.int32(0x7FFFFFFF), bits)


def _dsa_keys(qi_ref, psq_ref, kie, kio, key_ref, *, tq, row0, nk, n_meta):
    wi = psq_ref[...]
    ke = kie[0:nk, :]
    ko = kio[0:nk, :]
    score = jnp.zeros((tq, nk), F32)
    for p in range(IDX_HEADS // 2):
        qp = qi_ref[:, p * LANES:(p + 1) * LANES]
        c0 = IDX_DIM + 2 * p
        score += jnp.maximum(_dot_nt(qp, ke), 0.0) * wi[:, c0:c0 + 1]
        score += jnp.maximum(_dot_nt(qp, ko), 0.0) * wi[:, c0 + 1:c0 + 2]
    col = lax.broadcasted_iota(jnp.int32, (tq, nk), 1)
    row = lax.broadcasted_iota(jnp.int32, (tq, nk), 0) + row0
    adm = jnp.where(col < LANES, col - n_meta, col - LANES - row - 1) < 0
    score = jnp.where(score == 0.0, 0.0, score)
    key_ref[:, 0:nk] = _float_key(jnp.where(adm, score, -jnp.inf))


def _count_ge(key_ref, cand, r0, rows, nch):
    acc = jnp.zeros((rows, LANES), F32)
    for c in range(nch):
        acc += jnp.where(key_ref[r0:r0 + rows, c * LANES:(c + 1) * LANES] >= cand, 1.0, 0.0)
    return jnp.sum(acc, axis=-1, keepdims=True)


def _search_trip(key_ref, trip, bases, *, tq, nk, topk):
    rb = tq // SEARCH_ROW_BLOCKS
    bases = list(bases)
    for u in range(SEARCH_UNROLL):
        bit = jnp.int32(31) - (trip.astype(jnp.int32) * SEARCH_UNROLL + u)
        step = lax.shift_left(jnp.int32(1), bit)
        for blk in range(SEARCH_ROW_BLOCKS):
            cand = bases[blk] + step
            cnt = _count_ge(key_ref, cand, blk * rb, rb, nk // LANES)
            bases[blk] = jnp.where(cnt >= float(topk), cand, bases[blk])
    return tuple(bases)


def _search_init(tq):
    return tuple(jnp.full((tq // SEARCH_ROW_BLOCKS, 1), INT_MIN, jnp.int32) for _ in range(SEARCH_ROW_BLOCKS))


def _dsa_mask(key_ref, bias_ref, bases, *, tq, nk, topk):
    NCH = nk // LANES
    kf = float(topk)
    ones = jnp.ones((LANES, LANES), BF16)
    thr = jnp.maximum(jnp.concatenate(bases, axis=0), NEG_INF_KEY + 1)
    cnt = _count_ge(key_ref, thr, 0, tq, NCH)
    for c in range(NCH):
        sl = slice(c * LANES, (c + 1) * LANES)
        bias_ref[:, sl] = jnp.where(key_ref[:, sl] >= thr, 0.0, -jnp.inf)

    @pl.when(jnp.max(cnt) > kf)
    def _():
        need = kf - _count_ge(key_ref, thr + 1, 0, tq, NCH)
        rr = lax.broadcasted_iota(jnp.int32, (LANES, LANES), 0)
        cc = lax.broadcasted_iota(jnp.int32, (LANES, LANES), 1)
        tri = jnp.where(rr <= cc, 1.0, 0.0).astype(BF16)
        run = jnp.zeros((tq, LANES), F32)
        for c in range(NCH):
            sl = slice(c * LANES, (c + 1) * LANES)
            kc = key_ref[:, sl]
            eq = kc == thr
            eqb = jnp.where(eq, 1.0, 0.0).astype(BF16)
            rank = run + _dot(eqb, tri)
            keep_eq = jnp.where(rank <= need, 0.0, -jnp.inf)
            bias_ref[:, sl] = jnp.where(kc > thr, 0.0, jnp.where(eq, keep_eq, -jnp.inf))
            run = run + _dot(eqb, ones)


def _attn_heads(qs, kcat, vcat, bias_ref, os, trip, *, nk):
    hd = LANES
    per_group = (A_HEADS // A_KV_HEADS) // ATT_HEADS_PER_TRIP
    g = trip // per_group
    kg = kcat[g, 0:nk, :]
    vg = vcat[g, 0:nk, :]
    heads = [trip * ATT_HEADS_PER_TRIP + r for r in range(ATT_HEADS_PER_TRIP)]
    probs = []
    for h in heads:
        logits = _dot_nt(qs[h], kg) + bias_ref[:, 0:nk]
        mx = jnp.max(logits, axis=-1, keepdims=True)
        probs.append(jnp.exp2(logits - mx).astype(BF16))
    for h, p in zip(heads, probs):
        pv = _dot(p, vg)
        os[h] = (pv[:, 0:hd] / pv[:, hd:hd + 1]).astype(BF16)


def _dsa_step(i, q_ref, qi_ref, psq_ref, o_ref, kcat, vcat, kie, kio, key_ref,
              bias_ref, qs, os, *, tq, nk, topk, n_meta):
    hd = LANES
    _dsa_keys(qi_ref, psq_ref, kie, kio, key_ref, tq=tq, row0=i * tq, nk=nk, n_meta=n_meta)
    bases = lax.fori_loop(0, 32 // SEARCH_UNROLL,
                          lambda t, b: _search_trip(key_ref, t, b, tq=tq, nk=nk, topk=topk), _search_init(tq))
    _dsa_mask(key_ref, bias_ref, bases, tq=tq, nk=nk, topk=topk)

    for h in range(A_HEADS):
        qs[h] = q_ref[:, h * hd:(h + 1) * hd]

    def trip(t, carry):
        _attn_heads(qs, kcat, vcat, bias_ref, os, t, nk=nk)
        return carry

    lax.fori_loop(0, A_HEADS // ATT_HEADS_PER_TRIP, trip, 0)
    for h in range(A_HEADS):
        o_ref[:, h * hd:(h + 1) * hd] = os[h]


def _dsa_kernel(q_ref, qi_ref, psq_ref, k_ref, v_ref, psk_ref, km_ref, vm_ref, psm_ref, o_ref,
                kcat, vcat, kie, kio, key_ref, bias_ref, qs, os, *, tq, S, topk, n_meta):
    i = pl.program_id(1)
    NK = LANES + S
    hd = LANES
    G = A_KV_HEADS
    nq = S // tq

    @pl.when(i == 0)
    def _():
        lane = lax.broadcasted_iota(jnp.int32, (NK, hd), 1)
        ones_col = jnp.where(lane == 0, 1.0, 0.0).astype(BF16)
        for g in range(G):
            sl = slice(g * hd, (g + 1) * hd)
            kcat[g, 0:LANES, :] = km_ref[:, sl]
            kcat[g, LANES:, :] = k_ref[:, sl]
            vcat[g, 0:LANES, 0:hd] = vm_ref[:, sl]
            vcat[g, LANES:, 0:hd] = v_ref[:, sl]
            vcat[g, :, hd:] = ones_col
        for dst0, dst1, src in ((0, LANES, psm_ref), (LANES, NK, psk_ref)):
            a = src[...]
            lane = lax.broadcasted_iota(jnp.int32, a.shape, 1)
            even = jnp.where(lane < IDX_DIM, a, 0.0)
            kie[dst0:dst1, :] = even.astype(BF16)
            kio[dst0:dst1, :] = pltpu.roll(even, IDX_DIM, 1).astype(BF16)

    bounds = sorted({-(-nq * (v + 1) // DSA_VARIANTS) for v in range(DSA_VARIANTS)})
    lo = 0
    for hi in bounds:
        pl.when((i >= lo) & (i < hi))(functools.partial(
            _dsa_step, i, q_ref, qi_ref, psq_ref, o_ref, kcat, vcat, kie, kio,
            key_ref, bias_ref, qs, os, tq=tq, nk=LANES + hi * tq, topk=topk, n_meta=n_meta))
        lo = hi


def _dsa(p, pqi, ps, pm, psm, B, S, tq, topk):
    M, N = p.shape
    D = N * 2 // 3
    kv_w = D // 4
    nq = S // tq
    NK = LANES + S
    qi_w = IDX_HEADS * IDX_DIM
    kern = functools.partial(_dsa_kernel, tq=tq, S=S, topk=topk, n_meta=N_META)
    return pl.pallas_call(
        kern,
        grid=(B, nq),
        in_specs=[
            pl.BlockSpec((tq, D), lambda b, i: (b * nq + i, 0)),
            pl.BlockSpec((tq, qi_w), lambda b, i: (b * nq + i, 0)),
            pl.BlockSpec((tq, LANES), lambda b, i: (b * nq + i, 0)),
            pl.BlockSpec((S, kv_w), lambda b, i: (b, D // kv_w)),
            pl.BlockSpec((S, kv_w), lambda b, i: (b, D // kv_w + 1)),
            pl.BlockSpec((S, LANES), lambda b, i: (b, 0)),
            pl.BlockSpec((LANES, kv_w), lambda b, i: (0, D // kv_w)),
            pl.BlockSpec((LANES, kv_w), lambda b, i: (0, D // kv_w + 1)),
            pl.BlockSpec((LANES, LANES), lambda b, i: (0, 0)),
        ],
        out_specs=pl.BlockSpec((tq, D), lambda b, i: (b * nq + i, 0)),
        out_shape=jax.ShapeDtypeStruct((M, D), BF16),
        scratch_shapes=[
            pltpu.VMEM((A_KV_HEADS, NK, LANES), BF16),
            pltpu.VMEM((A_KV_HEADS, NK, 2 * LANES), BF16),
            pltpu.VMEM((NK, LANES), BF16),
            pltpu.VMEM((NK, LANES), BF16),
            pltpu.VMEM((tq, NK), jnp.int32),
            pltpu.VMEM((tq, NK), F32),
            pltpu.VMEM((A_HEADS, tq, LANES), BF16),
            pltpu.VMEM((A_HEADS, tq, LANES), BF16),
        ],
        compiler_params=_params("parallel", "arbitrary"),
        name="dsa_attention",
    )(p, pqi, ps, p, p, ps, pm, pm, psm)


def kernel(x, meta_tokens, ffn1_norm, ffn1_w_gate, ffn1_w_up, ffn1_w_down, mix_norm, ffn2_norm, ffn2_w_gate, ffn2_w_up, ffn2_w_down, mlstm_w_in, mlstm_b_i, mlstm_b_f, mlstm_head_norm, mlstm_w_out, dsa_w_in, dsa_q_norm, dsa_k_norm, dsa_w_out):
    B, S, D = x.shape
    depth = ffn1_norm.shape[0]
    assert depth == 2 and meta_tokens.shape == (N_META, D)
    F = ffn1_w_gate.shape[-1]
    H, DK, DV = M_HEADS, D // (2 * M_HEADS), D // M_HEADS
    hd = D // A_HEADS
    L, TM, TMO, TF = MLSTM_CHUNK, ROW_TILE, OUT_ROW_TILE, FF_TILE
    assert S % L == 0 and S % TM == 0 and (B * S) % TMO == 0 and F % TF == 0

    hx = x.reshape(B * S, D)
    hm = meta_tokens.astype(x.dtype)
    bf = lambda w: w.astype(BF16)

    f1 = (ffn1_w_gate, ffn1_w_up, ffn1_w_down)
    f2 = (ffn2_w_gate, ffn2_w_up, ffn2_w_down)

    def ffn(hx, hm, g, ws, layer):
        hm, wg, wu, wd = _ffn_cast(hm, g, *ws, layer, TF)
        return _ffn(hx, g, wg[None], wu[None], wd[None], 0, TM, TF), hm

    def pad_cols(w):
        return jnp.pad(w, ((0, 0), (0, LANES - w.shape[1])))

    def pad_rows(a, n, front=False):
        r = n - a.shape[0]
        return jnp.pad(a, ((r, 0) if front else (0, r), (0, 0)))

    hx, hm = ffn(hx, hm, ffn1_norm[0], f1, 0)

    n_wide = 2 * H * DK + 2 * H * DV
    w_wide = bf(mlstm_w_in[0][:, :n_wide])
    w_gate = bf(pad_cols(mlstm_w_in[0][:, n_wide:]))
    px, psx = _inproj_m(hx, mix_norm[0], w_wide, w_gate, TM, MLSTM_PROJ_COL_TILE)
    pm, psm = _inproj_m(hm, mix_norm[0], w_wide, w_gate, N_META, MLSTM_PROJ_COL_TILE)

    bias = pad_cols(jnp.concatenate([mlstm_b_i[0], mlstm_b_f[0]]).astype(F32)[None, :])
    gh = mlstm_head_norm[0].astype(F32).reshape(1, H * DV)
    zc = jnp.zeros((H, DK, DV), F32)
    zn = jnp.zeros((H, 1, DK), F32)
    zm = jnp.zeros((H, 1, LANES), F32)
    ym, c0, n0, m0 = _mlstm(pad_rows(pm, L, True), pad_rows(psm, L, True), bias, gh, zc, zn, zm,
                            1, L, L - N_META)
    yx, _, _, _ = _mlstm(px, psx, bias, gh, c0[0], n0[0], m0[0], B, L, 0)
    w_out = bf(mlstm_w_out[0])
    hx = _outproj(yx, w_out, hx, TMO)
    hm = _outproj(ym[L - N_META:], w_out, hm, N_META)

    hx, hm = ffn(hx, hm, ffn2_norm[0], f2, 0)

    hx, hm = ffn(hx, hm, ffn1_norm[1], f1, 1)

    n_wide = A_HEADS * hd + 2 * A_KV_HEADS * hd + IDX_HEADS * IDX_DIM
    w_wide = bf(dsa_w_in[0][:, :n_wide])
    w_idx = bf(pad_cols(dsa_w_in[0][:, n_wide:]))
    gq = (dsa_q_norm[0].astype(F32) * (hd ** -0.5 * float(np.log2(np.e))))[None, :]
    gk = dsa_k_norm[0].astype(F32)[None, :]
    wi_scale = IDX_HEADS ** -0.5 * IDX_DIM ** -0.5
    pos = jnp.arange(N_META + S, dtype=F32)
    t128 = _rope_table(pos, hd)
    t64 = _rope_table(pos, IDX_DIM)
    tki = _rope_table(pos, IDX_DIM, (IDX_DIM, IDX_DIM + IDX_HEADS), wi_scale, reps=1)
    tabs_x = [t[N_META:] for t in (t128, t64, tki)]
    tabs_m = [t[:N_META] for t in (t128, t64, tki)]
    px, pqx, psx = _inproj_a(hx, mix_norm[1], w_wide, w_idx, gq, gk, *tabs_x, TM)
    pm, _, psm = _inproj_a(hm, mix_norm[1], w_wide, w_idx, gq, gk, *tabs_m, N_META)

    topk = min(TOPK_MAX, (N_META + S - N_META) // 4)
    ox = _dsa(px, pqx, psx, pad_rows(pm, LANES), pad_rows(psm, LANES), B, S, DSA_Q_TILE, topk)
    hx = _outproj(ox, bf(dsa_w_out[0]), hx, TMO)

    hx, _ = ffn(hx, hm, ffn2_norm[1], f2, 1)
    return hx.reshape(B, S, D)
```

```python
import functools

import numpy as np
import jax
import jax.numpy as jnp
from jax import lax
from jax.experimental import pallas as pl
from jax.experimental.pallas import tpu as pltpu

F32 = jnp.float32
BF16 = jnp.bfloat16

N_META = 16
FFN_HALF = 0.5
NORM_EPS = 1e-6
ROPE_THETA = 500000.0
ROPE_FRAC = 4
M_HEADS = 4
A_HEADS = 16
A_KV_HEADS = 4
IDX_HEADS = 16
IDX_DIM = 64
TOPK_MAX = 256
LOG_I_PAD = -1e30

LANES = 128
VMEM_LIMIT = 60 * 1024 * 1024
DSA_VARIANTS = 3
SEARCH_ROW_BLOCKS = 2
SEARCH_UNROLL = 8
ATT_HEADS_PER_TRIP = 4
FFN_SPLIT = 2
MLSTM_CHUNK = 256
ROW_TILE = 1024
OUT_ROW_TILE = 512
FF_TILE = 512
PROJ_COL_TILE = 1024
MLSTM_PROJ_COL_TILE = 2048
DSA_Q_TILE = 256
INT_MIN = int(np.iinfo(np.int32).min)
NEG_INF_KEY = int(np.array(-np.inf, np.float32).view(np.int32) ^ np.int32(0x7FFFFFFF))


def _params(*sem):
    return pltpu.CompilerParams(dimension_semantics=sem, vmem_limit_bytes=VMEM_LIMIT)


def _row_block(tm, d):
    return pl.BlockSpec((tm, d), lambda i, j: (i, 0))


def _rms(x, g):
    return x * lax.rsqrt(jnp.mean(x * x, axis=-1, keepdims=True) + NORM_EPS) * g


def _dot(a, b):
    return jnp.dot(a, b, preferred_element_type=F32)


def _dot_nt(a, b):
    return lax.dot_general(a, b, (((1,), (1,)), ((), ())), preferred_element_type=F32)


def _swiglu_half(xn, wg_ref, wu_ref, wd_ref):
    w = wg_ref.shape[1] // FFN_SPLIT
    acc = None
    for c in range(FFN_SPLIT):
        sl = slice(c * w, (c + 1) * w)
        gate = _dot(xn, wg_ref[:, sl])
        up = _dot(xn, wu_ref[:, sl])
        act = (gate * jax.nn.sigmoid(gate)) * (up * FFN_HALF)
        part = _dot(act.astype(BF16), wd_ref[sl, :])
        acc = part if acc is None else acc + part
    return acc


def _ffn_kernel(h_ref, g_ref, wg_ref, wu_ref, wd_ref, o_ref, xn_ref):
    j = pl.program_id(1)

    @pl.when(j == 0)
    def _():
        h = h_ref[...]
        xn = _rms(h, g_ref[...]).astype(BF16)
        xn_ref[...] = xn
        o_ref[...] = h + _swiglu_half(xn, wg_ref, wu_ref, wd_ref)

    @pl.when(j > 0)
    def _():
        o_ref[...] += _swiglu_half(xn_ref[...], wg_ref, wu_ref, wd_ref)


def _ffn(h, g, wg, wu, wd, layer, tm, tf):
    M, D = h.shape
    F = wg.shape[2]
    return pl.pallas_call(
        _ffn_kernel,
        grid=(M // tm, F // tf),
        in_specs=[
            _row_block(tm, D),
            pl.BlockSpec((1, D), lambda i, j: (0, 0)),
            pl.BlockSpec((None, D, tf), lambda i, j: (layer, 0, j)),
            pl.BlockSpec((None, D, tf), lambda i, j: (layer, 0, j)),
            pl.BlockSpec((None, tf, D), lambda i, j: (layer, j, 0)),
        ],
        out_specs=pl.BlockSpec((tm, D), lambda i, j: (i, 0)),
        out_shape=jax.ShapeDtypeStruct((M, D), F32),
        scratch_shapes=[pltpu.VMEM((tm, D), BF16)],
        compiler_params=_params("parallel", "arbitrary"),
        name="ffn",
    )(h, g.reshape(1, D), wg, wu, wd)


def _ffn_cast_kernel(h_ref, g_ref, wg_ref, wu_ref, wd_ref, o_ref, wgb_ref, wub_ref, wdb_ref, xn_ref):
    wgb_ref[...] = wg_ref[...].astype(BF16)
    wub_ref[...] = wu_ref[...].astype(BF16)
    wdb_ref[...] = wd_ref[...].astype(BF16)

    @pl.when(pl.program_id(0) == 0)
    def _():
        h = h_ref[...]
        xn_ref[...] = _rms(h, g_ref[...]).astype(BF16)
        o_ref[...] = h

    o_ref[...] += _swiglu_half(xn_ref[...], wgb_ref, wub_ref, wdb_ref)


def _ffn_cast(h, g, wg, wu, wd, layer, tf):
    M, D = h.shape
    F = wg.shape[2]
    return pl.pallas_call(
        _ffn_cast_kernel,
        grid=(F // tf,),
        in_specs=[
            pl.BlockSpec((M, D), lambda j: (0, 0)),
            pl.BlockSpec((1, D), lambda j: (0, 0)),
            pl.BlockSpec((None, D, tf), lambda j: (layer, 0, j)),
            pl.BlockSpec((None, D, tf), lambda j: (layer, 0, j)),
            pl.BlockSpec((None, tf, D), lambda j: (layer, j, 0)),
        ],
        out_specs=[
            pl.BlockSpec((M, D), lambda j: (0, 0)),
            pl.BlockSpec((D, tf), lambda j: (0, j)),
            pl.BlockSpec((D, tf), lambda j: (0, j)),
            pl.BlockSpec((tf, D), lambda j: (j, 0)),
        ],
        out_shape=[
            jax.ShapeDtypeStruct((M, D), F32),
            jax.ShapeDtypeStruct((D, F), BF16),
            jax.ShapeDtypeStruct((D, F), BF16),
            jax.ShapeDtypeStruct((F, D), BF16),
        ],
        scratch_shapes=[pltpu.VMEM((M, D), BF16)],
        compiler_params=_params("arbitrary"),
        name="ffn_meta_cast",
    )(h, g.reshape(1, D), wg, wu, wd)


def _inproj_m_kernel(h_ref, g_ref, w_ref, ws_ref, p_ref, ps_ref, xn_ref):
    j = pl.program_id(1)

    @pl.when(j == 0)
    def _():
        xn = _rms(h_ref[...], g_ref[...]).astype(BF16)
        xn_ref[...] = xn
        ps_ref[...] = _dot(xn, ws_ref[...])
        p_ref[...] = _dot(xn, w_ref[...]).astype(BF16)

    @pl.when(j > 0)
    def _():
        p_ref[...] = _dot(xn_ref[...], w_ref[...]).astype(BF16)


def _inproj_m(h, g, w, ws, tm, tn):
    M, D = h.shape
    N = w.shape[1]
    return pl.pallas_call(
        _inproj_m_kernel,
        grid=(M // tm, N // tn),
        in_specs=[
            _row_block(tm, D),
            pl.BlockSpec((1, D), lambda i, j: (0, 0)),
            pl.BlockSpec((D, tn), lambda i, j: (0, j)),
            pl.BlockSpec((D, LANES), lambda i, j: (0, 0)),
        ],
        out_specs=[
            pl.BlockSpec((tm, tn), lambda i, j: (i, j)),
            pl.BlockSpec((tm, LANES), lambda i, j: (i, 0)),
        ],
        out_shape=[jax.ShapeDtypeStruct((M, N), BF16), jax.ShapeDtypeStruct((M, LANES), F32)],
        scratch_shapes=[pltpu.VMEM((tm, D), BF16)],
        compiler_params=_params("parallel", "arbitrary"),
        name="mlstm_inproj",
    )(h, g.reshape(1, D), w, ws)


def _log_sigmoid(x):
    return jnp.minimum(x, 0.0) - jnp.log1p(jnp.exp(-jnp.abs(x)))


def _mlstm_kernel(q_ref, k_ref, v_ref, o_ref, ps_ref, bias_ref, gh_ref, c0_ref, n0_ref, m0_ref,
                  y_ref, c_ref, n_ref, m_ref, *, L, npad, H, DK, DV):
    @pl.when(pl.program_id(1) == 0)
    def _():
        c_ref[...] = c0_ref[...]
        n_ref[...] = n0_ref[...]
        m_ref[...] = m0_ref[...]

    gates = ps_ref[...] + bias_ref[...]
    logf = _log_sigmoid(gates)
    if npad:
        valid = lax.broadcasted_iota(jnp.int32, (L, 1), 0) >= npad
        logf = jnp.where(valid, logf, 0.0)
        gates = jnp.where(valid, gates, LOG_I_PAD)
    ii = lax.broadcasted_iota(jnp.int32, (L, L), 0)
    jj = lax.broadcasted_iota(jnp.int32, (L, L), 1)
    causal = jj <= ii
    cum = jnp.dot(causal.astype(F32), logf, preferred_element_type=F32,
                  precision=lax.Precision.HIGHEST)
    cum_t = cum.T
    gates_t = gates.T

    updates = []
    for h in range(H):
        b_col = cum[:, H + h:H + h + 1]
        b_row = cum_t[H + h:H + h + 1, :]
        li_col = gates[:, h:h + 1]
        li_row = gates_t[h:h + 1, :]
        m_prev = m_ref[h][:, 0:1]
        q = q_ref[:, h * DK:(h + 1) * DK] * (DK ** -0.5)
        k = k_ref[:, h * DK:(h + 1) * DK]
        v = v_ref[:, h * DV:(h + 1) * DV]
        c_old = c_ref[h]
        n_old = n_ref[h]

        dm = jnp.where(causal, b_col - b_row + li_row, -jnp.inf)
        inter = b_col + m_prev
        m_t = jnp.maximum(inter, jnp.max(dm, axis=-1, keepdims=True))
        decay_mat = jnp.exp(dm - m_t)
        w_inter = jnp.exp(inter - m_t)
        b_last = b_col[L - 1:L, :]
        g_row = b_last - b_row + li_row
        g_col = b_last - b_col + li_col
        m_new = jnp.maximum(b_last + m_prev, jnp.max(g_row, axis=-1, keepdims=True))
        decay = jnp.exp(b_last + m_prev - m_new)
        updates.append((m_new, decay, jnp.exp(g_col - m_new)))

        s = _dot_nt(q, k) * decay_mat
        num = w_inter * _dot(q, c_old.astype(BF16)) + _dot(s.astype(BF16), v)
        qn = jnp.sum(q.astype(F32) * n_old, axis=-1, keepdims=True)
        den = w_inter * qn + jnp.sum(s, axis=-1, keepdims=True)
        hout = num / jnp.maximum(jnp.abs(den), jnp.exp(-m_t))

        hn = _rms(hout, gh_ref[:, h * DV:(h + 1) * DV])
        og = jax.nn.sigmoid(o_ref[:, h * DV:(h + 1) * DV].astype(F32))
        y_ref[:, h * DV:(h + 1) * DV] = (og * hn).astype(BF16)

    for h, (m_new, decay, wk) in enumerate(updates):
        k = k_ref[:, h * DK:(h + 1) * DK]
        v = v_ref[:, h * DV:(h + 1) * DV]
        kw = k.astype(F32) * wk
        c_ref[h] = decay * c_ref[h] + _dot(kw.T.astype(BF16), v)
        n_ref[h] = decay * n_ref[h] + jnp.sum(kw, axis=0, keepdims=True)
        m_ref[h] = jnp.broadcast_to(m_new, (1, LANES))


def _mlstm(p, ps, bias, gh, c0, n0, m0, B, L, npad):
    M = p.shape[0]
    H = M_HEADS
    DK, DV = c0.shape[1], c0.shape[2]
    NC = M // (B * L)
    row = lambda b, c: b * NC + c
    qk_w, vo_w = H * DK, H * DV
    assert vo_w == 2 * qk_w
    kern = functools.partial(_mlstm_kernel, L=L, npad=npad, H=H, DK=DK, DV=DV)
    return pl.pallas_call(
        kern,
        grid=(B, NC),
        in_specs=[
            pl.BlockSpec((L, qk_w), lambda b, c: (row(b, c), 0)),
            pl.BlockSpec((L, qk_w), lambda b, c: (row(b, c), 1)),
            pl.BlockSpec((L, vo_w), lambda b, c: (row(b, c), 1)),
            pl.BlockSpec((L, vo_w), lambda b, c: (row(b, c), 2)),
            pl.BlockSpec((L, LANES), lambda b, c: (row(b, c), 0)),
            pl.BlockSpec((1, LANES), lambda b, c: (0, 0)),
            pl.BlockSpec((1, vo_w), lambda b, c: (0, 0)),
            pl.BlockSpec((H, DK, DV), lambda b, c: (0, 0, 0)),
            pl.BlockSpec((H, 1, DK), lambda b, c: (0, 0, 0)),
            pl.BlockSpec((H, 1, LANES), lambda b, c: (0, 0, 0)),
        ],
        out_specs=[
            pl.BlockSpec((L, vo_w), lambda b, c: (row(b, c), 0)),
            pl.BlockSpec((None, H, DK, DV), lambda b, c: (b, 0, 0, 0)),
            pl.BlockSpec((None, H, 1, DK), lambda b, c: (b, 0, 0, 0)),
            pl.BlockSpec((None, H, 1, LANES), lambda b, c: (b, 0, 0, 0)),
        ],
        out_shape=[
            jax.ShapeDtypeStruct((M, vo_w), BF16),
            jax.ShapeDtypeStruct((B, H, DK, DV), F32),
            jax.ShapeDtypeStruct((B, H, 1, DK), F32),
            jax.ShapeDtypeStruct((B, H, 1, LANES), F32),
        ],
        compiler_params=_params("parallel", "arbitrary"),
        name="mlstm",
    )(p, p, p, p, ps, bias, gh, c0, n0, m0)


def _outproj_kernel(y_ref, w_ref, h_ref, o_ref):
    o_ref[...] = h_ref[...] + _dot(y_ref[...], w_ref[...])


def _outproj(y, w, h, tm):
    M, D = h.shape
    K = y.shape[1]
    return pl.pallas_call(
        _outproj_kernel,
        grid=(M // tm,),
        in_specs=[
            pl.BlockSpec((tm, K), lambda i: (i, 0)),
            pl.BlockSpec((K, D), lambda i: (0, 0)),
            pl.BlockSpec((tm, D), lambda i: (i, 0)),
        ],
        out_specs=pl.BlockSpec((tm, D), lambda i: (i, 0)),
        out_shape=jax.ShapeDtypeStruct((M, D), F32),
        compiler_params=_params("parallel"),
        name="outproj",
    )(y, w, h)


def _rope(x, tab, half):
    c = tab[:, 0:LANES]
    s1 = tab[:, LANES:2 * LANES]
    s2 = tab[:, 2 * LANES:3 * LANES]
    return x * c + pltpu.roll(x, LANES - half, 1) * s1 + pltpu.roll(x, half, 1) * s2


def _inproj_a_kernel(h_ref, g_ref, w_ref, ws_ref, gq_ref, gk_ref, t128_ref, t64_ref, tki_ref,
                     p_ref, pqi_ref, ps_ref, xn_ref, acc_ref, *, nq_tiles, tn, half128, half64):
    j = pl.program_id(1)
    nh = tn // LANES
    n_tiles = nq_tiles + 2

    def epilogue(tile, acc, p_ref):
        if tile < nq_tiles:
            t = t128_ref[...]
            for c in range(nh):
                xs = _rms(acc[:, c * LANES:(c + 1) * LANES], gq_ref[...])
                p_ref[:, c * LANES:(c + 1) * LANES] = _rope(xs, t, half128).astype(BF16)
        elif tile == nq_tiles:
            t = t128_ref[...]
            for c in range(nh // 2):
                xs = _rms(acc[:, c * LANES:(c + 1) * LANES], gk_ref[...])
                p_ref[:, c * LANES:(c + 1) * LANES] = _rope(xs, t, half128).astype(BF16)
            p_ref[:, tn // 2:] = acc[:, tn // 2:].astype(BF16)
        else:
            t = t64_ref[...]
            for c in range(nh):
                p_ref[:, c * LANES:(c + 1) * LANES] = _rope(acc[:, c * LANES:(c + 1) * LANES], t, half64).astype(BF16)

    for s in range(n_tiles):
        @pl.when(j == s)
        def _(s=s):
            if s == 0:
                xn = _rms(h_ref[...], g_ref[...]).astype(BF16)
                xn_ref[...] = xn
                ps_ref[...] = _rope(_dot(xn, ws_ref[...]), tki_ref[...], half64)
                acc_ref[0] = _dot(xn, w_ref[...])
            else:
                acc_ref[s % 2] = _dot(xn_ref[...], w_ref[...])
                epilogue(s - 1, acc_ref[(s - 1) % 2], p_ref)
            if s == n_tiles - 1:
                epilogue(s, acc_ref[s % 2], pqi_ref)


def _inproj_a(h, g, w, ws, gq, gk, t128, t64, tki, tm):
    M, D = h.shape
    N = w.shape[1]
    tn = PROJ_COL_TILE
    hd = D // A_HEADS
    assert hd == LANES and D // 4 * 2 == tn and N == D + 2 * tn
    nt = t128.shape[0] // tm
    n_tiles = N // tn
    kern = functools.partial(_inproj_a_kernel, nq_tiles=D // tn, tn=tn,
                             half128=hd // ROPE_FRAC // 2, half64=IDX_DIM // ROPE_FRAC // 2)
    tab = pl.BlockSpec((tm, 3 * LANES), lambda i, j: (i % nt, 0))
    return pl.pallas_call(
        kern,
        grid=(M // tm, n_tiles),
        in_specs=[
            _row_block(tm, D),
            pl.BlockSpec((1, D), lambda i, j: (0, 0)),
            pl.BlockSpec((D, tn), lambda i, j: (0, j)),
            pl.BlockSpec((D, LANES), lambda i, j: (0, 0)),
            pl.BlockSpec((1, LANES), lambda i, j: (0, 0)),
            pl.BlockSpec((1, LANES), lambda i, j: (0, 0)),
            tab, tab, tab,
        ],
        out_specs=[
            pl.BlockSpec((tm, tn), lambda i, j: (i, jnp.maximum(j - 1, 0))),
            pl.BlockSpec((tm, tn), lambda i, j: (i, 0)),
            pl.BlockSpec((tm, LANES), lambda i, j: (i, 0)),
        ],
        out_shape=[jax.ShapeDtypeStruct((M, N - tn), BF16), jax.ShapeDtypeStruct((M, tn), BF16),
                   jax.ShapeDtypeStruct((M, LANES), F32)],
        scratch_shapes=[pltpu.VMEM((tm, D), BF16), pltpu.VMEM((2, tm, tn), F32)],
        compiler_params=_params("parallel", "arbitrary"),
        name="dsa_inproj",
    )(h, g.reshape(1, D), w, ws, gq, gk, t128, t64, tki)


def _rope_table(pos, d, extra_scale_lanes=None, extra_scale=1.0, reps=None):
    rot = d // ROPE_FRAC
    half = rot // 2
    inv = 1.0 / (ROPE_THETA ** (jnp.arange(0, rot, 2, dtype=F32) / rot))
    ang = pos[:, None] * inv[None, :]
    cos, sin = jnp.cos(ang), jnp.sin(ang)
    T = pos.shape[0]
    one = jnp.ones((T, d - rot), F32)
    zero = lambda n: jnp.zeros((T, n), F32)
    c = jnp.concatenate([cos, cos, one], axis=1)
    s1 = jnp.concatenate([-sin, zero(d - half)], axis=1)
    s2 = jnp.concatenate([zero(half), sin, zero(d - rot)], axis=1)
    if reps is None:
        reps = LANES // d
    c, s1, s2 = (jnp.tile(a, (1, reps)) for a in (c, s1, s2))
    fill = LANES - reps * d
    if fill:
        tail = jnp.ones((T, fill), F32)
        if extra_scale_lanes is not None:
            lo, hi = extra_scale_lanes
            lane = jnp.arange(reps * d, LANES)
            tail = jnp.where((lane >= lo) & (lane < hi), extra_scale, 1.0)[None, :] * tail
        c = jnp.concatenate([c, tail], axis=1)
        s1 = jnp.concatenate([s1, zero(fill)], axis=1)
        s2 = jnp.concatenate([s2, zero(fill)], axis=1)
    return jnp.concatenate([c, s1, s2], axis=1)


def _float_key(x):
    bits = lax.bitcast_convert_type(x, jnp.int32)
    return jnp.where(bits < 0, bits ^ jnp.int32(0x7FFFFFFF), bits)


def _dsa_keys(qi_ref, psq_ref, kie, kio, key_ref, *, tq, row0, nk, n_meta):
    wi = psq_ref[...]
    ke = kie[0:nk, :]
    ko = kio[0:nk, :]
    score = jnp.zeros((tq, nk), F32)
    for p in range(IDX_HEADS // 2):
        qp = qi_ref[:, p * LANES:(p + 1) * LANES]
        c0 = IDX_DIM + 2 * p
        score += jnp.maximum(_dot_nt(qp, ke), 0.0) * wi[:, c0:c0 + 1]
        score += jnp.maximum(_dot_nt(qp, ko), 0.0) * wi[:, c0 + 1:c0 + 2]
    col = lax.broadcasted_iota(jnp.int32, (tq, nk), 1)
    row = lax.broadcasted_iota(jnp.int32, (tq, nk), 0) + row0
    adm = jnp.where(col < LANES, col - n_meta, col - LANES - row - 1) < 0
    score = jnp.where(score == 0.0, 0.0, score)
    key_ref[:, 0:nk] = _float_key(jnp.where(adm, score, -jnp.inf))


def _count_ge(key_ref, cand, r0, rows, nch):
    acc = jnp.zeros((rows, LANES), F32)
    for c in range(nch):
        acc += jnp.where(key_ref[r0:r0 + rows, c * LANES:(c + 1) * LANES] >= cand, 1.0, 0.0)
    return jnp.sum(acc, axis=-1, keepdims=True)


def _search_trip(key_ref, trip, bases, *, tq, nk, topk):
    rb = tq // SEARCH_ROW_BLOCKS
    bases = list(bases)
    for u in range(SEARCH_UNROLL):
        bit = jnp.int32(31) - (trip.astype(jnp.int32) * SEARCH_UNROLL + u)
        step = lax.shift_left(jnp.int32(1), bit)
        for blk in range(SEARCH_ROW_BLOCKS):
            cand = bases[blk] + step
            cnt = _count_ge(key_ref, cand, blk * rb, rb, nk // LANES)
            bases[blk] = jnp.where(cnt >= float(topk), cand, bases[blk])
    return tuple(bases)


def _search_init(tq):
    return tuple(jnp.full((tq // SEARCH_ROW_BLOCKS, 1), INT_MIN, jnp.int32) for _ in range(SEARCH_ROW_BLOCKS))


def _dsa_mask(key_ref, bias_ref, bases, *, tq, nk, topk):
    NCH = nk // LANES
    kf = float(topk)
    ones = jnp.ones((LANES, LANES), BF16)
    thr = jnp.maximum(jnp.concatenate(bases, axis=0), NEG_INF_KEY + 1)
    cnt = _count_ge(key_ref, thr, 0, tq, NCH)
    for c in range(NCH):
        sl = slice(c * LANES, (c + 1) * LANES)
        bias_ref[:, sl] = jnp.where(key_ref[:, sl] >= thr, 0.0, -jnp.inf)

    @pl.when(jnp.max(cnt) > kf)
    def _():
        need = kf - _count_ge(key_ref, thr + 1, 0, tq, NCH)
        rr = lax.broadcasted_iota(jnp.int32, (LANES, LANES), 0)
        cc = lax.broadcasted_iota(jnp.int32, (LANES, LANES), 1)
        tri = jnp.where(rr <= cc, 1.0, 0.0).astype(BF16)
        run = jnp.zeros((tq, LANES), F32)
        for c in range(NCH):
            sl = slice(c * LANES, (c + 1) * LANES)
            kc = key_ref[:, sl]
            eq = kc == thr
            eqb = jnp.where(eq, 1.0, 0.0).astype(BF16)
            rank = run + _dot(eqb, tri)
            keep_eq = jnp.where(rank <= need, 0.0, -jnp.inf)
            bias_ref[:, sl] = jnp.where(kc > thr, 0.0, jnp.where(eq, keep_eq, -jnp.inf))
            run = run + _dot(eqb, ones)


def _attn_heads(qs, kcat, vcat, bias_ref, os, trip, *, nk):
    hd = LANES
    per_group = (A_HEADS // A_KV_HEADS) // ATT_HEADS_PER_TRIP
    g = trip // per_group
    kg = kcat[g, 0:nk, :]
    vg = vcat[g, 0:nk, :]
    heads = [trip * ATT_HEADS_PER_TRIP + r for r in range(ATT_HEADS_PER_TRIP)]
    probs = []
    for h in heads:
        logits = _dot_nt(qs[h], kg) + bias_ref[:, 0:nk]
        mx = jnp.max(logits, axis=-1, keepdims=True)
        probs.append(jnp.exp2(logits - mx).astype(BF16))
    for h, p in zip(heads, probs):
        pv = _dot(p, vg)
        os[h] = (pv[:, 0:hd] / pv[:, hd:hd + 1]).astype(BF16)


def _dsa_step(i, q_ref, qi_ref, psq_ref, o_ref, kcat, vcat, kie, kio, key_ref,
              bias_ref, qs, os, *, tq, nk, topk, n_meta):
    hd = LANES
    _dsa_keys(qi_ref, psq_ref, kie, kio, key_ref, tq=tq, row0=i * tq, nk=nk, n_meta=n_meta)
    bases = lax.fori_loop(0, 32 // SEARCH_UNROLL,
                          lambda t, b: _search_trip(key_ref, t, b, tq=tq, nk=nk, topk=topk), _search_init(tq))
    _dsa_mask(key_ref, bias_ref, bases, tq=tq, nk=nk, topk=topk)

    for h in range(A_HEADS):
        qs[h] = q_ref[:, h * hd:(h + 1) * hd]

    def trip(t, carry):
        _attn_heads(qs, kcat, vcat, bias_ref, os, t, nk=nk)
        return carry

    lax.fori_loop(0, A_HEADS // ATT_HEADS_PER_TRIP, trip, 0)
    for h in range(A_HEADS):
        o_ref[:, h * hd:(h + 1) * hd] = os[h]


def _dsa_kernel(q_ref, qi_ref, psq_ref, k_ref, v_ref, psk_ref, km_ref, vm_ref, psm_ref, o_ref,
                kcat, vcat, kie, kio, key_ref, bias_ref, qs, os, *, tq, S, topk, n_meta):
    i = pl.program_id(1)
    NK = LANES + S
    hd = LANES
    G = A_KV_HEADS
    nq = S // tq

    @pl.when(i == 0)
    def _():
        lane = lax.broadcasted_iota(jnp.int32, (NK, hd), 1)
        ones_col = jnp.where(lane == 0, 1.0, 0.0).astype(BF16)
        for g in range(G):
            sl = slice(g * hd, (g + 1) * hd)
            kcat[g, 0:LANES, :] = km_ref[:, sl]
            kcat[g, LANES:, :] = k_ref[:, sl]
            vcat[g, 0:LANES, 0:hd] = vm_ref[:, sl]
            vcat[g, LANES:, 0:hd] = v_ref[:, sl]
            vcat[g, :, hd:] = ones_col
        for dst0, dst1, src in ((0, LANES, psm_ref), (LANES, NK, psk_ref)):
            a = src[...]
            lane = lax.broadcasted_iota(jnp.int32, a.shape, 1)
            even = jnp.where(lane < IDX_DIM, a, 0.0)
            kie[dst0:dst1, :] = even.astype(BF16)
            kio[dst0:dst1, :] = pltpu.roll(even, IDX_DIM, 1).astype(BF16)

    bounds = sorted({-(-nq * (v + 1) // DSA_VARIANTS) for v in range(DSA_VARIANTS)})
    lo = 0
    for hi in bounds:
        pl.when((i >= lo) & (i < hi))(functools.partial(
            _dsa_step, i, q_ref, qi_ref, psq_ref, o_ref, kcat, vcat, kie, kio,
            key_ref, bias_ref, qs, os, tq=tq, nk=LANES + hi * tq, topk=topk, n_meta=n_meta))
        lo = hi


def _dsa(p, pqi, ps, pm, psm, B, S, tq, topk):
    M, N = p.shape
    D = N * 2 // 3
    kv_w = D // 4
    nq = S // tq
    NK = LANES + S
    qi_w = IDX_HEADS * IDX_DIM
    kern = functools.partial(_dsa_kernel, tq=tq, S=S, topk=topk, n_meta=N_META)
    return pl.pallas_call(
        kern,
        grid=(B, nq),
        in_specs=[
            pl.BlockSpec((tq, D), lambda b, i: (b * nq + i, 0)),
            pl.BlockSpec((tq, qi_w), lambda b, i: (b * nq + i, 0)),
            pl.BlockSpec((tq, LANES), lambda b, i: (b * nq + i, 0)),
            pl.BlockSpec((S, kv_w), lambda b, i: (b, D // kv_w)),
            pl.BlockSpec((S, kv_w), lambda b, i: (b, D // kv_w + 1)),
            pl.BlockSpec((S, LANES), lambda b, i: (b, 0)),
            pl.BlockSpec((LANES, kv_w), lambda b, i: (0, D // kv_w)),
            pl.BlockSpec((LANES, kv_w), lambda b, i: (0, D // kv_w + 1)),
            pl.BlockSpec((LANES, LANES), lambda b, i: (0, 0)),
        ],
        out_specs=pl.BlockSpec((tq, D), lambda b, i: (b * nq + i, 0)),
        out_shape=jax.ShapeDtypeStruct((M, D), BF16),
        scratch_shapes=[
            pltpu.VMEM((A_KV_HEADS, NK, LANES), BF16),
            pltpu.VMEM((A_KV_HEADS, NK, 2 * LANES), BF16),
            pltpu.VMEM((NK, LANES), BF16),
            pltpu.VMEM((NK, LANES), BF16),
            pltpu.VMEM((tq, NK), jnp.int32),
            pltpu.VMEM((tq, NK), F32),
            pltpu.VMEM((A_HEADS, tq, LANES), BF16),
            pltpu.VMEM((A_HEADS, tq, LANES), BF16),
        ],
        compiler_params=_params("parallel", "arbitrary"),
        name="dsa_attention",
    )(p, pqi, ps, p, p, ps, pm, pm, psm)


def kernel(x, meta_tokens, ffn1_norm, ffn1_w_gate, ffn1_w_up, ffn1_w_down, mix_norm, ffn2_norm, ffn2_w_gate, ffn2_w_up, ffn2_w_down, mlstm_w_in, mlstm_b_i, mlstm_b_f, mlstm_head_norm, mlstm_w_out, dsa_w_in, dsa_q_norm, dsa_k_norm, dsa_w_out):
    B, S, D = x.shape
    depth = ffn1_norm.shape[0]
    assert depth == 2 and meta_tokens.shape == (N_META, D)
    F = ffn1_w_gate.shape[-1]
    H, DK, DV = M_HEADS, D // (2 * M_HEADS), D // M_HEADS
    hd = D // A_HEADS
    L, TM, TMO, TF = MLSTM_CHUNK, ROW_TILE, OUT_ROW_TILE, FF_TILE
    assert S % L == 0 and S % TM == 0 and (B * S) % TMO == 0 and F % TF == 0

    hx = x.reshape(B * S, D)
    hm = meta_tokens.astype(x.dtype)
    bf = lambda w: w.astype(BF16)

    f1 = (ffn1_w_gate, ffn1_w_up, ffn1_w_down)
    f2 = (ffn2_w_gate, ffn2_w_up, ffn2_w_down)

    def ffn(hx, hm, g, ws, layer):
        hm, wg, wu, wd = _ffn_cast(hm, g, *ws, layer, TF)
        return _ffn(hx, g, wg[None], wu[None], wd[None], 0, TM, TF), hm

    def pad_cols(w):
        return jnp.pad(w, ((0, 0), (0, LANES - w.shape[1])))

    def pad_rows(a, n, front=False):
        r = n - a.shape[0]
        return jnp.pad(a, ((r, 0) if front else (0, r), (0, 0)))

    hx, hm = ffn(hx, hm, ffn1_norm[0], f1, 0)

    n_wide = 2 * H * DK + 2 * H * DV
    w_wide = bf(mlstm_w_in[0][:, :n_wide])
    w_gate = bf(pad_cols(mlstm_w_in[0][:, n_wide:]))
    px, psx = _inproj_m(hx, mix_norm[0], w_wide, w_gate, TM, MLSTM_PROJ_COL_TILE)
    pm, psm = _inproj_m(hm, mix_norm[0], w_wide, w_gate, N_META, MLSTM_PROJ_COL_TILE)

    bias = pad_cols(jnp.concatenate([mlstm_b_i[0], mlstm_b_f[0]]).astype(F32)[None, :])
    gh = mlstm_head_norm[0].astype(F32).reshape(1, H * DV)
    zc = jnp.zeros((H, DK, DV), F32)
    zn = jnp.zeros((H, 1, DK), F32)
    zm = jnp.zeros((H, 1, LANES), F32)
    ym, c0, n0, m0 = _mlstm(pad_rows(pm, L, True), pad_rows(psm, L, True), bias, gh, zc, zn, zm,
                            1, L, L - N_META)
    yx, _, _, _ = _mlstm(px, psx, bias, gh, c0[0], n0[0], m0[0], B, L, 0)
    w_out = bf(mlstm_w_out[0])
    hx = _outproj(yx, w_out, hx, TMO)
    hm = _outproj(ym[L - N_META:], w_out, hm, N_META)

    hx, hm = ffn(hx, hm, ffn2_norm[0], f2, 0)

    hx, hm = ffn(hx, hm, ffn1_norm[1], f1, 1)

    n_wide = A_HEADS * hd + 2 * A_KV_HEADS * hd + IDX_HEADS * IDX_DIM
    w_wide = bf(dsa_w_in[0][:, :n_wide])
    w_idx = bf(pad_cols(dsa_w_in[0][:, n_wide:]))
    gq = (dsa_q_norm[0].astype(F32) * (hd ** -0.5 * float(np.log2(np.e))))[None, :]
    gk = dsa_k_norm[0].astype(F32)[None, :]
    wi_scale = IDX_HEADS ** -0.5 * IDX_DIM ** -0.5
    pos = jnp.arange(N_META + S, dtype=F32)
    t128 = _rope_table(pos, hd)
    t64 = _rope_table(pos, IDX_DIM)
    tki = _rope_table(pos, IDX_DIM, (IDX_DIM, IDX_DIM + IDX_HEADS), wi_scale, reps=1)
    tabs_x = [t[N_META:] for t in (t128, t64, tki)]
    tabs_m = [t[:N_META] for t in (t128, t64, tki)]
    px, pqx, psx = _inproj_a(hx, mix_norm[1], w_wide, w_idx, gq, gk, *tabs_x, TM)
    pm, _, psm = _inproj_a(hm, mix_norm[1], w_wide, w_idx, gq, gk, *tabs_m, N_META)

    topk = min(TOPK_MAX, (N_META + S - N_META) // 4)
    ox = _dsa(px, pqx, psx, pad_rows(pm, LANES), pad_rows(psm, LANES), B, S, DSA_Q_TILE, topk)
    hx = _outproj(ox, bf(dsa_w_out[0]), hx, TMO)

    hx, _ = ffn(hx, hm, ffn2_norm[1], f2, 1)
    return hx.reshape(B, S, D)
```

```python
import functools

import numpy as np
import jax
import jax.numpy as jnp
from jax import lax
from jax.experimental import pallas as pl
from jax.experimental.pallas import tpu as pltpu

F32 = jnp.float32
BF16 = jnp.bfloat16

N_META = 16
FFN_HALF = 0.5
NORM_EPS = 1e-6
ROPE_THETA = 500000.0
ROPE_FRAC = 4
M_HEADS = 4
A_HEADS = 16
A_KV_HEADS = 4
IDX_HEADS = 16
IDX_DIM = 64
TOPK_MAX = 256
LOG_I_PAD = -1e30

LANES = 128
VMEM_LIMIT = 60 * 1024 * 1024
DSA_VARIANTS = 3
SEARCH_ROW_BLOCKS = 2
SEARCH_UNROLL = 8
ATT_HEADS_PER_TRIP = 4
FFN_SPLIT = 2
MLSTM_CHUNK = 256
ROW_TILE = 1024
OUT_ROW_TILE = 1024
FF_TILE = 512
PROJ_COL_TILE = 1024
MLSTM_PROJ_COL_TILE = 2048
DSA_Q_TILE = 256
INT_MIN = int(np.iinfo(np.int32).min)
NEG_INF_KEY = int(np.array(-np.inf, np.float32).view(np.int32) ^ np.int32(0x7FFFFFFF))


def _params(*sem):
    return pltpu.CompilerParams(dimension_semantics=sem, vmem_limit_bytes=VMEM_LIMIT)


def _row_block(tm, d):
    return pl.BlockSpec((tm, d), lambda i, j: (i, 0))


def _rms(x, g):
    return x * lax.rsqrt(jnp.mean(x * x, axis=-1, keepdims=True) + NORM_EPS) * g


def _dot(a, b):
    return jnp.dot(a, b, preferred_element_type=F32)


def _dot_nt(a, b):
    return lax.dot_general(a, b, (((1,), (1,)), ((), ())), preferred_element_type=F32)


def _swiglu_half(xn, wg_ref, wu_ref, wd_ref):
    w = wg_ref.shape[1] // FFN_SPLIT
    acc = None
    for c in range(FFN_SPLIT):
        sl = slice(c * w, (c + 1) * w)
        gate = _dot(xn, wg_ref[:, sl])
        up = _dot(xn, wu_ref[:, sl])
        act = (gate * jax.nn.sigmoid(gate)) * (up * FFN_HALF)
        part = _dot(act.astype(BF16), wd_ref[sl, :])
        acc = part if acc is None else acc + part
    return acc


def _ffn_kernel(h_ref, g_ref, wg_ref, wu_ref, wd_ref, o_ref, xn_ref):
    j = pl.program_id(1)

    @pl.when(j == 0)
    def _():
        h = h_ref[...]
        xn = _rms(h, g_ref[...]).astype(BF16)
        xn_ref[...] = xn
        o_ref[...] = h + _swiglu_half(xn, wg_ref, wu_ref, wd_ref)

    @pl.when(j > 0)
    def _():
        o_ref[...] += _swiglu_half(xn_ref[...], wg_ref, wu_ref, wd_ref)


def _ffn(h, g, wg, wu, wd, layer, tm, tf):
    M, D = h.shape
    F = wg.shape[2]
    return pl.pallas_call(
        _ffn_kernel,
        grid=(M // tm, F // tf),
        in_specs=[
            _row_block(tm, D),
            pl.BlockSpec((1, D), lambda i, j: (0, 0)),
            pl.BlockSpec((None, D, tf), lambda i, j: (layer, 0, j)),
            pl.BlockSpec((None, D, tf), lambda i, j: (layer, 0, j)),
            pl.BlockSpec((None, tf, D), lambda i, j: (layer, j, 0)),
        ],
        out_specs=pl.BlockSpec((tm, D), lambda i, j: (i, 0)),
        out_shape=jax.ShapeDtypeStruct((M, D), F32),
        scratch_shapes=[pltpu.VMEM((tm, D), BF16)],
        compiler_params=_params("parallel", "arbitrary"),
        name="ffn",
    )(h, g.reshape(1, D), wg, wu, wd)


def _ffn_cast_kernel(h_ref, g_ref, wg_ref, wu_ref, wd_ref, o_ref, wgb_ref, wub_ref, wdb_ref, xn_ref):
    wgb_ref[...] = wg_ref[...].astype(BF16)
    wub_ref[...] = wu_ref[...].astype(BF16)
    wdb_ref[...] = wd_ref[...].astype(BF16)

    @pl.when(pl.program_id(0) == 0)
    def _():
        h = h_ref[...]
        xn_ref[...] = _rms(h, g_ref[...]).astype(BF16)
        o_ref[...] = h

    o_ref[...] += _swiglu_half(xn_ref[...], wgb_ref, wub_ref, wdb_ref)


def _ffn_cast(h, g, wg, wu, wd, layer, tf):
    M, D = h.shape
    F = wg.shape[2]
    return pl.pallas_call(
        _ffn_cast_kernel,
        grid=(F // tf,),
        in_specs=[
            pl.BlockSpec((M, D), lambda j: (0, 0)),
            pl.BlockSpec((1, D), lambda j: (0, 0)),
            pl.BlockSpec((None, D, tf), lambda j: (layer, 0, j)),
            pl.BlockSpec((None, D, tf), lambda j: (layer, 0, j)),
            pl.BlockSpec((None, tf, D), lambda j: (layer, j, 0)),
        ],
        out_specs=[
            pl.BlockSpec((M, D), lambda j: (0, 0)),
            pl.BlockSpec((D, tf), lambda j: (0, j)),
            pl.BlockSpec((D, tf), lambda j: (0, j)),
            pl.BlockSpec((tf, D), lambda j: (j, 0)),
        ],
        out_shape=[
            jax.ShapeDtypeStruct((M, D), F32),
            jax.ShapeDtypeStruct((D, F), BF16),
            jax.ShapeDtypeStruct((D, F), BF16),
            jax.ShapeDtypeStruct((F, D), BF16),
        ],
        scratch_shapes=[pltpu.VMEM((M, D), BF16)],
        compiler_params=_params("arbitrary"),
        name="ffn_meta_cast",
    )(h, g.reshape(1, D), wg, wu, wd)


def _inproj_m_kernel(h_ref, g_ref, w_ref, ws_ref, p_ref, ps_ref, xn_ref):
    j = pl.program_id(1)

    @pl.when(j == 0)
    def _():
        xn = _rms(h_ref[...], g_ref[...]).astype(BF16)
        xn_ref[...] = xn
        ps_ref[...] = _dot(xn, ws_ref[...])
        p_ref[...] = _dot(xn, w_ref[...]).astype(BF16)

    @pl.when(j > 0)
    def _():
        p_ref[...] = _dot(xn_ref[...], w_ref[...]).astype(BF16)


def _inproj_m(h, g, w, ws, tm, tn):
    M, D = h.shape
    N = w.shape[1]
    return pl.pallas_call(
        _inproj_m_kernel,
        grid=(M // tm, N // tn),
        in_specs=[
            _row_block(tm, D),
            pl.BlockSpec((1, D), lambda i, j: (0, 0)),
            pl.BlockSpec((D, tn), lambda i, j: (0, j)),
            pl.BlockSpec((D, LANES), lambda i, j: (0, 0)),
        ],
        out_specs=[
            pl.BlockSpec((tm, tn), lambda i, j: (i, j)),
            pl.BlockSpec((tm, LANES), lambda i, j: (i, 0)),
        ],
        out_shape=[jax.ShapeDtypeStruct((M, N), BF16), jax.ShapeDtypeStruct((M, LANES), F32)],
        scratch_shapes=[pltpu.VMEM((tm, D), BF16)],
        compiler_params=_params("parallel", "arbitrary"),
        name="mlstm_inproj",
    )(h, g.reshape(1, D), w, ws)


def _log_sigmoid(x):
    return jnp.minimum(x, 0.0) - jnp.log1p(jnp.exp(-jnp.abs(x)))


def _mlstm_kernel(q_ref, k_ref, v_ref, o_ref, ps_ref, bias_ref, gh_ref, c0_ref, n0_ref, m0_ref,
                  y_ref, c_ref, n_ref, m_ref, *, L, npad, H, DK, DV):
    @pl.when(pl.program_id(1) == 0)
    def _():
        c_ref[...] = c0_ref[...]
        n_ref[...] = n0_ref[...]
        m_ref[...] = m0_ref[...]

    gates = ps_ref[...] + bias_ref[...]
    logf = _log_sigmoid(gates)
    if npad:
        valid = lax.broadcasted_iota(jnp.int32, (L, 1), 0) >= npad
        logf = jnp.where(valid, logf, 0.0)
        gates = jnp.where(valid, gates, LOG_I_PAD)
    ii = lax.broadcasted_iota(jnp.int32, (L, L), 0)
    jj = lax.broadcasted_iota(jnp.int32, (L, L), 1)
    causal = jj <= ii
    cum = jnp.dot(causal.astype(F32), logf, preferred_element_type=F32,
                  precision=lax.Precision.HIGHEST)
    cum_t = cum.T
    gates_t = gates.T

    updates = []
    for h in range(H):
        b_col = cum[:, H + h:H + h + 1]
        b_row = cum_t[H + h:H + h + 1, :]
        li_col = gates[:, h:h + 1]
        li_row = gates_t[h:h + 1, :]
        m_prev = m_ref[h][:, 0:1]
        q = q_ref[:, h * DK:(h + 1) * DK] * (DK ** -0.5)
        k = k_ref[:, h * DK:(h + 1) * DK]
        v = v_ref[:, h * DV:(h + 1) * DV]
        c_old = c_ref[h]
        n_old = n_ref[h]

        dm = jnp.where(causal, b_col - b_row + li_row, -jnp.inf)
        inter = b_col + m_prev
        m_t = jnp.maximum(inter, jnp.max(dm, axis=-1, keepdims=True))
        decay_mat = jnp.exp(dm - m_t)
        w_inter = jnp.exp(inter - m_t)
        b_last = b_col[L - 1:L, :]
        g_row = b_last - b_row + li_row
        g_col = b_last - b_col + li_col
        m_new = jnp.maximum(b_last + m_prev, jnp.max(g_row, axis=-1, keepdims=True))
        decay = jnp.exp(b_last + m_prev - m_new)
        updates.append((m_new, decay, jnp.exp(g_col - m_new)))

        s = _dot_nt(q, k) * decay_mat
        num = w_inter * _dot(q, c_old.astype(BF16)) + _dot(s.astype(BF16), v)
        qn = jnp.sum(q.astype(F32) * n_old, axis=-1, keepdims=True)
        den = w_inter * qn + jnp.sum(s, axis=-1, keepdims=True)
        hout = num / jnp.maximum(jnp.abs(den), jnp.exp(-m_t))

        hn = _rms(hout, gh_ref[:, h * DV:(h + 1) * DV])
        og = jax.nn.sigmoid(o_ref[:, h * DV:(h + 1) * DV].astype(F32))
        y_ref[:, h * DV:(h + 1) * DV] = (og * hn).astype(BF16)

    for h, (m_new, decay, wk) in enumerate(updates):
        k = k_ref[:, h * DK:(h + 1) * DK]
        v = v_ref[:, h * DV:(h + 1) * DV]
        kw = k.astype(F32) * wk
        c_ref[h] = decay * c_ref[h] + _dot(kw.T.astype(BF16), v)
        n_ref[h] = decay * n_ref[h] + jnp.sum(kw, axis=0, keepdims=True)
        m_ref[h] = jnp.broadcast_to(m_new, (1, LANES))


def _mlstm(p, ps, bias, gh, c0, n0, m0, B, L, npad):
    M = p.shape[0]
    H = M_HEADS
    DK, DV = c0.shape[1], c0.shape[2]
    NC = M // (B * L)
    row = lambda b, c: b * NC + c
    qk_w, vo_w = H * DK, H * DV
    assert vo_w == 2 * qk_w
    kern = functools.partial(_mlstm_kernel, L=L, npad=npad, H=H, DK=DK, DV=DV)
    return pl.pallas_call(
        kern,
        grid=(B, NC),
        in_specs=[
            pl.BlockSpec((L, qk_w), lambda b, c: (row(b, c), 0)),
            pl.BlockSpec((L, qk_w), lambda b, c: (row(b, c), 1)),
            pl.BlockSpec((L, vo_w), lambda b, c: (row(b, c), 1)),
            pl.BlockSpec((L, vo_w), lambda b, c: (row(b, c), 2)),
            pl.BlockSpec((L, LANES), lambda b, c: (row(b, c), 0)),
            pl.BlockSpec((1, LANES), lambda b, c: (0, 0)),
            pl.BlockSpec((1, vo_w), lambda b, c: (0, 0)),
            pl.BlockSpec((H, DK, DV), lambda b, c: (0, 0, 0)),
            pl.BlockSpec((H, 1, DK), lambda b, c: (0, 0, 0)),
            pl.BlockSpec((H, 1, LANES), lambda b, c: (0, 0, 0)),
        ],
        out_specs=[
            pl.BlockSpec((L, vo_w), lambda b, c: (row(b, c), 0)),
            pl.BlockSpec((None, H, DK, DV), lambda b, c: (b, 0, 0, 0)),
            pl.BlockSpec((None, H, 1, DK), lambda b, c: (b, 0, 0, 0)),
            pl.BlockSpec((None, H, 1, LANES), lambda b, c: (b, 0, 0, 0)),
        ],
        out_shape=[
            jax.ShapeDtypeStruct((M, vo_w), BF16),
            jax.ShapeDtypeStruct((B, H, DK, DV), F32),
            jax.ShapeDtypeStruct((B, H, 1, DK), F32),
            jax.ShapeDtypeStruct((B, H, 1, LANES), F32),
        ],
        compiler_params=_params("parallel", "arbitrary"),
        name="mlstm",
    )(p, p, p, p, ps, bias, gh, c0, n0, m0)


def _outproj_kernel(y_ref, w_ref, h_ref, o_ref):
    o_ref[...] = h_ref[...] + _dot(y_ref[...], w_ref[...])


def _outproj(y, w, h, tm):
    M, D = h.shape
    K = y.shape[1]
    return pl.pallas_call(
        _outproj_kernel,
        grid=(M // tm,),
        in_specs=[
            pl.BlockSpec((tm, K), lambda i: (i, 0)),
            pl.BlockSpec((K, D), lambda i: (0, 0), pipeline_mode=pl.Buffered(1)),
            pl.BlockSpec((tm, D), lambda i: (i, 0)),
        ],
        out_specs=pl.BlockSpec((tm, D), lambda i: (i, 0)),
        out_shape=jax.ShapeDtypeStruct((M, D), F32),
        compiler_params=_params("parallel"),
        name="outproj",
    )(y, w, h)


def _rope(x, tab, half):
    c = tab[:, 0:LANES]
    s1 = tab[:, LANES:2 * LANES]
    s2 = tab[:, 2 * LANES:3 * LANES]
    return x * c + pltpu.roll(x, LANES - half, 1) * s1 + pltpu.roll(x, half, 1) * s2


def _inproj_a_kernel(h_ref, g_ref, w_ref, ws_ref, gq_ref, gk_ref, t128_ref, t64_ref, tki_ref,
                     p_ref, pqi_ref, ps_ref, xn_ref, acc_ref, *, nq_tiles, tn, half128, half64):
    j = pl.program_id(1)
    nh = tn // LANES
    n_tiles = nq_tiles + 2

    def epilogue(tile, acc, p_ref):
        if tile < nq_tiles:
            t = t128_ref[...]
            for c in range(nh):
                xs = _rms(acc[:, c * LANES:(c + 1) * LANES], gq_ref[...])
                p_ref[:, c * LANES:(c + 1) * LANES] = _rope(xs, t, half128).astype(BF16)
        elif tile == nq_tiles:
            t = t128_ref[...]
            for c in range(nh // 2):
                xs = _rms(acc[:, c * LANES:(c + 1) * LANES], gk_ref[...])
                p_ref[:, c * LANES:(c + 1) * LANES] = _rope(xs, t, half128).astype(BF16)
            p_ref[:, tn // 2:] = acc[:, tn // 2:].astype(BF16)
        else:
            t = t64_ref[...]
            for c in range(nh):
                p_ref[:, c * LANES:(c + 1) * LANES] = _rope(acc[:, c * LANES:(c + 1) * LANES], t, half64).astype(BF16)

    for s in range(n_tiles):
        @pl.when(j == s)
        def _(s=s):
            if s == 0:
                xn = _rms(h_ref[...], g_ref[...]).astype(BF16)
                xn_ref[...] = xn
                ps_ref[...] = _rope(_dot(xn, ws_ref[...]), tki_ref[...], half64)
                acc_ref[0] = _dot(xn, w_ref[...])
            else:
                acc_ref[s % 2] = _dot(xn_ref[...], w_ref[...])
                epilogue(s - 1, acc_ref[(s - 1) % 2], p_ref)
            if s == n_tiles - 1:
                epilogue(s, acc_ref[s % 2], pqi_ref)


def _inproj_a(h, g, w, ws, gq, gk, t128, t64, tki, tm):
    M, D = h.shape
    N = w.shape[1]
    tn = PROJ_COL_TILE
    hd = D // A_HEADS
    assert hd == LANES and D // 4 * 2 == tn and N == D + 2 * tn
    nt = t128.shape[0] // tm
    n_tiles = N // tn
    kern = functools.partial(_inproj_a_kernel, nq_tiles=D // tn, tn=tn,
                             half128=hd // ROPE_FRAC // 2, half64=IDX_DIM // ROPE_FRAC // 2)
    tab = pl.BlockSpec((tm, 3 * LANES), lambda i, j: (i % nt, 0))
    return pl.pallas_call(
        kern,
        grid=(M // tm, n_tiles),
        in_specs=[
            _row_block(tm, D),
            pl.BlockSpec((1, D), lambda i, j: (0, 0)),
            pl.BlockSpec((D, tn), lambda i, j: (0, j)),
            pl.BlockSpec((D, LANES), lambda i, j: (0, 0)),
            pl.BlockSpec((1, LANES), lambda i, j: (0, 0)),
            pl.BlockSpec((1, LANES), lambda i, j: (0, 0)),
            tab, tab, tab,
        ],
        out_specs=[
            pl.BlockSpec((tm, tn), lambda i, j: (i, jnp.maximum(j - 1, 0))),
            pl.BlockSpec((tm, tn), lambda i, j: (i, 0)),
            pl.BlockSpec((tm, LANES), lambda i, j: (i, 0)),
        ],
        out_shape=[jax.ShapeDtypeStruct((M, N - tn), BF16), jax.ShapeDtypeStruct((M, tn), BF16),
                   jax.ShapeDtypeStruct((M, LANES), F32)],
        scratch_shapes=[pltpu.VMEM((tm, D), BF16), pltpu.VMEM((2, tm, tn), F32)],
        compiler_params=_params("parallel", "arbitrary"),
        name="dsa_inproj",
    )(h, g.reshape(1, D), w, ws, gq, gk, t128, t64, tki)


def _rope_table(pos, d, extra_scale_lanes=None, extra_scale=1.0, reps=None):
    rot = d // ROPE_FRAC
    half = rot // 2
    inv = 1.0 / (ROPE_THETA ** (jnp.arange(0, rot, 2, dtype=F32) / rot))
    ang = pos[:, None] * inv[None, :]
    cos, sin = jnp.cos(ang), jnp.sin(ang)
    T = pos.shape[0]
    one = jnp.ones((T, d - rot), F32)
    zero = lambda n: jnp.zeros((T, n), F32)
    c = jnp.concatenate([cos, cos, one], axis=1)
    s1 = jnp.concatenate([-sin, zero(d - half)], axis=1)
    s2 = jnp.concatenate([zero(half), sin, zero(d - rot)], axis=1)
    if reps is None:
        reps = LANES // d
    c, s1, s2 = (jnp.tile(a, (1, reps)) for a in (c, s1, s2))
    fill = LANES - reps * d
    if fill:
        tail = jnp.ones((T, fill), F32)
        if extra_scale_lanes is not None:
            lo, hi = extra_scale_lanes
            lane = jnp.arange(reps * d, LANES)
            tail = jnp.where((lane >= lo) & (lane < hi), extra_scale, 1.0)[None, :] * tail
        c = jnp.concatenate([c, tail], axis=1)
        s1 = jnp.concatenate([s1, zero(fill)], axis=1)
        s2 = jnp.concatenate([s2, zero(fill)], axis=1)
    return jnp.concatenate([c, s1, s2], axis=1)


def _float_key(x):
    bits = lax.bitcast_convert_type(x, jnp.int32)
    return jnp.where(bits < 0, bits ^ jnp.int32(0x7FFFFFFF), bits)


def _dsa_keys(qi_ref, psq_ref, kie, kio, key_ref, *, tq, row0, nk, n_meta):
    wi = psq_ref[...]
    ke = kie[0:nk, :]
    ko = kio[0:nk, :]
    score = jnp.zeros((tq, nk), F32)
    for p in range(IDX_HEADS // 2):
        qp = qi_ref[:, p * LANES:(p + 1) * LANES]
        c0 = IDX_DIM + 2 * p
        score += jnp.maximum(_dot_nt(qp, ke), 0.0) * wi[:, c0:c0 + 1]
        score += jnp.maximum(_dot_nt(qp, ko), 0.0) * wi[:, c0 + 1:c0 + 2]
    col = lax.broadcasted_iota(jnp.int32, (tq, nk), 1)
    row = lax.broadcasted_iota(jnp.int32, (tq, nk), 0) + row0
    adm = jnp.where(col < LANES, col - n_meta, col - LANES - row - 1) < 0
    score = jnp.where(score == 0.0, 0.0, score)
    key_ref[:, 0:nk] = _float_key(jnp.where(adm, score, -jnp.inf))


def _count_ge(key_ref, cand, r0, rows, nch):
    acc = jnp.zeros((rows, LANES), F32)
    for c in range(nch):
        acc += jnp.where(key_ref[r0:r0 + rows, c * LANES:(c + 1) * LANES] >= cand, 1.0, 0.0)
    return jnp.sum(acc, axis=-1, keepdims=True)


def _search_trip(key_ref, trip, bases, *, tq, nk, topk):
    rb = tq // SEARCH_ROW_BLOCKS
    bases = list(bases)
    for u in range(SEARCH_UNROLL):
        bit = jnp.int32(31) - (trip.astype(jnp.int32) * SEARCH_UNROLL + u)
        step = lax.shift_left(jnp.int32(1), bit)
        for blk in range(SEARCH_ROW_BLOCKS):
            cand = bases[blk] + step
            cnt = _count_ge(key_ref, cand, blk * rb, rb, nk // LANES)
            bases[blk] = jnp.where(cnt >= float(topk), cand, bases[blk])
    return tuple(bases)


def _search_init(tq):
    return tuple(jnp.full((tq // SEARCH_ROW_BLOCKS, 1), INT_MIN, jnp.int32) for _ in range(SEARCH_ROW_BLOCKS))


def _dsa_mask(key_ref, bias_ref, bases, *, tq, nk, topk):
    NCH = nk // LANES
    kf = float(topk)
    ones = jnp.ones((LANES, LANES), BF16)
    thr = jnp.maximum(jnp.concatenate(bases, axis=0), NEG_INF_KEY + 1)
    cnt = _count_ge(key_ref, thr, 0, tq, NCH)
    for c in range(NCH):
        sl = slice(c * LANES, (c + 1) * LANES)
        bias_ref[:, sl] = jnp.where(key_ref[:, sl] >= thr, 0.0, -jnp.inf)

    @pl.when(jnp.max(cnt) > kf)
    def _():
        need = kf - _count_ge(key_ref, thr + 1, 0, tq, NCH)
        rr = lax.broadcasted_iota(jnp.int32, (LANES, LANES), 0)
        cc = lax.broadcasted_iota(jnp.int32, (LANES, LANES), 1)
        tri = jnp.where(rr <= cc, 1.0, 0.0).astype(BF16)
        run = jnp.zeros((tq, LANES), F32)
        for c in range(NCH):
            sl = slice(c * LANES, (c + 1) * LANES)
            kc = key_ref[:, sl]
            eq = kc == thr
            eqb = jnp.where(eq, 1.0, 0.0).astype(BF16)
            rank = run + _dot(eqb, tri)
            keep_eq = jnp.where(rank <= need, 0.0, -jnp.inf)
            bias_ref[:, sl] = jnp.where(kc > thr, 0.0, jnp.where(eq, keep_eq, -jnp.inf))
            run = run + _dot(eqb, ones)


def _attn_heads(qs, kcat, vcat, bias_ref, os, trip, *, nk):
    hd = LANES
    per_group = (A_HEADS // A_KV_HEADS) // ATT_HEADS_PER_TRIP
    g = trip // per_group
    kg = kcat[g, 0:nk, :]
    vg = vcat[g, 0:nk, :]
    heads = [trip * ATT_HEADS_PER_TRIP + r for r in range(ATT_HEADS_PER_TRIP)]
    probs = []
    for h in heads:
        logits = _dot_nt(qs[h], kg) + bias_ref[:, 0:nk]
        mx = jnp.max(logits, axis=-1, keepdims=True)
        probs.append(jnp.exp2(logits - mx).astype(BF16))
    for h, p in zip(heads, probs):
        pv = _dot(p, vg)
        os[h] = (pv[:, 0:hd] / pv[:, hd:hd + 1]).astype(BF16)


def _dsa_step(i, q_ref, qi_ref, psq_ref, o_ref, kcat, vcat, kie, kio, key_ref,
              bias_ref, qs, os, *, tq, nk, topk, n_meta):
    hd = LANES
    _dsa_keys(qi_ref, psq_ref, kie, kio, key_ref, tq=tq, row0=i * tq, nk=nk, n_meta=n_meta)
    bases = lax.fori_loop(0, 32 // SEARCH_UNROLL,
                          lambda t, b: _search_trip(key_ref, t, b, tq=tq, nk=nk, topk=topk), _search_init(tq))
    _dsa_mask(key_ref, bias_ref, bases, tq=tq, nk=nk, topk=topk)

    for h in range(A_HEADS):
        qs[h] = q_ref[:, h * hd:(h + 1) * hd]

    def trip(t, carry):
        _attn_heads(qs, kcat, vcat, bias_ref, os, t, nk=nk)
        return carry

    lax.fori_loop(0, A_HEADS // ATT_HEADS_PER_TRIP, trip, 0)
    for h in range(A_HEADS):
        o_ref[:, h * hd:(h + 1) * hd] = os[h]


def _dsa_kernel(q_ref, qi_ref, psq_ref, k_ref, v_ref, psk_ref, km_ref, vm_ref, psm_ref, o_ref,
                kcat, vcat, kie, kio, key_ref, bias_ref, qs, os, *, tq, S, topk, n_meta):
    i = pl.program_id(1)
    NK = LANES + S
    hd = LANES
    G = A_KV_HEADS
    nq = S // tq

    @pl.when(i == 0)
    def _():
        lane = lax.broadcasted_iota(jnp.int32, (NK, hd), 1)
        ones_col = jnp.where(lane == 0, 1.0, 0.0).astype(BF16)
        for g in range(G):
            sl = slice(g * hd, (g + 1) * hd)
            kcat[g, 0:LANES, :] = km_ref[:, sl]
            kcat[g, LANES:, :] = k_ref[:, sl]
            vcat[g, 0:LANES, 0:hd] = vm_ref[:, sl]
            vcat[g, LANES:, 0:hd] = v_ref[:, sl]
            vcat[g, :, hd:] = ones_col
        for dst0, dst1, src in ((0, LANES, psm_ref), (LANES, NK, psk_ref)):
            a = src[...]
            lane = lax.broadcasted_iota(jnp.int32, a.shape, 1)
            even = jnp.where(lane < IDX_DIM, a, 0.0)
            kie[dst0:dst1, :] = even.astype(BF16)
            kio[dst0:dst1, :] = pltpu.roll(even, IDX_DIM, 1).astype(BF16)

    bounds = sorted({-(-nq * (v + 1) // DSA_VARIANTS) for v in range(DSA_VARIANTS)})
    lo = 0
    for hi in bounds:
        pl.when((i >= lo) & (i < hi))(functools.partial(
            _dsa_step, i, q_ref, qi_ref, psq_ref, o_ref, kcat, vcat, kie, kio,
            key_ref, bias_ref, qs, os, tq=tq, nk=LANES + hi * tq, topk=topk, n_meta=n_meta))
        lo = hi


def _dsa(p, pqi, ps, pm, psm, B, S, tq, topk):
    M, N = p.shape
    D = N * 2 // 3
    kv_w = D // 4
    nq = S // tq
    NK = LANES + S
    qi_w = IDX_HEADS * IDX_DIM
    kern = functools.partial(_dsa_kernel, tq=tq, S=S, topk=topk, n_meta=N_META)
    return pl.pallas_call(
        kern,
        grid=(B, nq),
        in_specs=[
            pl.BlockSpec((tq, D), lambda b, i: (b * nq + i, 0)),
            pl.BlockSpec((tq, qi_w), lambda b, i: (b * nq + i, 0)),
            pl.BlockSpec((tq, LANES), lambda b, i: (b * nq + i, 0)),
            pl.BlockSpec((S, kv_w), lambda b, i: (b, D // kv_w)),
            pl.BlockSpec((S, kv_w), lambda b, i: (b, D // kv_w + 1)),
            pl.BlockSpec((S, LANES), lambda b, i: (b, 0)),
            pl.BlockSpec((LANES, kv_w), lambda b, i: (0, D // kv_w)),
            pl.BlockSpec((LANES, kv_w), lambda b, i: (0, D // kv_w + 1)),
            pl.BlockSpec((LANES, LANES), lambda b, i: (0, 0)),
        ],
        out_specs=pl.BlockSpec((tq, D), lambda b, i: (b * nq + i, 0)),
        out_shape=jax.ShapeDtypeStruct((M, D), BF16),
        scratch_shapes=[
            pltpu.VMEM((A_KV_HEADS, NK, LANES), BF16),
            pltpu.VMEM((A_KV_HEADS, NK, 2 * LANES), BF16),
            pltpu.VMEM((NK, LANES), BF16),
            pltpu.VMEM((NK, LANES), BF16),
            pltpu.VMEM((tq, NK), jnp.int32),
            pltpu.VMEM((tq, NK), F32),
            pltpu.VMEM((A_HEADS, tq, LANES), BF16),
            pltpu.VMEM((A_HEADS, tq, LANES), BF16),
        ],
        compiler_params=_params("parallel", "arbitrary"),
        name="dsa_attention",
    )(p, pqi, ps, p, p, ps, pm, pm, psm)


def kernel(x, meta_tokens, ffn1_norm, ffn1_w_gate, ffn1_w_up, ffn1_w_down, mix_norm, ffn2_norm, ffn2_w_gate, ffn2_w_up, ffn2_w_down, mlstm_w_in, mlstm_b_i, mlstm_b_f, mlstm_head_norm, mlstm_w_out, dsa_w_in, dsa_q_norm, dsa_k_norm, dsa_w_out):
    B, S, D = x.shape
    depth = ffn1_norm.shape[0]
    assert depth == 2 and meta_tokens.shape == (N_META, D)
    F = ffn1_w_gate.shape[-1]
    H, DK, DV = M_HEADS, D // (2 * M_HEADS), D // M_HEADS
    hd = D // A_HEADS
    L, TM, TMO, TF = MLSTM_CHUNK, ROW_TILE, OUT_ROW_TILE, FF_TILE
    assert S % L == 0 and S % TM == 0 and (B * S) % TMO == 0 and F % TF == 0

    hx = x.reshape(B * S, D)
    hm = meta_tokens.astype(x.dtype)
    bf = lambda w: w.astype(BF16)

    f1 = (ffn1_w_gate, ffn1_w_up, ffn1_w_down)
    f2 = (ffn2_w_gate, ffn2_w_up, ffn2_w_down)

    def ffn(hx, hm, g, ws, layer):
        hm, wg, wu, wd = _ffn_cast(hm, g, *ws, layer, TF)
        return _ffn(hx, g, wg[None], wu[None], wd[None], 0, TM, TF), hm

    def pad_cols(w):
        return jnp.pad(w, ((0, 0), (0, LANES - w.shape[1])))

    def pad_rows(a, n, front=False):
        r = n - a.shape[0]
        return jnp.pad(a, ((r, 0) if front else (0, r), (0, 0)))

    hx, hm = ffn(hx, hm, ffn1_norm[0], f1, 0)

    n_wide = 2 * H * DK + 2 * H * DV
    w_wide = bf(mlstm_w_in[0][:, :n_wide])
    w_gate = bf(pad_cols(mlstm_w_in[0][:, n_wide:]))
    px, psx = _inproj_m(hx, mix_norm[0], w_wide, w_gate, TM, MLSTM_PROJ_COL_TILE)
    pm, psm = _inproj_m(hm, mix_norm[0], w_wide, w_gate, N_META, MLSTM_PROJ_COL_TILE)

    bias = pad_cols(jnp.concatenate([mlstm_b_i[0], mlstm_b_f[0]]).astype(F32)[None, :])
    gh = mlstm_head_norm[0].astype(F32).reshape(1, H * DV)
    zc = jnp.zeros((H, DK, DV), F32)
    zn = jnp.zeros((H, 1, DK), F32)
    zm = jnp.zeros((H, 1, LANES), F32)
    ym, c0, n0, m0 = _mlstm(pad_rows(pm, L, True), pad_rows(psm, L, True), bias, gh, zc, zn, zm,
                            1, L, L - N_META)
    yx, _, _, _ = _mlstm(px, psx, bias, gh, c0[0], n0[0], m0[0], B, L, 0)
    w_out = bf(mlstm_w_out[0])
    hx = _outproj(yx, w_out, hx, TMO)
    hm = _outproj(ym[L - N_META:], w_out, hm, N_META)

    hx, hm = ffn(hx, hm, ffn2_norm[0], f2, 0)

    hx, hm = ffn(hx, hm, ffn1_norm[1], f1, 1)

    n_wide = A_HEADS * hd + 2 * A_KV_HEADS * hd + IDX_HEADS * IDX_DIM
    w_wide = bf(dsa_w_in[0][:, :n_wide])
    w_idx = bf(pad_cols(dsa_w_in[0][:, n_wide:]))
    gq = (dsa_q_norm[0].astype(F32) * (hd ** -0.5 * float(np.log2(np.e))))[None, :]
    gk = dsa_k_norm[0].astype(F32)[None, :]
    wi_scale = IDX_HEADS ** -0.5 * IDX_DIM ** -0.5
    pos = jnp.arange(N_META + S, dtype=F32)
    t128 = _rope_table(pos, hd)
    t64 = _rope_table(pos, IDX_DIM)
    tki = _rope_table(pos, IDX_DIM, (IDX_DIM, IDX_DIM + IDX_HEADS), wi_scale, reps=1)
    tabs_x = [t[N_META:] for t in (t128, t64, tki)]
    tabs_m = [t[:N_META] for t in (t128, t64, tki)]
    px, pqx, psx = _inproj_a(hx, mix_norm[1], w_wide, w_idx, gq, gk, *tabs_x, TM)
    pm, _, psm = _inproj_a(hm, mix_norm[1], w_wide, w_idx, gq, gk, *tabs_m, N_META)

    topk = min(TOPK_MAX, (N_META + S - N_META) // 4)
    ox = _dsa(px, pqx, psx, pad_rows(pm, LANES), pad_rows(psm, LANES), B, S, DSA_Q_TILE, topk)
    hx = _outproj(ox, bf(dsa_w_out[0]), hx, TMO)

    hx, _ = ffn(hx, hm, ffn2_norm[1], f2, 1)
    return hx.reshape(B, S, D)
```
